```python
import jax, jax.numpy as jnp
from jax import lax
import numpy as np

D_MODEL = 1024
BATCH = 8
SEQ = 2048
DEPTH = 2
DEC_BATCH = 128
DEC_SEQ = 8
PAST_LEN = 16384
PAGE_SIZE = 128

N_MIXERS = 2
N_LAYERS_A = (DEPTH + 1) // 2
N_LAYERS_B = DEPTH // 2
A_HEADS = 8
A_KEY = 128
A_VAL = D_MODEL // A_HEADS
B_HEADS = 4
B_KEY = D_MODEL // 2 // B_HEADS
B_VAL = D_MODEL // B_HEADS
B_GATE_RANK = 16
B_GATE_NORMALIZER = 16.0
CHUNK = 16
MEM_LEN = 256
X_HEADS = 4
X_HEAD_DIM = D_MODEL // X_HEADS
N_EXPERTS = 16
N_GROUPS = 4
EXPERTS_PER_GROUP = N_EXPERTS // N_GROUPS
TOP_K = 2
D_EXPERT = 512
DN_ALPHA = (2 * DEPTH) ** 0.25
DN_BETA = (8 * DEPTH) ** -0.25
LN_EPS = 1e-5
RMS_EPS = 1e-6

A_IN = 2 * A_HEADS * A_KEY + A_HEADS * A_VAL + D_MODEL
B_IN = 2 * B_HEADS * B_KEY + B_HEADS * B_VAL + D_MODEL + B_GATE_RANK

kernel_name = "hgrn2_gla_memxattn_grouped_moe_step"

F32 = jnp.float32


def chunked_gated_linear_attention(q, k, v, g, s0):
    B, T, H, K = q.shape
    V = v.shape[-1]
    c = min(CHUNK, T)
    n = -(-T // c)
    pad = n * c - T
    if pad:
        pw = ((0, 0), (0, pad), (0, 0), (0, 0))
        q, k, v, g = (jnp.pad(a, pw) for a in (q, k, v, g))
    qc = q.reshape(B, n, c, H, K).transpose(0, 1, 3, 2, 4)
    kc = k.reshape(B, n, c, H, K).transpose(0, 1, 3, 2, 4)
    gc = g.reshape(B, n, c, H, K).transpose(0, 1, 3, 2, 4)
    vc = v.reshape(B, n, c, H, V).transpose(0, 1, 3, 2, 4)
    b = jnp.cumsum(gc, axis=3)
    b_mid = b[:, :, :, c // 2:c // 2 + 1, :]
    scores = jnp.einsum('bnhik,bnhjk->bnhij', qc * jnp.exp(b - b_mid), kc * jnp.exp(b_mid - b))
    causal = jnp.tril(jnp.ones((c, c), dtype=bool))
    scores = jnp.where(causal, scores, 0.0)
    o_intra = jnp.einsum('bnhij,bnhjv->bnhiv', scores, vc)
    q_inter = qc * jnp.exp(b)
    b_last = b[:, :, :, -1, :]
    k_state = kc * jnp.exp(b_last[:, :, :, None, :] - b)

    def step(S, inp):
        q_i, k_i, v_i, bl = inp
        o = jnp.einsum('bhck,bhkv->bhcv', q_i, S)
        S = jnp.exp(bl)[..., None] * S + jnp.einsum('bhck,bhcv->bhkv', k_i, v_i)
        return S, o

    S, o_inter = lax.scan(step, s0,
                          (jnp.moveaxis(q_inter, 1, 0), jnp.moveaxis(k_state, 1, 0),
                           jnp.moveaxis(vc, 1, 0), jnp.moveaxis(b_last, 1, 0)))
    o = o_intra + jnp.moveaxis(o_inter, 0, 1)
    o = o.transpose(0, 1, 3, 2, 4).reshape(B, n * c, H, V)[:, :T]
    return o, S


def gated_rmsnorm(o, gate, gain):
    o = o * lax.rsqrt(jnp.mean(o * o, axis=-1, keepdims=True) + RMS_EPS)
    return o * gain.astype(F32) * jax.nn.silu(gate.astype(F32))


def hgrn2_mixer(x, s0, w_in, lb, norm_g, w_out):
    B, T, _ = x.shape
    hk = A_HEADS * A_KEY
    proj = x @ w_in
    q, f, i, gate = jnp.split(proj, [hk, 2 * hk, 2 * hk + A_HEADS * A_VAL], axis=-1)
    q = jax.nn.silu(q.astype(F32)).reshape(B, T, A_HEADS, A_KEY) * (A_KEY ** -0.5)
    fg = lb + (1.0 - lb) * jax.nn.sigmoid(f.astype(F32))
    k = (1.0 - fg).reshape(B, T, A_HEADS, A_KEY)
    logf = jnp.log(fg).reshape(B, T, A_HEADS, A_KEY)
    v = i.astype(F32).reshape(B, T, A_HEADS, A_VAL)
    o, s = chunked_gated_linear_attention(q, k, v, logf, s0.astype(F32))
    o = gated_rmsnorm(o, gate.reshape(B, T, A_HEADS, A_VAL), norm_g)
    return o.reshape(B, T, A_HEADS * A_VAL).astype(x.dtype) @ w_out, s.astype(s0.dtype)


def gla_mixer(x, s0, w_in, w_gk2, b_gk2, norm_g, w_out):
    B, T, _ = x.shape
    hk = B_HEADS * B_KEY
    hv = B_HEADS * B_VAL
    proj = x @ w_in
    q, k, v, gate, lr = jnp.split(proj, [hk, 2 * hk, 2 * hk + hv, 2 * hk + hv + D_MODEL], axis=-1)
    gk = jax.nn.log_sigmoid((lr @ w_gk2 + b_gk2).astype(F32)) / B_GATE_NORMALIZER
    q = q.astype(F32).reshape(B, T, B_HEADS, B_KEY) * (B_KEY ** -0.5)
    k = k.astype(F32).reshape(B, T, B_HEADS, B_KEY)
    v = v.astype(F32).reshape(B, T, B_HEADS, B_VAL)
    o, s = chunked_gated_linear_attention(q, k, v, gk.reshape(B, T, B_HEADS, B_KEY), s0.astype(F32))
    o = gated_rmsnorm(o, gate.reshape(B, T, B_HEADS, B_VAL), norm_g)
    return o.reshape(B, T, hv).astype(x.dtype) @ w_out, s.astype(s0.dtype)


def memory_kv(mem, w_kv):
    B, M, _ = mem.shape
    kv = (mem @ w_kv).reshape(B, M, 2, X_HEADS, X_HEAD_DIM)
    return kv[:, :, 0], kv[:, :, 1]


def memory_cross_attention(x, mem_k, mem_v, w_q, w_o):
    B, T, _ = x.shape
    q = (x @ w_q).reshape(B, T, X_HEADS, X_HEAD_DIM)
    s = jnp.einsum('bthd,bmhd->bhtm', q, mem_k).astype(F32) * (X_HEAD_DIM ** -0.5)
    p = jax.nn.softmax(s, axis=-1).astype(x.dtype)
    o = jnp.einsum('bhtm,bmhd->bthd', p, mem_v).reshape(B, T, D_MODEL)
    return o @ w_o


def grouped_moe(x, router_w, router_bias, w_gate, w_up, w_down):
    scores = jax.nn.sigmoid((x @ router_w).astype(F32))
    sel = scores + router_bias.astype(F32)
    grouped = sel.reshape(*sel.shape[:-1], N_GROUPS, EXPERTS_PER_GROUP)
    group_score = lax.top_k(grouped, TOP_K)[0].sum(-1)
    best = jnp.argmax(group_score, axis=-1)
    in_group = (jnp.arange(N_EXPERTS) // EXPERTS_PER_GROUP) == best[..., None]
    _, idx = lax.top_k(jnp.where(in_group, sel, -jnp.inf), TOP_K)
    w = jnp.take_along_axis(scores, idx, axis=-1)
    w = w / jnp.sum(w, axis=-1, keepdims=True)
    combine = jnp.sum(jax.nn.one_hot(idx, N_EXPERTS, dtype=F32) * w[..., None], axis=-2)
    y = jnp.zeros(x.shape, F32)
    for e in range(N_EXPERTS):
        h = jax.nn.silu(x @ w_gate[e]) * (x @ w_up[e])
        y = y + combine[..., e:e + 1] * (h @ w_down[e]).astype(F32)
    return y.astype(x.dtype)


def post_norm(x, h, g, b):
    z = (DN_ALPHA * x + h).astype(F32)
    mu = jnp.mean(z, axis=-1, keepdims=True)
    var = jnp.mean(jnp.square(z - mu), axis=-1, keepdims=True)
    z = (z - mu) * lax.rsqrt(var + LN_EPS)
    return (z * g.astype(F32) + b.astype(F32)).astype(x.dtype)


def setup_inputs(seed: int = 0) -> dict:
    key = jax.random.key(seed)
    ks = iter(jax.random.split(key, 32))

    def nrm(shape, scale):
        return jax.random.normal(next(ks), shape, F32) * scale

    D = D_MODEL
    return {
        "x_prompt": nrm((BATCH, SEQ, D), 1.0),
        "x_sample": nrm((DEC_BATCH, DEC_SEQ, D), 1.0),
        "state_hgrn": nrm((N_LAYERS_A, DEC_BATCH, A_HEADS, A_KEY, A_VAL), 0.5),
        "state_gla": nrm((N_LAYERS_B, DEC_BATCH, B_HEADS, B_KEY, B_VAL), 1.0),
        "cache_mem_k": nrm((DEPTH, DEC_BATCH, MEM_LEN, X_HEADS, X_HEAD_DIM), 1.0),
        "cache_mem_v": nrm((DEPTH, DEC_BATCH, MEM_LEN, X_HEADS, X_HEAD_DIM), 1.0),
        "mem_prompt": nrm((BATCH, MEM_LEN, D), 1.0),
        "hgrn_w_in": nrm((N_LAYERS_A, D, A_IN), D ** -0.5),
        "hgrn_lb_logits": nrm((DEPTH + 1, A_HEADS * A_KEY), 0.1),
        "hgrn_norm_g": 1.0 + nrm((N_LAYERS_A, A_VAL), 0.02),
        "hgrn_w_out": nrm((N_LAYERS_A, A_HEADS * A_VAL, D), DN_BETA * (A_HEADS * A_VAL) ** -0.5),
        "gla_w_in": nrm((N_LAYERS_B, D, B_IN), D ** -0.5),
        "gla_w_gk2": nrm((N_LAYERS_B, B_GATE_RANK, B_HEADS * B_KEY), B_GATE_RANK ** -0.5),
        "gla_b_gk2": nrm((N_LAYERS_B, B_HEADS * B_KEY), 0.02),
        "gla_norm_g": 1.0 + nrm((N_LAYERS_B, B_VAL), 0.02),
        "gla_w_out": nrm((N_LAYERS_B, B_HEADS * B_VAL, D), DN_BETA * (B_HEADS * B_VAL) ** -0.5),
        "xattn_w_q": nrm((DEPTH, D, D), D ** -0.5),
        "xattn_w_kv": nrm((DEPTH, D, 2 * D), D ** -0.5),
        "xattn_w_o": nrm((DEPTH, D, D), DN_BETA * D ** -0.5),
        "router_w": nrm((D, N_EXPERTS), D ** -0.5),
        "router_bias": nrm((N_EXPERTS,), 0.01),
        "moe_w_gate": nrm((DEPTH, N_EXPERTS, D, D_EXPERT), D ** -0.5),
        "moe_w_up": nrm((DEPTH, N_EXPERTS, D, D_EXPERT), D ** -0.5),
        "moe_w_down": nrm((DEPTH, N_EXPERTS, D_EXPERT, D), DN_BETA * D_EXPERT ** -0.5),
        "ln_g": 1.0 + nrm((DEPTH, 3, D), 0.02),
        "ln_b": nrm((DEPTH, 3, D), 0.02),
    }


def reference(x_prompt, x_sample, state_hgrn, state_gla, cache_mem_k, cache_mem_v, mem_prompt,
              hgrn_w_in, hgrn_lb_logits, hgrn_norm_g, hgrn_w_out,
              gla_w_in, gla_w_gk2, gla_b_gk2, gla_norm_g, gla_w_out,
              xattn_w_q, xattn_w_kv, xattn_w_o,
              router_w, router_bias, moe_w_gate, moe_w_up, moe_w_down,
              ln_g, ln_b):
    lb_all = jnp.cumsum(jax.nn.softmax(hgrn_lb_logits.astype(F32), axis=0), axis=0)

    def run_trunk(x, states_a, states_b, mem_ks, mem_vs):
        new_a, new_b = [], []
        for l in range(DEPTH):
            j = l // N_MIXERS
            if l % N_MIXERS == 0:
                h, s = hgrn2_mixer(x, states_a[j], hgrn_w_in[j], lb_all[l], hgrn_norm_g[j], hgrn_w_out[j])
                new_a.append(s)
            else:
                h, s = gla_mixer(x, states_b[j], gla_w_in[j], gla_w_gk2[j], gla_b_gk2[j],
                                 gla_norm_g[j], gla_w_out[j])
                new_b.append(s)
            x = post_norm(x, h, ln_g[l, 0], ln_b[l, 0])
            c = memory_cross_attention(x, mem_ks[l], mem_vs[l], xattn_w_q[l], xattn_w_o[l])
            x = post_norm(x, c, ln_g[l, 1], ln_b[l, 1])
            m = grouped_moe(x, router_w, router_bias, moe_w_gate[l], moe_w_up[l], moe_w_down[l])
            x = post_norm(x, m, ln_g[l, 2], ln_b[l, 2])
        return x, jnp.stack(new_a), jnp.stack(new_b)

    pkv = [memory_kv(mem_prompt, xattn_w_kv[l]) for l in range(DEPTH)]
    mem_k_prompt = jnp.stack([kv[0] for kv in pkv])
    mem_v_prompt = jnp.stack([kv[1] for kv in pkv])
    za = jnp.zeros((N_LAYERS_A, BATCH, A_HEADS, A_KEY, A_VAL), x_prompt.dtype)
    zb = jnp.zeros((N_LAYERS_B, BATCH, B_HEADS, B_KEY, B_VAL), x_prompt.dtype)
    y_prompt, state_hgrn_prompt, state_gla_prompt = run_trunk(x_prompt, za, zb, mem_k_prompt, mem_v_prompt)

    y_sample, state_hgrn_sample, state_gla_sample = run_trunk(x_sample, state_hgrn, state_gla,
                                                              cache_mem_k, cache_mem_v)

    return (y_prompt, y_sample, state_hgrn_prompt, state_gla_prompt, mem_k_prompt, mem_v_prompt,
            state_hgrn_sample, state_gla_sample)
```

```python
import functools

import jax
import jax.numpy as jnp
from jax import lax
from jax.experimental import pallas as pl
from jax.experimental.pallas import tpu as pltpu

F32 = jnp.float32
BF16 = jnp.bfloat16
I32 = jnp.int32

N_GROUPS = 4
GROUP_SIZE = 4
GLA_GATE_NORMALIZER = 16.0
LN_EPS = 1e-5
RMS_EPS = 1e-6
GLA_RANK_PAD = 128

VMEM_LIMIT_BYTES = 56 * 1024 * 1024
REC_CHUNK = 64
MOE_TILE = 256

_HI = lax.Precision.HIGHEST


def _params(sem):
    return pltpu.CompilerParams(dimension_semantics=sem, vmem_limit_bytes=VMEM_LIMIT_BYTES)


def _dot(a, b):
    return jnp.dot(a, b, preferred_element_type=F32)


def _dot_nt(a, b):
    return lax.dot_general(a, b, (((1,), (1,)), ((), ())), preferred_element_type=F32)


def _dot_tn(a, b):
    return lax.dot_general(a, b, (((0,), (0,)), ((), ())), preferred_element_type=F32)


def _layer_norm(z, g, b):
    mu = jnp.mean(z, axis=-1, keepdims=True)
    zc = z - mu
    var = jnp.mean(zc * zc, axis=-1, keepdims=True)
    return zc * lax.rsqrt(var + LN_EPS) * g + b


def _linear_kernel(x_ref, w_ref, o_ref):
    o_ref[...] = _dot(x_ref[...].astype(BF16), w_ref[...]).astype(o_ref.dtype)


def _linear(x, w, out_dtype, tm):
    m, k = x.shape
    n = w.shape[1]
    return pl.pallas_call(
        _linear_kernel,
        out_shape=jax.ShapeDtypeStruct((m, n), out_dtype),
        grid=(m // tm,),
        in_specs=[pl.BlockSpec((tm, k), lambda i: (i, 0)),
                  pl.BlockSpec((k, n), lambda i: (0, 0))],
        out_specs=pl.BlockSpec((tm, n), lambda i: (i, 0)),
        compiler_params=_params(("arbitrary",)),
        name="linear",
    )(x, w)


def _kv_proj_kernel(x_ref, wk_ref, wv_ref, k_ref, v_ref):
    x = x_ref[...].astype(BF16)
    k_ref[...] = _dot(x, wk_ref[...])
    v_ref[...] = _dot(x, wv_ref[...])


def _kv_proj(mem, wk, wv, tm):
    r, d = mem.shape
    nl = wk.shape[0]
    out = jax.ShapeDtypeStruct((nl, r, d), F32)
    wspec = pl.BlockSpec((None, d, d), lambda l, i: (l, 0, 0))
    ospec = pl.BlockSpec((None, tm, d), lambda l, i: (l, i, 0))
    return pl.pallas_call(
        _kv_proj_kernel,
        out_shape=(out, out),
        grid=(nl, r // tm),
        in_specs=[pl.BlockSpec((tm, d), lambda l, i: (i, 0)), wspec, wspec],
        out_specs=(ospec, ospec),
        compiler_params=_params(("arbitrary", "arbitrary")),
        name="kv_proj",
    )(mem, wk, wv)


def _linear_res_ln_kernel(h_ref, w_ref, x_ref, g_ref, b_ref, o_ref, *, alpha):
    c = _dot(h_ref[...].astype(BF16), w_ref[...])
    o_ref[...] = _layer_norm(alpha * x_ref[...] + c, g_ref[...], b_ref[...])


def _linear_res_ln(h, w, x, g, b, alpha, tm):
    m, k = h.shape
    d = w.shape[1]
    row = pl.BlockSpec((1, d), lambda i: (0, 0))
    return pl.pallas_call(
        functools.partial(_linear_res_ln_kernel, alpha=alpha),
        out_shape=jax.ShapeDtypeStruct((m, d), F32),
        grid=(m // tm,),
        in_specs=[pl.BlockSpec((tm, k), lambda i: (i, 0)),
                  pl.BlockSpec((k, d), lambda i: (0, 0)),
                  pl.BlockSpec((tm, d), lambda i: (i, 0)), row, row],
        out_specs=pl.BlockSpec((tm, d), lambda i: (i, 0)),
        compiler_params=_params(("arbitrary",)),
        name="linear_res_ln",
    )(h, w, x, g, b)


def _cumsum_rows(x):
    n = x.shape[0]
    row = lax.broadcasted_iota(I32, x.shape, 0)
    s = 1
    while s < n:
        x = x + jnp.where(row >= s, pltpu.roll(x, s, 0), 0.0)
        s *= 2
    return x


def _chunk_step(q, k, v, g, state):
    c, kd = q.shape
    vd = v.shape[1]
    b = _cumsum_rows(g)
    b_last = b[c - 1:c, :]
    b_mid = b[c // 2:c // 2 + 1, :]
    vb = v.astype(BF16)
    o = _dot((q * jnp.exp(b)).astype(BF16), state.astype(BF16))
    qa = (q * jnp.exp(b - b_mid)).astype(BF16)
    ka = (k * jnp.exp(b_mid - b)).astype(BF16)
    scores = _dot_nt(qa, ka)
    ri = lax.broadcasted_iota(I32, (c, c), 0)
    ci = lax.broadcasted_iota(I32, (c, c), 1)
    scores = jnp.where(ri >= ci, scores, 0.0)
    o = o + _dot(scores.astype(BF16), vb)
    ks = (k * jnp.exp(b_last - b)).astype(BF16)
    decay = jnp.broadcast_to(jnp.exp(b_last), (kd, kd)).T
    if vd != kd:
        decay = jnp.concatenate([decay] * (vd // kd), axis=1)
    return o, state * decay + _dot_tn(ks, vb)


def _gated_rmsnorm(o, gate, gain):
    o = o * lax.rsqrt(jnp.mean(o * o, axis=-1, keepdims=True) + RMS_EPS)
    return o * gain * (gate * jax.nn.sigmoid(gate))


def _log_sigmoid(x):
    return jnp.minimum(x, 0.0) - jnp.log(1.0 + jnp.exp(-jnp.abs(x)))


def _rec_kernel(*refs, mode, heads, kd, vd, chunk, tb, nbatch, lb_row, zero_init):
    if zero_init:
        proj_ref, aux0_ref, aux1_ref, gain_ref, o_ref, s_ref = refs
        s0_ref = None
    else:
        proj_ref, s0_ref, aux0_ref, aux1_ref, gain_ref, o_ref, s_ref = refs

    @pl.when(pl.program_id(1) == 0)
    def _():
        if zero_init:
            s_ref[...] = jnp.zeros(s_ref.shape, F32)
        else:
            s_ref[...] = s0_ref[...]

    hk = heads * kd
    hv = heads * vd
    gain = gain_ref[...]
    if mode == "hgrn":
        logits = aux0_ref[...]
        e = jnp.exp(logits - jnp.max(logits, axis=0, keepdims=True))
        lb = jnp.sum(e[:lb_row + 1], axis=0, keepdims=True) / jnp.sum(e, axis=0, keepdims=True)

    def one_batch(nb):
        for sc in range(tb // chunk):
            r0 = nb * tb + sc * chunk
            rows = pl.ds(r0, chunk) if isinstance(r0, int) else pl.ds(pl.multiple_of(r0, 8), chunk)
            if mode == "gla":
                lr = proj_ref[rows, 2 * hk + 2 * hv:2 * hk + 2 * hv + GLA_RANK_PAD]
                gk_all = jnp.dot(lr, aux0_ref[...], precision=_HI, preferred_element_type=F32)
                gk_all = _log_sigmoid(gk_all + aux1_ref[...]) * (1.0 / GLA_GATE_NORMALIZER)
            for h in range(heads):
                if mode == "hgrn":
                    q = proj_ref[rows, h * kd:(h + 1) * kd]
                    f = proj_ref[rows, hk + h * kd:hk + (h + 1) * kd]
                    v = proj_ref[rows, 2 * hk + h * vd:2 * hk + (h + 1) * vd]
                    gate = proj_ref[rows, 2 * hk + hv + h * vd:2 * hk + hv + (h + 1) * vd]
                    lbh = lb[:, h * kd:(h + 1) * kd]
                    q = q * jax.nn.sigmoid(q) * (kd ** -0.5)
                    fg = lbh + (1.0 - lbh) * jax.nn.sigmoid(f)
                    k = 1.0 - fg
                    g = jnp.log(fg)
                else:
                    q = proj_ref[rows, h * kd:(h + 1) * kd] * (kd ** -0.5)
                    k = proj_ref[rows, hk + h * kd:hk + (h + 1) * kd]
                    v = proj_ref[rows, 2 * hk + h * vd:2 * hk + (h + 1) * vd]
                    gate = proj_ref[rows, 2 * hk + hv + h * vd:2 * hk + hv + (h + 1) * vd]
                    g = gk_all[:, h * kd:(h + 1) * kd]
                o, s_new = _chunk_step(q, k, v, g, s_ref[nb, h])
                s_ref[nb, h] = s_new
                o_ref[rows, h * vd:(h + 1) * vd] = _gated_rmsnorm(o, gate, gain)

    if nbatch == 1:
        one_batch(0)
    else:
        def body(nb, carry):
            one_batch(nb)
            return carry
        lax.fori_loop(0, nbatch, body, 0)


def _recurrence(proj, s0, aux0, aux1, gain, *, mode, batch, seq, heads, kd, vd, lb_row=0):
    n, width = proj.shape
    chunk = min(REC_CHUNK, seq)
    tb = min(2 * chunk, seq)
    nbatch = 1 if seq > tb else min(8, batch)
    nblk = seq // tb
    grid = (batch // nbatch, nblk)
    state_spec = pl.BlockSpec((nbatch, heads, kd, vd), lambda b, c: (b, 0, 0, 0))
    full2 = lambda a: pl.BlockSpec(a.shape, lambda b, c: (0, 0))
    in_specs = [pl.BlockSpec((nbatch * tb, width), lambda b, c: (b * nblk + c, 0))]
    args = [proj]
    if s0 is not None:
        in_specs.append(state_spec)
        args.append(s0)
    in_specs += [full2(aux0), full2(aux1), full2(gain)]
    args += [aux0, aux1, gain]
    kern = functools.partial(_rec_kernel, mode=mode, heads=heads, kd=kd, vd=vd, chunk=chunk,
                             tb=tb, nbatch=nbatch, lb_row=lb_row, zero_init=s0 is None)
    return pl.pallas_call(
        kern,
        out_shape=(jax.ShapeDtypeStruct((n, heads * vd), F32),
                   jax.ShapeDtypeStruct((batch, heads, kd, vd), F32)),
        grid=grid,
        in_specs=in_specs,
        out_specs=(pl.BlockSpec((nbatch * tb, heads * vd), lambda b, c: (b * nblk + c, 0)),
                   state_spec),
        compiler_params=_params(("arbitrary", "arbitrary")),
        name="recurrence_" + mode,
    )(*args)


def _attn_kernel(q_ref, k_ref, v_ref, o_ref, *, heads, nbatch, tq):
    hd = q_ref.shape[1] // heads
    scale = hd ** -0.5
    for nb in range(nbatch):
        kb = k_ref[nb].astype(BF16)
        vb = v_ref[nb].astype(BF16)
        q = q_ref[nb * tq:(nb + 1) * tq, :]
        for h in range(heads):
            sl = slice(h * hd, (h + 1) * hd)
            s = _dot_nt(q[:, sl].astype(BF16), kb[:, sl]) * scale
            s = s - jnp.max(s, axis=-1, keepdims=True)
            p = jnp.exp(s)
            p = p / jnp.sum(p, axis=-1, keepdims=True)
            o_ref[nb * tq:(nb + 1) * tq, sl] = _dot(p.astype(BF16), vb[:, sl]).astype(o_ref.dtype)


def _attention(q, mem_k, mem_v, *, batch, seq, heads):
    n, d = q.shape
    mlen = mem_k.shape[1]
    if seq >= 256:
        tq, nbatch = 512, 1
    else:
        tq, nbatch = seq, 2
    nblk = seq // tq
    mem_spec = pl.BlockSpec((nbatch, mlen, d), lambda b, i: (b, 0, 0))
    qspec = pl.BlockSpec((nbatch * tq, d), lambda b, i: (b * nblk + i, 0))
    return pl.pallas_call(
        functools.partial(_attn_kernel, heads=heads, nbatch=nbatch, tq=tq),
        out_shape=jax.ShapeDtypeStruct((n, d), q.dtype),
        grid=(batch // nbatch, nblk),
        in_specs=[qspec, mem_spec, mem_spec],
        out_specs=qspec,
        compiler_params=_params(("arbitrary", "arbitrary")),
        name="mem_attention",
    )(q, mem_k, mem_v)


def _router_kernel(x_ref, wt_ref, bias_ref, cls_ref):
    logits = lax.dot_general(wt_ref[...], x_ref[...], (((1,), (1,)), ((), ())),
                             precision=_HI, preferred_element_type=F32)
    sel = jax.nn.sigmoid(logits) + bias_ref[...]
    rows = [sel[e:e + 1, :] for e in range(N_GROUPS * GROUP_SIZE)]

    def first_argmax(vals):
        best_v, best_i = vals[0], jnp.zeros(vals[0].shape, I32)
        for i in range(1, len(vals)):
            better = vals[i] > best_v
            best_i = jnp.where(better, i, best_i)
            best_v = jnp.where(better, vals[i], best_v)
        return best_i

    group_scores = []
    for gi in range(N_GROUPS):
        a = rows[gi * GROUP_SIZE:(gi + 1) * GROUP_SIZE]
        top2 = None
        for i in range(GROUP_SIZE):
            for j in range(i + 1, GROUP_SIZE):
                s = a[i] + a[j]
                top2 = s if top2 is None else jnp.maximum(top2, s)
        group_scores.append(top2)
    best = first_argmax(group_scores)
    cand = []
    for j in range(GROUP_SIZE):
        cj = rows[j]
        for gi in range(1, N_GROUPS):
            cj = jnp.where(best == gi, rows[gi * GROUP_SIZE + j], cj)
        cand.append(cj)
    i1 = first_argmax(cand)
    i2 = first_argmax([jnp.where(i1 == j, -jnp.inf, cand[j]) for j in range(GROUP_SIZE)])
    lo = jnp.minimum(i1, i2)
    hi = jnp.maximum(i1, i2)
    pair = jnp.where(lo == 0, hi - 1, jnp.where(lo == 1, hi + 1, 5))
    cls_ref[...] = best * 6 + pair


def _router(x, router_wt, bias_col, tm):
    n, d = x.shape
    ne = router_wt.shape[0]
    return pl.pallas_call(
        _router_kernel,
        out_shape=jax.ShapeDtypeStruct((1, n), I32),
        grid=(n // tm,),
        in_specs=[pl.BlockSpec((tm, d), lambda i: (i, 0)),
                  pl.BlockSpec((ne, d), lambda i: (0, 0)),
                  pl.BlockSpec((ne, 1), lambda i: (0, 0))],
        out_specs=pl.BlockSpec((1, tm), lambda i: (0, i)),
        compiler_params=_params(("arbitrary",)),
        name="router",
    )(x, router_wt, bias_col)


_PAIR_LO = (0, 0, 0, 1, 1, 2)
_PAIR_HI = (1, 2, 3, 2, 3, 3)


def _moe_plan(cls, tile):
    n = cls.shape[0]
    ncls = N_GROUPS * 6
    max_tiles = n // tile + ncls
    shift = max(n - 1, 1).bit_length()
    keys = jnp.sort(cls * (1 << shift) + jnp.arange(n, dtype=I32))
    src = keys & ((1 << shift) - 1)
    cid = jnp.arange(ncls, dtype=I32)
    count = jnp.sum((cls[None, :] == cid[:, None]).astype(I32), axis=1)
    cstart = jnp.cumsum(count) - count
    ntile = (count + tile - 1) // tile
    tend = jnp.cumsum(ntile)
    tid = jnp.arange(max_tiles, dtype=I32)
    n_used = tend[-1]
    tcls = jnp.sum((tid[:, None] >= tend[None, :]).astype(I32), axis=1)
    last_cls = jnp.sum((n_used - 1 >= tend).astype(I32))
    tcls = jnp.where(tid < n_used, tcls, last_cls)
    within = tid - (tend - ntile)[tcls]
    tstart = cstart[tcls] + within * tile
    tcnt = jnp.clip(count[tcls] - within * tile, 0, tile)
    tcnt = jnp.where(tid < n_used, tcnt, 0)
    grp = tcls // 6
    e_lo = grp * GROUP_SIZE + jnp.asarray(_PAIR_LO, I32)[tcls % 6]
    e_hi = grp * GROUP_SIZE + jnp.asarray(_PAIR_HI, I32)[tcls % 6]
    return src, tstart, tcnt, e_lo, e_hi, n_used.reshape(1)


def _moe_kernel(src_ref, tstart_ref, tcnt_ref, elo_ref, ehi_ref, nused_ref,
                x_hbm, rw_lo_ref, rw_hi_ref, wg_lo_ref, wu_lo_ref, wd_lo_ref,
                wg_hi_ref, wu_hi_ref, wd_hi_ref, g_ref, b_ref,
                out_hbm, xbuf, ybuf, gsem, ssem, *, alpha):
    j = pl.program_id(0)

    @pl.when(j == 0)
    def _():
        xbuf[...] = jnp.zeros(xbuf.shape, F32)

    @pl.when(j < nused_ref[0])
    def _():
        cnt = tcnt_ref[j]
        start = tstart_ref[j]

        def row_in(r):
            tok = src_ref[start + r]
            return pltpu.make_async_copy(x_hbm.at[pl.ds(tok, 1)], xbuf.at[pl.ds(r, 1)], gsem)

        def row_out(r):
            tok = src_ref[start + r]
            return pltpu.make_async_copy(ybuf.at[pl.ds(r, 1)], out_hbm.at[pl.ds(tok, 1)], ssem)

        def start_in(r, c):
            row_in(r).start()
            return c

        def wait_in(r, c):
            row_in(r).wait()
            return c

        lax.fori_loop(0, cnt, start_in, 0)
        lax.fori_loop(0, cnt, wait_in, 0)

        x = xbuf[...]
        xb = x.astype(BF16)

        def expert(rw_ref, wg_ref, wu_ref, wd_ref):
            h = _dot(xb, wg_ref[...])
            h = h * jax.nn.sigmoid(h) * _dot(xb, wu_ref[...])
            y = _dot(h.astype(BF16), wd_ref[...])
            score = jax.nn.sigmoid(jnp.sum(x * rw_ref[...], axis=-1, keepdims=True))
            return y, score

        y_lo, s_lo = expert(rw_lo_ref, wg_lo_ref, wu_lo_ref, wd_lo_ref)
        y_hi, s_hi = expert(rw_hi_ref, wg_hi_ref, wu_hi_ref, wd_hi_ref)
        tot = s_lo + s_hi
        m = (s_lo / tot) * y_lo + (s_hi / tot) * y_hi
        ybuf[...] = _layer_norm(alpha * x + m, g_ref[...], b_ref[...])

        def start_out(r, c):
            row_out(r).start()
            return c

        def wait_out(r, c):
            row_out(r).wait()
            return c

        lax.fori_loop(0, cnt, start_out, 0)
        lax.fori_loop(0, cnt, wait_out, 0)


def _moe(x, cls, router_wt3, wg, wu, wd, g, b, alpha):
    n, d = x.shape
    de = wg.shape[2]
    tile = MOE_TILE
    src, tstart, tcnt, e_lo, e_hi, n_used = _moe_plan(cls, tile)
    max_tiles = tstart.shape[0]

    def by_lo(shape):
        return pl.BlockSpec(shape, lambda j, s, ts, tc, el, eh, nu: (el[j], 0, 0))

    def by_hi(shape):
        return pl.BlockSpec(shape, lambda j, s, ts, tc, el, eh, nu: (eh[j], 0, 0))

    row = pl.BlockSpec((1, d), lambda j, s, ts, tc, el, eh, nu: (0, 0))
    grid_spec = pltpu.PrefetchScalarGridSpec(
        num_scalar_prefetch=6,
        grid=(max_tiles,),
        in_specs=[pl.BlockSpec(memory_space=pl.ANY),
                  by_lo((None, 1, d)), by_hi((None, 1, d)),
                  by_lo((None, d, de)), by_lo((None, d, de)), by_lo((None, de, d)),
                  by_hi((None, d, de)), by_hi((None, d, de)), by_hi((None, de, d)),
                  row, row],
        out_specs=pl.BlockSpec(memory_space=pl.ANY),
        scratch_shapes=[pltpu.VMEM((tile, d), F32), pltpu.VMEM((tile, d), F32),
                        pltpu.SemaphoreType.DMA(()), pltpu.SemaphoreType.DMA(())],
    )
    return pl.pallas_call(
        functools.partial(_moe_kernel, alpha=alpha),
        out_shape=jax.ShapeDtypeStruct((n, d), F32),
        grid_spec=grid_spec,
        compiler_params=_params(("arbitrary",)),
        name="grouped_moe",
    )(src, tstart, tcnt, e_lo, e_hi, n_used,
      x, router_wt3, router_wt3, wg, wu, wd, wg, wu, wd, g, b)


def kernel(x_prompt, x_sample, state_hgrn, state_gla, cache_mem_k, cache_mem_v, mem_prompt, hgrn_w_in, hgrn_lb_logits, hgrn_norm_g, hgrn_w_out, gla_w_in, gla_w_gk2, gla_b_gk2, gla_norm_g, gla_w_out, xattn_w_q, xattn_w_kv, xattn_w_o, router_w, router_bias, moe_w_gate, moe_w_up, moe_w_down, ln_g, ln_b):
    batch, seq, d = x_prompt.shape
    dec_batch, dec_seq, _ = x_sample.shape
    depth = ln_g.shape[0]
    alpha = (2 * depth) ** 0.25
    a_heads, a_key, a_val = state_hgrn.shape[2:]
    b_heads, b_key, b_val = state_gla.shape[2:]
    mem_len, x_heads = cache_mem_k.shape[2], cache_mem_k.shape[3]
    rank = gla_w_gk2.shape[1]
    b_main = 2 * b_heads * b_key + b_heads * b_val + d

    hgrn_w_in_b = hgrn_w_in.astype(BF16)
    hgrn_w_out_b = hgrn_w_out.astype(BF16)
    gla_w_in_b = jnp.concatenate(
        [gla_w_in, jnp.zeros(gla_w_in.shape[:2] + (GLA_RANK_PAD - rank,), F32)], axis=-1).astype(BF16)
    gla_w_gk2_p = jnp.concatenate(
        [gla_w_gk2, jnp.zeros((gla_w_gk2.shape[0], GLA_RANK_PAD - rank, gla_w_gk2.shape[2]), F32)], axis=1)
    gla_w_out_b = gla_w_out.astype(BF16)
    w_q_b = xattn_w_q.astype(BF16)
    w_k_b = xattn_w_kv[:, :, :d].astype(BF16)
    w_v_b = xattn_w_kv[:, :, d:].astype(BF16)
    w_o_b = xattn_w_o.astype(BF16)
    wg_b = moe_w_gate.astype(BF16)
    wu_b = moe_w_up.astype(BF16)
    wd_b = moe_w_down.astype(BF16)
    router_wt = router_w.T
    router_wt3 = router_wt[:, None, :]
    bias_col = router_bias[:, None]
    assert b_main + rank == gla_w_in.shape[2]

    mem_k_p, mem_v_p = _kv_proj(mem_prompt.reshape(batch * mem_len, d), w_k_b, w_v_b, 512)
    mem_k_prompt = mem_k_p.reshape(depth, batch, mem_len, x_heads, d // x_heads)
    mem_v_prompt = mem_v_p.reshape(depth, batch, mem_len, x_heads, d // x_heads)

    def run_trunk(x3, states_a, states_b, mem_ks, mem_vs):
        nb, ns, _ = x3.shape
        x = x3.reshape(nb * ns, d)
        new_a, new_b = [], []
        for l in range(depth):
            j = l // 2
            row = lambda a, i: a[l, i][None, :]
            if l % 2 == 0:
                proj = _linear(x, hgrn_w_in_b[j], F32, 512)
                o, s = _recurrence(proj, None if states_a is None else states_a[j],
                                   hgrn_lb_logits, hgrn_lb_logits[:1], hgrn_norm_g[j][None, :],
                                   mode="hgrn", batch=nb, seq=ns, heads=a_heads, kd=a_key, vd=a_val,
                                   lb_row=l)
                new_a.append(s)
                w_out = hgrn_w_out_b[j]
            else:
                proj = _linear(x, gla_w_in_b[j], F32, 512)
                o, s = _recurrence(proj, None if states_b is None else states_b[j],
                                   gla_w_gk2_p[j], gla_b_gk2[j][None, :], gla_norm_g[j][None, :],
                                   mode="gla", batch=nb, seq=ns, heads=b_heads, kd=b_key, vd=b_val)
                new_b.append(s)
                w_out = gla_w_out_b[j]
            x = _linear_res_ln(o, w_out, x, row(ln_g, 0), row(ln_b, 0), alpha, 512)
            q = _linear(x, w_q_b[l], BF16 if ns >= 256 else F32, 512)
            c = _attention(q, mem_ks[l].reshape(nb, mem_len, d), mem_vs[l].reshape(nb, mem_len, d),
                           batch=nb, seq=ns, heads=x_heads)
            x = _linear_res_ln(c, w_o_b[l], x, row(ln_g, 1), row(ln_b, 1), alpha, 512)
            cls = _router(x, router_wt, bias_col, 512)
            x = _moe(x, cls[0], router_wt3, wg_b[l], wu_b[l], wd_b[l], row(ln_g, 2), row(ln_b, 2), alpha)
        return x.reshape(nb, ns, d), jnp.stack(new_a), jnp.stack(new_b)

    y_prompt, state_hgrn_prompt, state_gla_prompt = run_trunk(x_prompt, None, None, mem_k_prompt, mem_v_prompt)
    y_sample, state_hgrn_sample, state_gla_sample = run_trunk(x_sample, state_hgrn, state_gla,
                                                              cache_mem_k, cache_mem_v)
    return (y_prompt, y_sample, state_hgrn_prompt, state_gla_prompt, mem_k_prompt, mem_v_prompt,
            state_hgrn_sample, state_gla_sample)
```

```python
import functools

import jax
import jax.numpy as jnp
from jax import lax
from jax.experimental import pallas as pl
from jax.experimental.pallas import tpu as pltpu

F32 = jnp.float32
BF16 = jnp.bfloat16
I32 = jnp.int32

N_GROUPS = 4
GROUP_SIZE = 4
GLA_GATE_NORMALIZER = 16.0
LN_EPS = 1e-5
RMS_EPS = 1e-6
GLA_RANK_PAD = 128

VMEM_LIMIT_BYTES = 56 * 1024 * 1024
REC_CHUNK = 64
MOE_TILE = 256

_HI = lax.Precision.HIGHEST


def _params(sem):
    return pltpu.CompilerParams(dimension_semantics=sem, vmem_limit_bytes=VMEM_LIMIT_BYTES)


def _dot(a, b):
    return jnp.dot(a, b, preferred_element_type=F32)


def _dot_nt(a, b):
    return lax.dot_general(a, b, (((1,), (1,)), ((), ())), preferred_element_type=F32)


def _dot_tn(a, b):
    return lax.dot_general(a, b, (((0,), (0,)), ((), ())), preferred_element_type=F32)


def _layer_norm(z, g, b):
    mu = jnp.mean(z, axis=-1, keepdims=True)
    zc = z - mu
    var = jnp.mean(zc * zc, axis=-1, keepdims=True)
    return zc * lax.rsqrt(var + LN_EPS) * g + b


def _linear_kernel(x_ref, w_ref, o_ref):
    o_ref[...] = _dot(x_ref[...].astype(BF16), w_ref[...]).astype(o_ref.dtype)


def _linear(x, w, out_dtype, tm):
    m, k = x.shape
    n = w.shape[1]
    return pl.pallas_call(
        _linear_kernel,
        out_shape=jax.ShapeDtypeStruct((m, n), out_dtype),
        grid=(m // tm,),
        in_specs=[pl.BlockSpec((tm, k), lambda i: (i, 0)),
                  pl.BlockSpec((k, n), lambda i: (0, 0))],
        out_specs=pl.BlockSpec((tm, n), lambda i: (i, 0)),
        compiler_params=_params(("arbitrary",)),
        name="linear",
    )(x, w)


def _kv_proj_kernel(x_ref, wk_ref, wv_ref, k_ref, v_ref):
    x = x_ref[...].astype(BF16)
    k_ref[...] = _dot(x, wk_ref[...])
    v_ref[...] = _dot(x, wv_ref[...])


def _kv_proj(mem, wk, wv, tm):
    r, d = mem.shape
    nl = wk.shape[0]
    out = jax.ShapeDtypeStruct((nl, r, d), F32)
    wspec = pl.BlockSpec((None, d, d), lambda l, i: (l, 0, 0))
    ospec = pl.BlockSpec((None, tm, d), lambda l, i: (l, i, 0))
    return pl.pallas_call(
        _kv_proj_kernel,
        out_shape=(out, out),
        grid=(nl, r // tm),
        in_specs=[pl.BlockSpec((tm, d), lambda l, i: (i, 0)), wspec, wspec],
        out_specs=(ospec, ospec),
        compiler_params=_params(("arbitrary", "arbitrary")),
        name="kv_proj",
    )(mem, wk, wv)


def _linear_res_ln_kernel(h_ref, w_ref, x_ref, g_ref, b_ref, o_ref, *, alpha):
    c = _dot(h_ref[...].astype(BF16), w_ref[...])
    o_ref[...] = _layer_norm(alpha * x_ref[...] + c, g_ref[...], b_ref[...])


def _linear_res_ln(h, w, x, g, b, alpha, tm):
    m, k = h.shape
    d = w.shape[1]
    row = pl.BlockSpec((1, d), lambda i: (0, 0))
    return pl.pallas_call(
        functools.partial(_linear_res_ln_kernel, alpha=alpha),
        out_shape=jax.ShapeDtypeStruct((m, d), F32),
        grid=(m // tm,),
        in_specs=[pl.BlockSpec((tm, k), lambda i: (i, 0)),
                  pl.BlockSpec((k, d), lambda i: (0, 0)),
                  pl.BlockSpec((tm, d), lambda i: (i, 0)), row, row],
        out_specs=pl.BlockSpec((tm, d), lambda i: (i, 0)),
        compiler_params=_params(("arbitrary",)),
        name="linear_res_ln",
    )(h, w, x, g, b)


def _cumsum_rows(x):
    n = x.shape[0]
    row = lax.broadcasted_iota(I32, x.shape, 0)
    s = 1
    while s < n:
        x = x + jnp.where(row >= s, pltpu.roll(x, s, 0), 0.0)
        s *= 2
    return x


def _chunk_step(q, k, v, g, state):
    c, kd = q.shape
    vd = v.shape[1]
    b = _cumsum_rows(g)
    b_last = b[c - 1:c, :]
    b_mid = b[c // 2:c // 2 + 1, :]
    vb = v.astype(BF16)
    o = _dot((q * jnp.exp(b)).astype(BF16), state.astype(BF16))
    qa = (q * jnp.exp(b - b_mid)).astype(BF16)
    ka = (k * jnp.exp(b_mid - b)).astype(BF16)
    scores = _dot_nt(qa, ka)
    ri = lax.broadcasted_iota(I32, (c, c), 0)
    ci = lax.broadcasted_iota(I32, (c, c), 1)
    scores = jnp.where(ri >= ci, scores, 0.0)
    o = o + _dot(scores.astype(BF16), vb)
    ks = (k * jnp.exp(b_last - b)).astype(BF16)
    decay = jnp.broadcast_to(jnp.exp(b_last), (kd, kd)).T
    if vd != kd:
        decay = jnp.concatenate([decay] * (vd // kd), axis=1)
    return o, state * decay + _dot_tn(ks, vb)


def _gated_rmsnorm(o, gate, gain):
    o = o * lax.rsqrt(jnp.mean(o * o, axis=-1, keepdims=True) + RMS_EPS)
    return o * gain * (gate * jax.nn.sigmoid(gate))


def _log_sigmoid(x):
    return jnp.minimum(x, 0.0) - jnp.log(1.0 + jnp.exp(-jnp.abs(x)))


def _rec_kernel(*refs, mode, heads, kd, vd, chunk, tb, nbatch, lb_row, zero_init):
    if zero_init:
        proj_ref, aux0_ref, aux1_ref, gain_ref, o_ref, s_ref = refs
        s0_ref = None
    else:
        proj_ref, s0_ref, aux0_ref, aux1_ref, gain_ref, o_ref, s_ref = refs

    @pl.when(pl.program_id(1) == 0)
    def _():
        if zero_init:
            s_ref[...] = jnp.zeros(s_ref.shape, F32)
        else:
            s_ref[...] = s0_ref[...]

    hk = heads * kd
    hv = heads * vd
    gain = gain_ref[...]
    if mode == "hgrn":
        logits = aux0_ref[...]
        e = jnp.exp(logits - jnp.max(logits, axis=0, keepdims=True))
        lb = jnp.sum(e[:lb_row + 1], axis=0, keepdims=True) / jnp.sum(e, axis=0, keepdims=True)

    def one_batch(nb):
        for sc in range(tb // chunk):
            r0 = nb * tb + sc * chunk
            rows = pl.ds(r0, chunk) if isinstance(r0, int) else pl.ds(pl.multiple_of(r0, 8), chunk)
            if mode == "gla":
                lr = proj_ref[rows, 2 * hk + 2 * hv:2 * hk + 2 * hv + GLA_RANK_PAD]
                gk_all = jnp.dot(lr, aux0_ref[...], precision=_HI, preferred_element_type=F32)
                gk_all = _log_sigmoid(gk_all + aux1_ref[...]) * (1.0 / GLA_GATE_NORMALIZER)
            for h in range(heads):
                if mode == "hgrn":
                    q = proj_ref[rows, h * kd:(h + 1) * kd]
                    f = proj_ref[rows, hk + h * kd:hk + (h + 1) * kd]
                    v = proj_ref[rows, 2 * hk + h * vd:2 * hk + (h + 1) * vd]
                    gate = proj_ref[rows, 2 * hk + hv + h * vd:2 * hk + hv + (h + 1) * vd]
                    lbh = lb[:, h * kd:(h + 1) * kd]
                    q = q * jax.nn.sigmoid(q) * (kd ** -0.5)
                    fg = lbh + (1.0 - lbh) * jax.nn.sigmoid(f)
                    k = 1.0 - fg
                    g = jnp.log(fg)
                else:
                    q = proj_ref[rows, h * kd:(h + 1) * kd] * (kd ** -0.5)
                    k = proj_ref[rows, hk + h * kd:hk + (h + 1) * kd]
                    v = proj_ref[rows, 2 * hk + h * vd:2 * hk + (h + 1) * vd]
                    gate = proj_ref[rows, 2 * hk + hv + h * vd:2 * hk + hv + (h + 1) * vd]
                    g = gk_all[:, h * kd:(h + 1) * kd]
                o, s_new = _chunk_step(q, k, v, g, s_ref[nb, h])
                s_ref[nb, h] = s_new
                o_ref[rows, h * vd:(h + 1) * vd] = _gated_rmsnorm(o, gate, gain)

    if nbatch == 1:
        one_batch(0)
    else:
        def body(nb, carry):
            one_batch(nb)
            return carry
        lax.fori_loop(0, nbatch, body, 0)


def _recurrence(proj, s0, aux0, aux1, gain, *, mode, batch, seq, heads, kd, vd, lb_row=0):
    n, width = proj.shape
    chunk = min(REC_CHUNK, seq)
    tb = min(2 * chunk, seq)
    nbatch = 1 if seq > tb else min(8, batch)
    nblk = seq // tb
    grid = (batch // nbatch, nblk)
    state_spec = pl.BlockSpec((nbatch, heads, kd, vd), lambda b, c: (b, 0, 0, 0))
    full2 = lambda a: pl.BlockSpec(a.shape, lambda b, c: (0, 0))
    in_specs = [pl.BlockSpec((nbatch * tb, width), lambda b, c: (b * nblk + c, 0))]
    args = [proj]
    if s0 is not None:
        in_specs.append(state_spec)
        args.append(s0)
    in_specs += [full2(aux0), full2(aux1), full2(gain)]
    args += [aux0, aux1, gain]
    kern = functools.partial(_rec_kernel, mode=mode, heads=heads, kd=kd, vd=vd, chunk=chunk,
                             tb=tb, nbatch=nbatch, lb_row=lb_row, zero_init=s0 is None)
    return pl.pallas_call(
        kern,
        out_shape=(jax.ShapeDtypeStruct((n, heads * vd), F32),
                   jax.ShapeDtypeStruct((batch, heads, kd, vd), F32)),
        grid=grid,
        in_specs=in_specs,
        out_specs=(pl.BlockSpec((nbatch * tb, heads * vd), lambda b, c: (b * nblk + c, 0)),
                   state_spec),
        compiler_params=_params(("arbitrary", "arbitrary")),
        name="recurrence_" + mode,
    )(*args)


def _attn_kernel(q_ref, k_ref, v_ref, o_ref, *, heads, nbatch, tq, per_head):
    hd = q_ref.shape[1] // heads
    scale = hd ** -0.5
    units = [(nb, h) for nb in range(nbatch) for h in range(heads)]

    def head_rows(ref, nb, h):
        if not per_head:
            return ref[nb, :, h * hd:(h + 1) * hd].astype(BF16)
        nt = hd // 128
        period = heads * nt
        mlen = ref.shape[1] // period
        parts = [ref[nb, pl.ds(t * heads + h, mlen, stride=period), :] for t in range(nt)]
        return jnp.concatenate(parts, axis=1).astype(BF16)

    scores = [_dot_nt(q_ref[nb * tq:(nb + 1) * tq, h * hd:(h + 1) * hd].astype(BF16),
                      head_rows(k_ref, nb, h)) * scale for nb, h in units]
    probs = []
    for s in scores:
        p = jnp.exp(s - jnp.max(s, axis=-1, keepdims=True))
        probs.append((p / jnp.sum(p, axis=-1, keepdims=True)).astype(BF16))
    for (nb, h), p in zip(units, probs):
        o_ref[nb * tq:(nb + 1) * tq, h * hd:(h + 1) * hd] = _dot(p, head_rows(v_ref, nb, h)).astype(o_ref.dtype)


def _interleaved_rows(mem):
    nl, b, m, heads, hd = mem.shape
    nt = hd // 128
    return mem.reshape(nl, b, m, heads, nt, 128).transpose(0, 1, 2, 4, 3, 5).reshape(nl, b, m * nt * heads, 128)


def _attention(q, mem_k, mem_v, layer, *, batch, seq, heads):
    n, d = q.shape
    per_head = mem_k.shape[3] != d
    if seq >= 256:
        tq, nbatch = 512, 1
    else:
        tq, nbatch = seq, 2
    nblk = seq // tq
    mem_spec = pl.BlockSpec((None, nbatch) + mem_k.shape[2:], lambda b, i: (layer, b, 0, 0))
    qspec = pl.BlockSpec((nbatch * tq, d), lambda b, i: (b * nblk + i, 0))
    return pl.pallas_call(
        functools.partial(_attn_kernel, heads=heads, nbatch=nbatch, tq=tq, per_head=per_head),
        out_shape=jax.ShapeDtypeStruct((n, d), q.dtype),
        grid=(batch // nbatch, nblk),
        in_specs=[qspec, mem_spec, mem_spec],
        out_specs=qspec,
        compiler_params=_params(("arbitrary", "arbitrary")),
        name="mem_attention",
    )(q, mem_k, mem_v)


def _router_kernel(x_ref, wt_ref, bias_ref, cls_ref):
    logits = lax.dot_general(wt_ref[...], x_ref[...], (((1,), (1,)), ((), ())),
                             precision=_HI, preferred_element_type=F32)
    sel = jax.nn.sigmoid(logits) + bias_ref[...]
    rows = [sel[e:e + 1, :] for e in range(N_GROUPS * GROUP_SIZE)]

    def first_argmax(vals):
        best_v, best_i = vals[0], jnp.zeros(vals[0].shape, I32)
        for i in range(1, len(vals)):
            better = vals[i] > best_v
            best_i = jnp.where(better, i, best_i)
            best_v = jnp.where(better, vals[i], best_v)
        return best_i

    group_scores = []
    for gi in range(N_GROUPS):
        a = rows[gi * GROUP_SIZE:(gi + 1) * GROUP_SIZE]
        top2 = None
        for i in range(GROUP_SIZE):
            for j in range(i + 1, GROUP_SIZE):
                s = a[i] + a[j]
                top2 = s if top2 is None else jnp.maximum(top2, s)
        group_scores.append(top2)
    best = first_argmax(group_scores)
    cand = []
    for j in range(GROUP_SIZE):
        cj = rows[j]
        for gi in range(1, N_GROUPS):
            cj = jnp.where(best == gi, rows[gi * GROUP_SIZE + j], cj)
        cand.append(cj)
    i1 = first_argmax(cand)
    i2 = first_argmax([jnp.where(i1 == j, -jnp.inf, cand[j]) for j in range(GROUP_SIZE)])
    lo = jnp.minimum(i1, i2)
    hi = jnp.maximum(i1, i2)
    pair = jnp.where(lo == 0, hi - 1, jnp.where(lo == 1, hi + 1, 5))
    cls_ref[...] = best * 6 + pair


def _router(x, router_wt, bias_col, tm):
    n, d = x.shape
    ne = router_wt.shape[0]
    return pl.pallas_call(
        _router_kernel,
        out_shape=jax.ShapeDtypeStruct((1, n), I32),
        grid=(n // tm,),
        in_specs=[pl.BlockSpec((tm, d), lambda i: (i, 0)),
                  pl.BlockSpec((ne, d), lambda i: (0, 0)),
                  pl.BlockSpec((ne, 1), lambda i: (0, 0))],
        out_specs=pl.BlockSpec((1, tm), lambda i: (0, i)),
        compiler_params=_params(("arbitrary",)),
        name="router",
    )(x, router_wt, bias_col)


_PAIR_LO = (0, 0, 0, 1, 1, 2)
_PAIR_HI = (1, 2, 3, 2, 3, 3)


def _moe_plan(cls, tile):
    n = cls.shape[0]
    ncls = N_GROUPS * 6
    max_tiles = n // tile + ncls
    shift = max(n - 1, 1).bit_length()
    keys = jnp.sort(cls * (1 << shift) + jnp.arange(n, dtype=I32))
    src = keys & ((1 << shift) - 1)
    cid = jnp.arange(ncls, dtype=I32)
    count = jnp.sum((cls[None, :] == cid[:, None]).astype(I32), axis=1)
    cstart = jnp.cumsum(count) - count
    ntile = (count + tile - 1) // tile
    tend = jnp.cumsum(ntile)
    tid = jnp.arange(max_tiles, dtype=I32)
    n_used = tend[-1]
    tcls = jnp.sum((tid[:, None] >= tend[None, :]).astype(I32), axis=1)
    last_cls = jnp.sum((n_used - 1 >= tend).astype(I32))
    tcls = jnp.where(tid < n_used, tcls, last_cls)
    within = tid - (tend - ntile)[tcls]
    tstart = cstart[tcls] + within * tile
    tcnt = jnp.clip(count[tcls] - within * tile, 0, tile)
    tcnt = jnp.where(tid < n_used, tcnt, 0)
    grp = tcls // 6
    e_lo = grp * GROUP_SIZE + jnp.asarray(_PAIR_LO, I32)[tcls % 6]
    e_hi = grp * GROUP_SIZE + jnp.asarray(_PAIR_HI, I32)[tcls % 6]
    return src, tstart, tcnt, e_lo, e_hi, n_used.reshape(1)


def _moe_kernel(src_ref, tstart_ref, tcnt_ref, elo_ref, ehi_ref, nused_ref,
                x_hbm, rw_lo_ref, rw_hi_ref, wg_lo_ref, wu_lo_ref, wd_lo_ref,
                wg_hi_ref, wu_hi_ref, wd_hi_ref, g_ref, b_ref,
                out_hbm, xbuf, ybuf, gsem, ssem, *, alpha):
    j = pl.program_id(0)
    n_used = nused_ref[0]
    slot = lax.rem(j, 2)

    def for_rows(cnt, fn):
        ngrp = lax.shift_right_logical(cnt, 3)

        def group(gi, c):
            base = pl.multiple_of(gi * 8, 8)
            for u in range(8):
                fn(base + u, u)
            return c

        def single(r, c):
            fn(r, 0)
            return c

        lax.fori_loop(0, ngrp, group, 0)
        lax.fori_loop(ngrp * 8, cnt, single, 0)

    def start_gather(t, s):
        start = tstart_ref[t]

        def fn(r, u):
            tok = src_ref[start + r]
            pltpu.make_async_copy(x_hbm.at[pl.ds(tok, 1)], xbuf.at[s, pl.ds(r, 1)],
                                  gsem.at[s]).start(priority=u % 2)
        for_rows(tcnt_ref[t], fn)

    def wait_rows(cnt, block_copy):
        n8 = pl.multiple_of(lax.shift_right_logical(cnt, 3) * 8, 8)

        @pl.when(n8 > 0)
        def _():
            block_copy(0, n8).wait()

        def single(r, c):
            block_copy(r, 1).wait()
            return c
        lax.fori_loop(n8, cnt, single, 0)

    def wait_gather(t, s):
        wait_rows(tcnt_ref[t], lambda r, n: pltpu.make_async_copy(
            x_hbm.at[pl.ds(r, n)], xbuf.at[s, pl.ds(r, n)], gsem.at[s]))

    def start_scatter(t, s):
        start = tstart_ref[t]

        def fn(r, u):
            tok = src_ref[start + r]
            pltpu.make_async_copy(ybuf.at[s, pl.ds(r, 1)], out_hbm.at[pl.ds(tok, 1)],
                                  ssem.at[s]).start(priority=u % 2)
        for_rows(tcnt_ref[t], fn)

    def wait_scatter(t, s):
        wait_rows(tcnt_ref[t], lambda r, n: pltpu.make_async_copy(
            ybuf.at[s, pl.ds(r, n)], out_hbm.at[pl.ds(r, n)], ssem.at[s]))

    @pl.when(j == 0)
    def _():
        xbuf[...] = jnp.zeros(xbuf.shape, F32)
        start_gather(0, 0)

    @pl.when(j < n_used)
    def _():
        @pl.when(j + 1 < n_used)
        def _():
            start_gather(j + 1, 1 - slot)

        wait_gather(j, slot)

        @pl.when(j >= 2)
        def _():
            wait_scatter(j - 2, slot)

        x = xbuf[slot]
        xb = x.astype(BF16)

        def expert(rw_ref, wg_ref, wu_ref, wd_ref):
            h = _dot(xb, wg_ref[...])
            h = h * jax.nn.sigmoid(h) * _dot(xb, wu_ref[...])
            y = _dot(h.astype(BF16), wd_ref[...])
            score = jax.nn.sigmoid(jnp.sum(x * rw_ref[...], axis=-1, keepdims=True))
            return y, score

        y_lo, s_lo = expert(rw_lo_ref, wg_lo_ref, wu_lo_ref, wd_lo_ref)
        y_hi, s_hi = expert(rw_hi_ref, wg_hi_ref, wu_hi_ref, wd_hi_ref)
        tot = s_lo + s_hi
        m = (s_lo / tot) * y_lo + (s_hi / tot) * y_hi
        ybuf[slot] = _layer_norm(alpha * x + m, g_ref[...], b_ref[...])
        start_scatter(j, slot)

        @pl.when(j == n_used - 1)
        def _():
            @pl.when(j >= 1)
            def _():
                wait_scatter(j - 1, 1 - slot)
            wait_scatter(j, slot)


def _moe(x, cls, router_wt3, wg, wu, wd, layer, g, b, alpha):
    n, d = x.shape
    de = wg.shape[3]
    tile = MOE_TILE
    src, tstart, tcnt, e_lo, e_hi, n_used = _moe_plan(cls, tile)
    max_tiles = tstart.shape[0]

    def by_lo(shape):
        return pl.BlockSpec(shape, lambda j, s, ts, tc, el, eh, nu: (el[j], 0, 0))

    def by_hi(shape):
        return pl.BlockSpec(shape, lambda j, s, ts, tc, el, eh, nu: (eh[j], 0, 0))

    def w_lo(r, c):
        return pl.BlockSpec((None, None, r, c), lambda j, s, ts, tc, el, eh, nu: (layer, el[j], 0, 0))

    def w_hi(r, c):
        return pl.BlockSpec((None, None, r, c), lambda j, s, ts, tc, el, eh, nu: (layer, eh[j], 0, 0))

    row = pl.BlockSpec((1, d), lambda j, s, ts, tc, el, eh, nu: (0, 0))
    grid_spec = pltpu.PrefetchScalarGridSpec(
        num_scalar_prefetch=6,
        grid=(max_tiles,),
        in_specs=[pl.BlockSpec(memory_space=pl.ANY),
                  by_lo((None, 1, d)), by_hi((None, 1, d)),
                  w_lo(d, de), w_lo(d, de), w_lo(de, d),
                  w_hi(d, de), w_hi(d, de), w_hi(de, d),
                  row, row],
        out_specs=pl.BlockSpec(memory_space=pl.ANY),
        scratch_shapes=[pltpu.VMEM((2, tile, d), F32), pltpu.VMEM((2, tile, d), F32),
                        pltpu.SemaphoreType.DMA((2,)), pltpu.SemaphoreType.DMA((2,))],
    )
    return pl.pallas_call(
        functools.partial(_moe_kernel, alpha=alpha),
        out_shape=jax.ShapeDtypeStruct((n, d), F32),
        grid_spec=grid_spec,
        compiler_params=_params(("arbitrary",)),
        name="grouped_moe",
    )(src, tstart, tcnt, e_lo, e_hi, n_used,
      x, router_wt3, router_wt3, wg, wu, wd, wg, wu, wd, g, b)


def kernel(x_prompt, x_sample, state_hgrn, state_gla, cache_mem_k, cache_mem_v, mem_prompt, hgrn_w_in, hgrn_lb_logits, hgrn_norm_g, hgrn_w_out, gla_w_in, gla_w_gk2, gla_b_gk2, gla_norm_g, gla_w_out, xattn_w_q, xattn_w_kv, xattn_w_o, router_w, router_bias, moe_w_gate, moe_w_up, moe_w_down, ln_g, ln_b):
    batch, seq, d = x_prompt.shape
    dec_batch, dec_seq, _ = x_sample.shape
    depth = ln_g.shape[0]
    alpha = (2 * depth) ** 0.25
    a_heads, a_key, a_val = state_hgrn.shape[2:]
    b_heads, b_key, b_val = state_gla.shape[2:]
    mem_len, x_heads = cache_mem_k.shape[2], cache_mem_k.shape[3]
    rank = gla_w_gk2.shape[1]
    b_main = 2 * b_heads * b_key + b_heads * b_val + d

    hgrn_w_in_b = hgrn_w_in.astype(BF16)
    hgrn_w_out_b = hgrn_w_out.astype(BF16)
    gla_w_in_b = jnp.concatenate(
        [gla_w_in, jnp.zeros(gla_w_in.shape[:2] + (GLA_RANK_PAD - rank,), F32)], axis=-1).astype(BF16)
    gla_w_gk2_p = jnp.concatenate(
        [gla_w_gk2, jnp.zeros((gla_w_gk2.shape[0], GLA_RANK_PAD - rank, gla_w_gk2.shape[2]), F32)], axis=1)
    gla_w_out_b = gla_w_out.astype(BF16)
    w_q_b = xattn_w_q.astype(BF16)
    w_k_b = xattn_w_kv[:, :, :d].astype(BF16)
    w_v_b = xattn_w_kv[:, :, d:].astype(BF16)
    w_o_b = xattn_w_o.astype(BF16)
    wg_b = moe_w_gate.astype(BF16)
    wu_b = moe_w_up.astype(BF16)
    wd_b = moe_w_down.astype(BF16)
    router_wt = router_w.T
    router_wt3 = router_wt[:, None, :]
    bias_col = router_bias[:, None]
    assert b_main + rank == gla_w_in.shape[2]

    mem_k_p, mem_v_p = _kv_proj(mem_prompt.reshape(batch * mem_len, d), w_k_b, w_v_b, 512)
    mem_k_prompt = mem_k_p.reshape(depth, batch, mem_len, x_heads, d // x_heads)
    mem_v_prompt = mem_v_p.reshape(depth, batch, mem_len, x_heads, d // x_heads)

    def run_trunk(x3, states_a, states_b, mem_ks, mem_vs):
        nb, ns, _ = x3.shape
        x = x3.reshape(nb * ns, d)
        new_a, new_b = [], []
        for l in range(depth):
            j = l // 2
            row = lambda a, i: a[l, i][None, :]
            if l % 2 == 0:
                proj = _linear(x, hgrn_w_in_b[j], F32, 512)
                o, s = _recurrence(proj, None if states_a is None else states_a[j],
                                   hgrn_lb_logits, hgrn_lb_logits[:1], hgrn_norm_g[j][None, :],
                                   mode="hgrn", batch=nb, seq=ns, heads=a_heads, kd=a_key, vd=a_val,
                                   lb_row=l)
                new_a.append(s)
                w_out = hgrn_w_out_b[j]
            else:
                proj = _linear(x, gla_w_in_b[j], F32, 512)
                o, s = _recurrence(proj, None if states_b is None else states_b[j],
                                   gla_w_gk2_p[j], gla_b_gk2[j][None, :], gla_norm_g[j][None, :],
                                   mode="gla", batch=nb, seq=ns, heads=b_heads, kd=b_key, vd=b_val)
                new_b.append(s)
                w_out = gla_w_out_b[j]
            x = _linear_res_ln(o, w_out, x, row(ln_g, 0), row(ln_b, 0), alpha, 512)
            q = _linear(x, w_q_b[l], BF16 if ns >= 256 else F32, 512)
            c = _attention(q, mem_ks, mem_vs, l, batch=nb, seq=ns, heads=x_heads)
            x = _linear_res_ln(c, w_o_b[l], x, row(ln_g, 1), row(ln_b, 1), alpha, 512)
            cls = _router(x, router_wt, bias_col, 512)
            x = _moe(x, cls[0], router_wt3, wg_b, wu_b, wd_b, l, row(ln_g, 2), row(ln_b, 2), alpha)
        return x.reshape(nb, ns, d), jnp.stack(new_a), jnp.stack(new_b)

    y_prompt, state_hgrn_prompt, state_gla_prompt = run_trunk(
        x_prompt, None, None, mem_k_p.reshape(depth, batch, mem_len, d), mem_v_p.reshape(depth, batch, mem_len, d))
    y_sample, state_hgrn_sample, state_gla_sample = run_trunk(
        x_sample, state_hgrn, state_gla, _interleaved_rows(cache_mem_k), _interleaved_rows(cache_mem_v))
    return (y_prompt, y_sample, state_hgrn_prompt, state_gla_prompt, mem_k_prompt, mem_v_prompt,
            state_hgrn_sample, state_gla_sample)
```

```python
import functools

import jax
import jax.numpy as jnp
from jax import lax
from jax.experimental import pallas as pl
from jax.experimental.pallas import tpu as pltpu

F32 = jnp.float32
BF16 = jnp.bfloat16
I32 = jnp.int32

N_GROUPS = 4
GROUP_SIZE = 4
GLA_GATE_NORMALIZER = 16.0
LN_EPS = 1e-5
RMS_EPS = 1e-6
GLA_RANK_PAD = 128

VMEM_LIMIT_BYTES = 56 * 1024 * 1024
REC_CHUNK = 64
MOE_TILE = 256


def _params(sem):
    return pltpu.CompilerParams(dimension_semantics=sem, vmem_limit_bytes=VMEM_LIMIT_BYTES)


def _dot(a, b):
    return jnp.dot(a, b, preferred_element_type=F32)


def _dot_precise(a, b):
    a_hi = a.astype(BF16)
    b_hi = b.astype(BF16)
    a_lo = (a - a_hi.astype(F32)).astype(BF16)
    b_lo = (b - b_hi.astype(F32)).astype(BF16)
    return _dot(a_hi, b_hi) + _dot(a_lo, b_hi) + _dot(a_hi, b_lo)


def _dot_nt(a, b):
    return lax.dot_general(a, b, (((1,), (1,)), ((), ())), preferred_element_type=F32)


def _dot_tn(a, b):
    return lax.dot_general(a, b, (((0,), (0,)), ((), ())), preferred_element_type=F32)


def _layer_norm(z, g, b):
    mu = jnp.mean(z, axis=-1, keepdims=True)
    zc = z - mu
    var = jnp.mean(zc * zc, axis=-1, keepdims=True)
    return zc * lax.rsqrt(var + LN_EPS) * g + b


def _linear_kernel(x_ref, w_ref, o_ref):
    o_ref[...] = _dot(x_ref[...].astype(BF16), w_ref[...]).astype(o_ref.dtype)


def _linear(x, w, out_dtype, tm, rows=None):
    k = x.shape[1]
    m = x.shape[0] if rows is None else rows
    n = w.shape[1]
    return pl.pallas_call(
        _linear_kernel,
        out_shape=jax.ShapeDtypeStruct((m, n), out_dtype),
        grid=(m // tm,),
        in_specs=[pl.BlockSpec((tm, k), lambda i: (i, 0)),
                  pl.BlockSpec((k, n), lambda i: (0, 0))],
        out_specs=pl.BlockSpec((tm, n), lambda i: (i, 0)),
        compiler_params=_params(("arbitrary",)),
        name="linear",
    )(x, w)


def _kv_proj_kernel(x_ref, wk_ref, wv_ref, k_ref, v_ref):
    x = x_ref[...].astype(BF16)
    k_ref[...] = _dot(x, wk_ref[...])
    v_ref[...] = _dot(x, wv_ref[...])


def _kv_proj(mem, wk, wv, tm):
    r, d = mem.shape
    nl = wk.shape[0]
    out = jax.ShapeDtypeStruct((nl, r, d), F32)
    wspec = pl.BlockSpec((None, d, d), lambda l, i: (l, 0, 0))
    ospec = pl.BlockSpec((None, tm, d), lambda l, i: (l, i, 0))
    return pl.pallas_call(
        _kv_proj_kernel,
        out_shape=(out, out),
        grid=(nl, r // tm),
        in_specs=[pl.BlockSpec((tm, d), lambda l, i: (i, 0)), wspec, wspec],
        out_specs=(ospec, ospec),
        compiler_params=_params(("arbitrary", "arbitrary")),
        name="kv_proj",
    )(mem, wk, wv)


def _linear_res_ln_kernel(*refs, alpha, route):
    if route:
        h_ref, w_ref, x_ref, g_ref, b_ref, rw_ref, rb_ref, o_ref, cls_ref = refs
    else:
        h_ref, w_ref, x_ref, g_ref, b_ref, o_ref = refs
    c = _dot(h_ref[...].astype(BF16), w_ref[...])
    y = _layer_norm(alpha * x_ref[...] + c, g_ref[...], b_ref[...])
    o_ref[...] = y
    if route:
        cls_ref[...] = _routing_class(y, rw_ref[...], rb_ref[...])


def _linear_res_ln(h, w, x, g, b, alpha, tm, router=None):
    m, k = h.shape
    d = w.shape[1]
    row = pl.BlockSpec((1, d), lambda i: (0, 0))
    in_specs = [pl.BlockSpec((tm, k), lambda i: (i, 0)),
                pl.BlockSpec((k, d), lambda i: (0, 0)),
                pl.BlockSpec((tm, d), lambda i: (i, 0)), row, row]
    out_shape = jax.ShapeDtypeStruct((m, d), F32)
    out_specs = pl.BlockSpec((tm, d), lambda i: (i, 0))
    args = (h, w, x, g, b)
    if router is not None:
        in_specs += [pl.BlockSpec(a.shape, lambda i: (0, 0)) for a in router]
        out_shape = (out_shape, jax.ShapeDtypeStruct((1, m), I32))
        out_specs = (out_specs, pl.BlockSpec((1, tm), lambda i: (0, i)))
        args += tuple(router)
    return pl.pallas_call(
        functools.partial(_linear_res_ln_kernel, alpha=alpha, route=router is not None),
        out_shape=out_shape,
        grid=(m // tm,),
        in_specs=in_specs,
        out_specs=out_specs,
        compiler_params=_params(("arbitrary",)),
        name="linear_res_ln",
    )(*args)


def _cumsum_rows(x):
    n = x.shape[0]
    row = lax.broadcasted_iota(I32, x.shape, 0)
    s = 1
    while s < n:
        x = x + jnp.where(row >= s, pltpu.roll(x, s, 0), 0.0)
        s *= 2
    return x


def _chunk_step(q, k, v, g, state):
    c, kd = q.shape
    vd = v.shape[1]
    b = _cumsum_rows(g)
    b_last = b[c - 1:c, :]
    b_mid = b[c // 2:c // 2 + 1, :]
    vb = v.astype(BF16)
    o = _dot((q * jnp.exp(b)).astype(BF16), state.astype(BF16))
    qa = (q * jnp.exp(b - b_mid)).astype(BF16)
    ka = (k * jnp.exp(b_mid - b)).astype(BF16)
    scores = _dot_nt(qa, ka)
    ri = lax.broadcasted_iota(I32, (c, c), 0)
    ci = lax.broadcasted_iota(I32, (c, c), 1)
    scores = jnp.where(ri >= ci, scores, 0.0)
    o = o + _dot(scores.astype(BF16), vb)
    ks = (k * jnp.exp(b_last - b)).astype(BF16)
    decay = jnp.broadcast_to(jnp.exp(b_last), (kd, kd)).T
    if vd != kd:
        decay = jnp.concatenate([decay] * (vd // kd), axis=1)
    return o, state * decay + _dot_tn(ks, vb)


def _gated_rmsnorm(o, gate, gain):
    o = o * lax.rsqrt(jnp.mean(o * o, axis=-1, keepdims=True) + RMS_EPS)
    return o * gain * (gate * jax.nn.sigmoid(gate))


def _log_sigmoid(x):
    return jnp.minimum(x, 0.0) - jnp.log(1.0 + jnp.exp(-jnp.abs(x)))


def _rec_kernel(*refs, mode, heads, kd, vd, chunk, tb, nbatch, lb_row, zero_init):
    if zero_init:
        proj_ref, aux0_ref, aux1_ref, gain_ref, o_ref, s_ref = refs
        s0_ref = None
    else:
        proj_ref, s0_ref, aux0_ref, aux1_ref, gain_ref, o_ref, s_ref = refs

    @pl.when(pl.program_id(1) == 0)
    def _():
        if zero_init:
            s_ref[...] = jnp.zeros(s_ref.shape, F32)
        else:
            s_ref[...] = s0_ref[...]

    hk = heads * kd
    hv = heads * vd
    gain = gain_ref[...]
    if mode == "hgrn":
        logits = aux0_ref[...]
        e = jnp.exp(logits - jnp.max(logits, axis=0, keepdims=True))
        lb = jnp.sum(e[:lb_row + 1], axis=0, keepdims=True) / jnp.sum(e, axis=0, keepdims=True)

    def one_batch(nb):
        for sc in range(tb // chunk):
            r0 = nb * tb + sc * chunk
            rows = pl.ds(r0, chunk) if isinstance(r0, int) else pl.ds(pl.multiple_of(r0, 8), chunk)
            if mode == "gla":
                lr = proj_ref[rows, 2 * hk + 2 * hv:2 * hk + 2 * hv + GLA_RANK_PAD]
                gk_all = _dot_precise(lr, aux0_ref[...])
                gk_all = _log_sigmoid(gk_all + aux1_ref[...]) * (1.0 / GLA_GATE_NORMALIZER)
            for h in range(heads):
                if mode == "hgrn":
                    q = proj_ref[rows, h * kd:(h + 1) * kd]
                    f = proj_ref[rows, hk + h * kd:hk + (h + 1) * kd]
                    v = proj_ref[rows, 2 * hk + h * vd:2 * hk + (h + 1) * vd]
                    gate = proj_ref[rows, 2 * hk + hv + h * vd:2 * hk + hv + (h + 1) * vd]
                    lbh = lb[:, h * kd:(h + 1) * kd]
                    q = q * jax.nn.sigmoid(q) * (kd ** -0.5)
                    fg = lbh + (1.0 - lbh) * jax.nn.sigmoid(f)
                    k = 1.0 - fg
                    g = jnp.log(fg)
                else:
                    q = proj_ref[rows, h * kd:(h + 1) * kd] * (kd ** -0.5)
                    k = proj_ref[rows, hk + h * kd:hk + (h + 1) * kd]
                    v = proj_ref[rows, 2 * hk + h * vd:2 * hk + (h + 1) * vd]
                    gate = proj_ref[rows, 2 * hk + hv + h * vd:2 * hk + hv + (h + 1) * vd]
                    g = gk_all[:, h * kd:(h + 1) * kd]
                o, s_new = _chunk_step(q, k, v, g, s_ref[nb, h])
                s_ref[nb, h] = s_new
                o_ref[rows, h * vd:(h + 1) * vd] = _gated_rmsnorm(o, gate, gain).astype(o_ref.dtype)

    if nbatch == 1:
        one_batch(0)
    else:
        def body(nb, carry):
            one_batch(nb)
            return carry
        lax.fori_loop(0, nbatch, body, 0)


def _recurrence(proj, s0, aux0, aux1, gain, *, mode, batch, seq, heads, kd, vd, lb_row=0):
    n, width = proj.shape
    chunk = min(REC_CHUNK, seq)
    tb = min(2 * chunk, seq)
    nbatch = 1 if seq > tb else min(8, batch)
    nblk = seq // tb
    grid = (batch // nbatch, nblk)
    state_spec = pl.BlockSpec((nbatch, heads, kd, vd), lambda b, c: (b, 0, 0, 0))
    full2 = lambda a: pl.BlockSpec(a.shape, lambda b, c: (0, 0))
    in_specs = [pl.BlockSpec((nbatch * tb, width), lambda b, c: (b * nblk + c, 0))]
    args = [proj]
    if s0 is not None:
        in_specs.append(state_spec)
        args.append(s0)
    in_specs += [full2(aux0), full2(aux1), full2(gain)]
    args += [aux0, aux1, gain]
    kern = functools.partial(_rec_kernel, mode=mode, heads=heads, kd=kd, vd=vd, chunk=chunk,
                             tb=tb, nbatch=nbatch, lb_row=lb_row, zero_init=s0 is None)
    return pl.pallas_call(
        kern,
        out_shape=(jax.ShapeDtypeStruct((n, heads * vd), BF16 if chunk % 16 == 0 else F32),
                   jax.ShapeDtypeStruct((batch, heads, kd, vd), F32)),
        grid=grid,
        in_specs=in_specs,
        out_specs=(pl.BlockSpec((nbatch * tb, heads * vd), lambda b, c: (b * nblk + c, 0)),
                   state_spec),
        compiler_params=_params(("arbitrary", "arbitrary")),
        name="recurrence_" + mode,
    )(*args)


def _attn_kernel(q_ref, k_ref, v_ref, o_ref, *, heads, nbatch, tq, per_head):
    hd = q_ref.shape[1] // heads
    scale = hd ** -0.5
    units = [(nb, h) for nb in range(nbatch) for h in range(heads)]

    def head_rows(ref, nb, h):
        if not per_head:
            return ref[nb, :, h * hd:(h + 1) * hd].astype(BF16)
        nt = hd // 128
        period = heads * nt
        mlen = ref.shape[1] // period
        parts = [ref[nb, pl.ds(t * heads + h, mlen, stride=period), :] for t in range(nt)]
        return jnp.concatenate(parts, axis=1).astype(BF16)

    scores = [_dot_nt(q_ref[nb * tq:(nb + 1) * tq, h * hd:(h + 1) * hd].astype(BF16),
                      head_rows(k_ref, nb, h)) * scale for nb, h in units]
    probs = []
    for s in scores:
        p = jnp.exp(s - jnp.max(s, axis=-1, keepdims=True))
        probs.append((p / jnp.sum(p, axis=-1, keepdims=True)).astype(BF16))
    for (nb, h), p in zip(units, probs):
        o_ref[nb * tq:(nb + 1) * tq, h * hd:(h + 1) * hd] = _dot(p, head_rows(v_ref, nb, h)).astype(o_ref.dtype)


def _interleaved_rows(mem):
    nl, b, m, heads, hd = mem.shape
    nt = hd // 128
    return mem.reshape(nl, b, m, heads, nt, 128).transpose(0, 1, 2, 4, 3, 5).reshape(nl, b, m * nt * heads, 128)


def _attention(q, mem_k, mem_v, layer, *, batch, seq, heads):
    n, d = q.shape
    per_head = mem_k.shape[3] != d
    if seq >= 256:
        tq, nbatch = 512, 1
    else:
        tq, nbatch = seq, 2
    nblk = seq // tq
    mem_spec = pl.BlockSpec((None, nbatch) + mem_k.shape[2:], lambda b, i: (layer, b, 0, 0))
    qspec = pl.BlockSpec((nbatch * tq, d), lambda b, i: (b * nblk + i, 0))
    return pl.pallas_call(
        functools.partial(_attn_kernel, heads=heads, nbatch=nbatch, tq=tq, per_head=per_head),
        out_shape=jax.ShapeDtypeStruct((n, d), q.dtype),
        grid=(batch // nbatch, nblk),
        in_specs=[qspec, mem_spec, mem_spec],
        out_specs=qspec,
        compiler_params=_params(("arbitrary", "arbitrary")),
        name="mem_attention",
    )(q, mem_k, mem_v)


def _routing_class(x, w_pad, bias_col):
    ne = N_GROUPS * GROUP_SIZE
    logits = _dot_precise(x, w_pad).T[:ne, :]
    sel = jax.nn.sigmoid(logits) + bias_col
    rows = [sel[e:e + 1, :] for e in range(ne)]

    def first_argmax(vals):
        best_v, best_i = vals[0], jnp.zeros(vals[0].shape, I32)
        for i in range(1, len(vals)):
            better = vals[i] > best_v
            best_i = jnp.where(better, i, best_i)
            best_v = jnp.where(better, vals[i], best_v)
        return best_i

    group_scores = []
    for gi in range(N_GROUPS):
        a = rows[gi * GROUP_SIZE:(gi + 1) * GROUP_SIZE]
        top2 = None
        for i in range(GROUP_SIZE):
            for j in range(i + 1, GROUP_SIZE):
                s = a[i] + a[j]
                top2 = s if top2 is None else jnp.maximum(top2, s)
        group_scores.append(top2)
    best = first_argmax(group_scores)
    cand = []
    for j in range(GROUP_SIZE):
        cj = rows[j]
        for gi in range(1, N_GROUPS):
            cj = jnp.where(best == gi, rows[gi * GROUP_SIZE + j], cj)
        cand.append(cj)
    i1 = first_argmax(cand)
    i2 = first_argmax([jnp.where(i1 == j, -jnp.inf, cand[j]) for j in range(GROUP_SIZE)])
    lo = jnp.minimum(i1, i2)
    hi = jnp.maximum(i1, i2)
    pair = jnp.where(lo == 0, hi - 1, jnp.where(lo == 1, hi + 1, 5))
    return best * 6 + pair


_PAIR_LO = (0, 0, 0, 1, 1, 2)
_PAIR_HI = (1, 2, 3, 2, 3, 3)


def _moe_plan(cls, tile):
    n = cls.shape[0]
    ncls = N_GROUPS * 6
    max_tiles = n // tile + ncls
    shift = max(n - 1, 1).bit_length()
    keys = jnp.sort(cls * (1 << shift) + jnp.arange(n, dtype=I32))
    src = keys & ((1 << shift) - 1)
    cid = jnp.arange(ncls, dtype=I32)
    count = jnp.sum((cls[None, :] == cid[:, None]).astype(I32), axis=1)
    cstart = jnp.cumsum(count) - count
    ntile = (count + tile - 1) // tile
    tend = jnp.cumsum(ntile)
    tid = jnp.arange(max_tiles, dtype=I32)
    n_used = tend[-1]
    tcls = jnp.sum((tid[:, None] >= tend[None, :]).astype(I32), axis=1)
    last_cls = jnp.sum((n_used - 1 >= tend).astype(I32))
    tcls = jnp.where(tid < n_used, tcls, last_cls)
    within = tid - (tend - ntile)[tcls]
    tstart = cstart[tcls] + within * tile
    tcnt = jnp.clip(count[tcls] - within * tile, 0, tile)
    tcnt = jnp.where(tid < n_used, tcnt, 0)
    grp = tcls // 6
    e_lo = grp * GROUP_SIZE + jnp.asarray(_PAIR_LO, I32)[tcls % 6]
    e_hi = grp * GROUP_SIZE + jnp.asarray(_PAIR_HI, I32)[tcls % 6]
    return src, tstart, tcnt, e_lo, e_hi, n_used.reshape(1)


def _moe_kernel(src_ref, tstart_ref, tcnt_ref, elo_ref, ehi_ref, nused_ref,
                x_hbm, rw_lo_ref, rw_hi_ref, wg_lo_ref, wu_lo_ref, wd_lo_ref,
                wg_hi_ref, wu_hi_ref, wd_hi_ref, g_ref, b_ref,
                out_hbm, xbuf, ybuf, gsem, ssem, *, alpha, n_rows):
    j = pl.program_id(0)
    n_used = nused_ref[0]
    parity = lax.rem(j, 2)
    tile = xbuf.shape[1]

    def gather_row(s, r, cnt, start):
        tok = src_ref[start + jnp.minimum(r, cnt - 1)]
        pltpu.make_async_copy(x_hbm.at[pl.ds(tok, 1)], xbuf.at[s, pl.ds(r, 1)],
                              gsem.at[s]).start(priority=r % 2)

    def scatter_row(s, r, cnt, start):
        tok = src_ref[start + jnp.minimum(r, jnp.maximum(cnt - 1, 0))]
        dst = jnp.where(r < cnt, tok, n_rows + s * tile + r)
        pltpu.make_async_copy(ybuf.at[s, pl.ds(r, 1)], out_hbm.at[pl.ds(dst, 1)],
                              ssem.at[s]).start(priority=r % 2)

    def wait_gather(s):
        pltpu.make_async_copy(x_hbm.at[pl.ds(0, tile)], xbuf.at[s], gsem.at[s]).wait()

    def wait_scatter(s):
        pltpu.make_async_copy(ybuf.at[s], out_hbm.at[pl.ds(0, tile)], ssem.at[s]).wait()

    @pl.when(j == 0)
    def _():
        ybuf[...] = jnp.zeros(ybuf.shape, F32)
        cnt0, start0 = tcnt_ref[0], tstart_ref[0]
        for r in range(tile):
            gather_row(0, r, cnt0, start0)
        for r in range(tile):
            scatter_row(0, r, 0, 0)

    def issue_neighbours(other):
        jn = jnp.minimum(j + 1, n_used - 1)
        jp = jnp.maximum(j - 1, 0)
        cnt_n, start_n = tcnt_ref[jn], tstart_ref[jn]
        cnt_p, start_p = jnp.where(j >= 1, tcnt_ref[jp], 0), tstart_ref[jp]
        for r in range(tile):
            scatter_row(other, r, cnt_p, start_p)
        for r in range(tile):
            gather_row(other, r, cnt_n, start_n)

    def compute_tile(slot):
        other = 1 - slot
        wait_gather(slot)
        x = xbuf[slot]
        xb = x.astype(BF16)

        def expert(rw_ref, wg_ref, wu_ref, wd_ref):
            h = _dot(xb, wg_ref[...])
            h = h * jax.nn.sigmoid(h) * _dot(xb, wu_ref[...])
            y = _dot(h.astype(BF16), wd_ref[...])
            score = jax.nn.sigmoid(jnp.sum(x * rw_ref[...], axis=-1, keepdims=True))
            return y, score

        y_lo, s_lo = expert(rw_lo_ref, wg_lo_ref, wu_lo_ref, wd_lo_ref)
        y_hi, s_hi = expert(rw_hi_ref, wg_hi_ref, wu_hi_ref, wd_hi_ref)
        tot = s_lo + s_hi
        m = (s_lo / tot) * y_lo + (s_hi / tot) * y_hi
        wait_scatter(slot)
        ybuf[slot] = _layer_norm(alpha * x + m, g_ref[...], b_ref[...])

        @pl.when(j == n_used - 1)
        def _():
            cnt, start = tcnt_ref[j], tstart_ref[j]
            for r in range(tile):
                scatter_row(slot, r, cnt, start)
            wait_scatter(slot)
            wait_scatter(other)
            wait_gather(other)

    for slot in range(2):
        active = jnp.logical_and(j < n_used, parity == slot)
        pl.when(active)(functools.partial(issue_neighbours, 1 - slot))
        pl.when(active)(functools.partial(compute_tile, slot))


def _moe(x, n, cls, router_wt3, wg, wu, wd, layer, g, b, alpha):
    d = x.shape[1]
    de = wg.shape[3]
    tile = MOE_TILE
    src, tstart, tcnt, e_lo, e_hi, n_used = _moe_plan(cls, tile)
    max_tiles = tstart.shape[0]

    def by_lo(shape):
        return pl.BlockSpec(shape, lambda j, s, ts, tc, el, eh, nu: (el[j], 0, 0))

    def by_hi(shape):
        return pl.BlockSpec(shape, lambda j, s, ts, tc, el, eh, nu: (eh[j], 0, 0))

    def w_lo(r, c):
        return pl.BlockSpec((None, None, r, c), lambda j, s, ts, tc, el, eh, nu: (layer, el[j], 0, 0))

    def w_hi(r, c):
        return pl.BlockSpec((None, None, r, c), lambda j, s, ts, tc, el, eh, nu: (layer, eh[j], 0, 0))

    row = pl.BlockSpec((1, d), lambda j, s, ts, tc, el, eh, nu: (0, 0))
    grid_spec = pltpu.PrefetchScalarGridSpec(
        num_scalar_prefetch=6,
        grid=(max_tiles,),
        in_specs=[pl.BlockSpec(memory_space=pl.ANY),
                  by_lo((None, 1, d)), by_hi((None, 1, d)),
                  w_lo(d, de), w_lo(d, de), w_lo(de, d),
                  w_hi(d, de), w_hi(d, de), w_hi(de, d),
                  row, row],
        out_specs=pl.BlockSpec(memory_space=pl.ANY),
        scratch_shapes=[pltpu.VMEM((2, tile, d), F32), pltpu.VMEM((2, tile, d), F32),
                        pltpu.SemaphoreType.DMA((2,)), pltpu.SemaphoreType.DMA((2,))],
    )
    return pl.pallas_call(
        functools.partial(_moe_kernel, alpha=alpha, n_rows=n),
        out_shape=jax.ShapeDtypeStruct((n + 2 * tile, d), F32),
        grid_spec=grid_spec,
        compiler_params=_params(("arbitrary",)),
        name="grouped_moe",
    )(src, tstart, tcnt, e_lo, e_hi, n_used,
      x, router_wt3, router_wt3, wg, wu, wd, wg, wu, wd, g, b)


def kernel(x_prompt, x_sample, state_hgrn, state_gla, cache_mem_k, cache_mem_v, mem_prompt, hgrn_w_in, hgrn_lb_logits, hgrn_norm_g, hgrn_w_out, gla_w_in, gla_w_gk2, gla_b_gk2, gla_norm_g, gla_w_out, xattn_w_q, xattn_w_kv, xattn_w_o, router_w, router_bias, moe_w_gate, moe_w_up, moe_w_down, ln_g, ln_b):
    batch, seq, d = x_prompt.shape
    dec_batch, dec_seq, _ = x_sample.shape
    depth = ln_g.shape[0]
    alpha = (2 * depth) ** 0.25
    a_heads, a_key, a_val = state_hgrn.shape[2:]
    b_heads, b_key, b_val = state_gla.shape[2:]
    mem_len, x_heads = cache_mem_k.shape[2], cache_mem_k.shape[3]
    rank = gla_w_gk2.shape[1]
    b_main = 2 * b_heads * b_key + b_heads * b_val + d

    hgrn_w_in_b = hgrn_w_in.astype(BF16)
    hgrn_w_out_b = hgrn_w_out.astype(BF16)
    gla_w_in_b = jnp.concatenate(
        [gla_w_in, jnp.zeros(gla_w_in.shape[:2] + (GLA_RANK_PAD - rank,), F32)], axis=-1).astype(BF16)
    gla_w_gk2_p = jnp.concatenate(
        [gla_w_gk2, jnp.zeros((gla_w_gk2.shape[0], GLA_RANK_PAD - rank, gla_w_gk2.shape[2]), F32)], axis=1)
    gla_w_out_b = gla_w_out.astype(BF16)
    w_q_b = xattn_w_q.astype(BF16)
    w_k_b = xattn_w_kv[:, :, :d].astype(BF16)
    w_v_b = xattn_w_kv[:, :, d:].astype(BF16)
    w_o_b = xattn_w_o.astype(BF16)
    wg_b = moe_w_gate.astype(BF16)
    wu_b = moe_w_up.astype(BF16)
    wd_b = moe_w_down.astype(BF16)
    n_experts = router_w.shape[1]
    router_wt3 = router_w.T[:, None, :]
    router = (jnp.concatenate([router_w, jnp.zeros((d, 128 - n_experts), F32)], axis=1), router_bias[:, None])
    assert b_main + rank == gla_w_in.shape[2]

    mem_k_p, mem_v_p = _kv_proj(mem_prompt.reshape(batch * mem_len, d), w_k_b, w_v_b, 512)
    mem_k_prompt = mem_k_p.reshape(depth, batch, mem_len, x_heads, d // x_heads)
    mem_v_prompt = mem_v_p.reshape(depth, batch, mem_len, x_heads, d // x_heads)

    def run_trunk(x3, states_a, states_b, mem_ks, mem_vs):
        nb, ns, _ = x3.shape
        n = nb * ns
        x = x3.reshape(n, d)
        new_a, new_b = [], []
        for l in range(depth):
            j = l // 2
            row = lambda a, i: a[l, i][None, :]
            if l % 2 == 0:
                proj = _linear(x, hgrn_w_in_b[j], F32, 512, rows=n)
                o, s = _recurrence(proj, None if states_a is None else states_a[j],
                                   hgrn_lb_logits, hgrn_lb_logits[:1], hgrn_norm_g[j][None, :],
                                   mode="hgrn", batch=nb, seq=ns, heads=a_heads, kd=a_key, vd=a_val,
                                   lb_row=l)
                new_a.append(s)
                w_out = hgrn_w_out_b[j]
            else:
                proj = _linear(x, gla_w_in_b[j], F32, 512, rows=n)
                o, s = _recurrence(proj, None if states_b is None else states_b[j],
                                   gla_w_gk2_p[j], gla_b_gk2[j][None, :], gla_norm_g[j][None, :],
                                   mode="gla", batch=nb, seq=ns, heads=b_heads, kd=b_key, vd=b_val)
                new_b.append(s)
                w_out = gla_w_out_b[j]
            x = _linear_res_ln(o, w_out, x, row(ln_g, 0), row(ln_b, 0), alpha, 512)
            q = _linear(x, w_q_b[l], BF16 if ns >= 256 else F32, 512)
            c = _attention(q, mem_ks, mem_vs, l, batch=nb, seq=ns, heads=x_heads)
            x, cls = _linear_res_ln(c, w_o_b[l], x, row(ln_g, 1), row(ln_b, 1), alpha, 512, router=router)
            x = _moe(x, n, cls[0], router_wt3, wg_b, wu_b, wd_b, l, row(ln_g, 2), row(ln_b, 2), alpha)
        return x[:n].reshape(nb, ns, d), jnp.stack(new_a), jnp.stack(new_b)

    y_prompt, state_hgrn_prompt, state_gla_prompt = run_trunk(
        x_prompt, None, None, mem_k_p.reshape(depth, batch, mem_len, d), mem_v_p.reshape(depth, batch, mem_len, d))
    y_sample, state_hgrn_sample, state_gla_sample = run_trunk(
        x_sample, state_hgrn, state_gla, _interleaved_rows(cache_mem_k), _interleaved_rows(cache_mem_v))
    return (y_prompt, y_sample, state_hgrn_prompt, state_gla_prompt, mem_k_prompt, mem_v_prompt,
            state_hgrn_sample, state_gla_sample)
```

```python
import functools

import jax
import jax.numpy as jnp
from jax import lax
from jax.experimental import pallas as pl
from jax.experimental.pallas import tpu as pltpu

F32 = jnp.float32
BF16 = jnp.bfloat16
I32 = jnp.int32

N_GROUPS = 4
GROUP_SIZE = 4
GLA_GATE_NORMALIZER = 16.0
LN_EPS = 1e-5
RMS_EPS = 1e-6
GLA_RANK_PAD = 128

VMEM_LIMIT_BYTES = 56 * 1024 * 1024
REC_CHUNK = 64
MOE_TILE = 256

_HI = lax.Precision.HIGHEST


def _params(sem):
    return pltpu.CompilerParams(dimension_semantics=sem, vmem_limit_bytes=VMEM_LIMIT_BYTES)


def _dot(a, b):
    return jnp.dot(a, b, preferred_element_type=F32)


def _dot_nt(a, b):
    return lax.dot_general(a, b, (((1,), (1,)), ((), ())), preferred_element_type=F32)


def _dot_tn(a, b):
    return lax.dot_general(a, b, (((0,), (0,)), ((), ())), preferred_element_type=F32)


def _layer_norm(z, g, b):
    mu = jnp.mean(z, axis=-1, keepdims=True)
    zc = z - mu
    var = jnp.mean(zc * zc, axis=-1, keepdims=True)
    return zc * lax.rsqrt(var + LN_EPS) * g + b


def _sigmoid(x):
    return 0.5 * jnp.tanh(0.5 * x) + 0.5


def _log_sigmoid(x):
    return jnp.minimum(x, 0.0) - jnp.log(1.0 + jnp.exp(-jnp.abs(x)))


def _linear_kernel(x_ref, w_ref, o_ref):
    o_ref[...] = _dot(x_ref[...].astype(BF16), w_ref[...]).astype(o_ref.dtype)


def _linear(x, w, out_dtype, tm):
    m, k = x.shape
    n = w.shape[1]
    return pl.pallas_call(
        _linear_kernel,
        out_shape=jax.ShapeDtypeStruct((m, n), out_dtype),
        grid=(m // tm,),
        in_specs=[pl.BlockSpec((tm, k), lambda i: (i, 0)),
                  pl.BlockSpec((k, n), lambda i: (0, 0))],
        out_specs=pl.BlockSpec((tm, n), lambda i: (i, 0)),
        compiler_params=_params(("arbitrary",)),
        name="linear",
    )(x, w)


def _kv_proj_kernel(x_ref, wk_ref, wv_ref, k_ref, v_ref):
    x = x_ref[...].astype(BF16)
    k_ref[...] = _dot(x, wk_ref[...])
    v_ref[...] = _dot(x, wv_ref[...])


def _kv_proj(mem, wk, wv, tm):
    r, d = mem.shape
    nl = wk.shape[0]
    out = jax.ShapeDtypeStruct((nl, r, d), F32)
    wspec = pl.BlockSpec((None, d, d), lambda l, i: (l, 0, 0))
    ospec = pl.BlockSpec((None, tm, d), lambda l, i: (l, i, 0))
    return pl.pallas_call(
        _kv_proj_kernel,
        out_shape=(out, out),
        grid=(nl, r // tm),
        in_specs=[pl.BlockSpec((tm, d), lambda l, i: (i, 0)), wspec, wspec],
        out_specs=(ospec, ospec),
        compiler_params=_params(("arbitrary", "arbitrary")),
        name="kv_proj",
    )(mem, wk, wv)


def _linear_res_ln_kernel(h_ref, w_ref, x_ref, g_ref, b_ref, o_ref, *, alpha):
    c = _dot(h_ref[...].astype(BF16), w_ref[...])
    o_ref[...] = _layer_norm(alpha * x_ref[...] + c, g_ref[...], b_ref[...])


def _linear_res_ln(h, w, x, g, b, alpha, tm):
    m, k = h.shape
    d = w.shape[1]
    row = pl.BlockSpec((1, d), lambda i: (0, 0))
    return pl.pallas_call(
        functools.partial(_linear_res_ln_kernel, alpha=alpha),
        out_shape=jax.ShapeDtypeStruct((m, d), F32),
        grid=(m // tm,),
        in_specs=[pl.BlockSpec((tm, k), lambda i: (i, 0)),
                  pl.BlockSpec((k, d), lambda i: (0, 0)),
                  pl.BlockSpec((tm, d), lambda i: (i, 0)), row, row],
        out_specs=pl.BlockSpec((tm, d), lambda i: (i, 0)),
        compiler_params=_params(("arbitrary",)),
        name="linear_res_ln",
    )(h, w, x, g, b)


def _cumsum_rows(x):
    n = x.shape[0]
    row = lax.broadcasted_iota(I32, x.shape, 0)
    s = 1
    while s < n:
        x = x + jnp.where(row >= s, pltpu.roll(x, s, 0), 0.0)
        s *= 2
    return x


def _chunk_step(q, k, v, g, state):
    c, kd = q.shape
    vd = v.shape[1]
    b = _cumsum_rows(g)
    b_last = b[c - 1:c, :]
    b_mid = b[c // 2:c // 2 + 1, :]
    vb = v.astype(BF16)
    qa = q * jnp.exp(b - b_mid)
    ka = k * jnp.exp(b_mid - b)
    o = _dot((qa * jnp.exp(b_mid)).astype(BF16), state.astype(BF16))
    scores = _dot_nt(qa.astype(BF16), ka.astype(BF16))
    ri = lax.broadcasted_iota(I32, (c, c), 0)
    ci = lax.broadcasted_iota(I32, (c, c), 1)
    scores = jnp.where(ri >= ci, scores, 0.0)
    o = o + _dot(scores.astype(BF16), vb)
    ks = (ka * jnp.exp(b_last - b_mid)).astype(BF16)
    decay = jnp.broadcast_to(jnp.exp(b_last), (kd, kd)).T
    if vd != kd:
        decay = jnp.concatenate([decay] * (vd // kd), axis=1)
    return o, state * decay + _dot_tn(ks, vb)


def _gated_rmsnorm(o, gate, gain):
    o = o * lax.rsqrt(jnp.mean(o * o, axis=-1, keepdims=True) + RMS_EPS)
    return o * gain * (gate * _sigmoid(gate))


def _lower_bound(logits, lb_row):
    e = jnp.exp(logits - jnp.max(logits, axis=0, keepdims=True))
    return jnp.sum(e[:lb_row + 1], axis=0, keepdims=True) / jnp.sum(e, axis=0, keepdims=True)


def _gla_gates(lr, w_gk2, b_gk2):
    z = jnp.dot(lr, w_gk2, precision=_HI, preferred_element_type=F32) + b_gk2
    return _log_sigmoid(z) * (1.0 / GLA_GATE_NORMALIZER)


def _head_inputs(mode, proj_ref, rows, h, heads, kd, vd, lb, gk_all):
    hk = heads * kd
    hv = heads * vd
    v = proj_ref[rows, 2 * hk + h * vd:2 * hk + (h + 1) * vd]
    gate = proj_ref[rows, 2 * hk + hv + h * vd:2 * hk + hv + (h + 1) * vd]
    if mode == "hgrn":
        q = proj_ref[rows, h * kd:(h + 1) * kd]
        f = proj_ref[rows, hk + h * kd:hk + (h + 1) * kd]
        lbh = lb[:, h * kd:(h + 1) * kd]
        q = q * _sigmoid(q) * (kd ** -0.5)
        fg = lbh + (1.0 - lbh) * _sigmoid(f)
        return q, 1.0 - fg, v, jnp.log(fg), gate
    q = proj_ref[rows, h * kd:(h + 1) * kd] * (kd ** -0.5)
    k = proj_ref[rows, hk + h * kd:hk + (h + 1) * kd]
    return q, k, v, gk_all[:, h * kd:(h + 1) * kd], gate


def _rec_kernel(proj_ref, s0_ref, aux0_ref, aux1_ref, gain_ref, o_ref, s_ref, *,
                mode, heads, kd, vd, chunk, tb, nbatch, lb_row):
    @pl.when(pl.program_id(1) == 0)
    def _():
        s_ref[...] = s0_ref[...]

    hk = heads * kd
    hv = heads * vd
    gain = gain_ref[...]
    lb = _lower_bound(aux0_ref[...], lb_row) if mode == "hgrn" else None

    def one_batch(nb):
        for sc in range(tb // chunk):
            r0 = nb * tb + sc * chunk
            rows = pl.ds(r0, chunk) if isinstance(r0, int) else pl.ds(pl.multiple_of(r0, 8), chunk)
            gk_all = None
            if mode == "gla":
                lr = proj_ref[rows, 2 * hk + 2 * hv:2 * hk + 2 * hv + GLA_RANK_PAD]
                gk_all = _gla_gates(lr, aux0_ref[...], aux1_ref[...])
            for h in range(heads):
                q, k, v, g, gate = _head_inputs(mode, proj_ref, rows, h, heads, kd, vd, lb, gk_all)
                o, s_new = _chunk_step(q, k, v, g, s_ref[nb, h])
                s_ref[nb, h] = s_new
                o_ref[rows, h * vd:(h + 1) * vd] = _gated_rmsnorm(o, gate, gain)

    if nbatch == 1:
        one_batch(0)
    else:
        def body(nb, carry):
            one_batch(nb)
            return carry
        lax.fori_loop(0, nbatch, body, 0)


def _recurrence(proj, s0, aux0, aux1, gain, *, mode, batch, seq, heads, kd, vd, lb_row=0):
    n, width = proj.shape
    chunk = min(REC_CHUNK, seq)
    tb = min(2 * chunk, seq)
    nbatch = 1 if seq > tb else min(8, batch)
    nblk = seq // tb
    grid = (batch // nbatch, nblk)
    state_spec = pl.BlockSpec((nbatch, heads, kd, vd), lambda b, c: (b, 0, 0, 0))
    full2 = lambda a: pl.BlockSpec(a.shape, lambda b, c: (0, 0))
    kern = functools.partial(_rec_kernel, mode=mode, heads=heads, kd=kd, vd=vd, chunk=chunk,
                             tb=tb, nbatch=nbatch, lb_row=lb_row)
    return pl.pallas_call(
        kern,
        out_shape=(jax.ShapeDtypeStruct((n, heads * vd), F32),
                   jax.ShapeDtypeStruct((batch, heads, kd, vd), F32)),
        grid=grid,
        in_specs=[pl.BlockSpec((nbatch * tb, width), lambda b, c: (b * nblk + c, 0)), state_spec,
                  full2(aux0), full2(aux1), full2(gain)],
        out_specs=(pl.BlockSpec((nbatch * tb, heads * vd), lambda b, c: (b * nblk + c, 0)),
                   state_spec),
        compiler_params=_params(("arbitrary", "arbitrary")),
        name="recurrence_" + mode,
    )(proj, s0, aux0, aux1, gain)


def _mixer_kernel(x_ref, w_in_ref, aux0_ref, aux1_ref, gain_ref, w_out_ref, g_ref, b_ref,
                  y_ref, s_ref, proj_scr, o_scr, *, mode, heads, kd, vd, chunk, lb_row, alpha):
    @pl.when(pl.program_id(1) == 0)
    def _():
        s_ref[...] = jnp.zeros(s_ref.shape, F32)

    tb = x_ref.shape[0]
    hk = heads * kd
    hv = heads * vd
    gain = gain_ref[...]
    x = x_ref[...]
    xb = x.astype(BF16)
    group = 256 // kd
    seg_starts = (0, hk, 2 * hk, 2 * hk + hv)
    seg_widths = (kd, kd, vd, vd)

    def project(gi):
        for start, w in zip(seg_starts, seg_widths):
            cols = slice(start + gi * group * w, start + (gi + 1) * group * w)
            proj_scr[:, cols] = _dot(xb, w_in_ref[:, cols])

    lb = None
    gk_blocks = None
    if mode == "hgrn":
        lb = _lower_bound(aux0_ref[...], lb_row)
    else:
        lr = _dot(xb, w_in_ref[:, 2 * hk + 2 * hv:2 * hk + 2 * hv + GLA_RANK_PAD])
        gk_full = _gla_gates(lr, aux0_ref[...], aux1_ref[...])
        gk_blocks = [gk_full[sc * chunk:(sc + 1) * chunk, :] for sc in range(tb // chunk)]

    n_groups = heads // group
    project(0)
    for gi in range(n_groups):
        if gi + 1 < n_groups:
            project(gi + 1)
        for sc in range(tb // chunk):
            rows = pl.ds(sc * chunk, chunk)
            for h in range(gi * group, (gi + 1) * group):
                q, k, v, g, gate = _head_inputs(mode, proj_scr, rows, h, heads, kd, vd, lb,
                                                None if gk_blocks is None else gk_blocks[sc])
                o, s_new = _chunk_step(q, k, v, g, s_ref[0, h])
                s_ref[0, h] = s_new
                o_scr[rows, h * vd:(h + 1) * vd] = _gated_rmsnorm(o, gate, gain).astype(BF16)
    c = _dot(o_scr[...], w_out_ref[...])
    y_ref[...] = _layer_norm(alpha * x + c, g_ref[...], b_ref[...])


def _mixer(x, w_in, aux0, aux1, gain, w_out, g, b, alpha, *, mode, batch, seq, heads, kd, vd, lb_row=0):
    n, d = x.shape
    width = w_in.shape[1]
    tb = 4 * REC_CHUNK
    nblk = seq // tb
    const = lambda a: pl.BlockSpec(a.shape, lambda bb, c: (0, 0))
    rows = pl.BlockSpec((tb, d), lambda bb, c: (bb * nblk + c, 0))
    kern = functools.partial(_mixer_kernel, mode=mode, heads=heads, kd=kd, vd=vd, chunk=REC_CHUNK,
                             lb_row=lb_row, alpha=alpha)
    return pl.pallas_call(
        kern,
        out_shape=(jax.ShapeDtypeStruct((n, d), F32),
                   jax.ShapeDtypeStruct((batch, heads, kd, vd), F32)),
        grid=(batch, nblk),
        in_specs=[rows, const(w_in), const(aux0), const(aux1), const(gain), const(w_out), const(g), const(b)],
        out_specs=(rows, pl.BlockSpec((1, heads, kd, vd), lambda bb, c: (bb, 0, 0, 0))),
        scratch_shapes=[pltpu.VMEM((tb, width), F32), pltpu.VMEM((tb, heads * vd), BF16)],
        compiler_params=_params(("arbitrary", "arbitrary")),
        name="mixer_" + mode,
    )(x, w_in, aux0, aux1, gain, w_out, g, b)


def _attn_kernel(q_ref, k_ref, v_ref, o_ref, *, heads, nbatch, tq, per_head):
    hd = q_ref.shape[1] // heads
    scale = hd ** -0.5
    units = [(nb, h) for nb in range(nbatch) for h in range(heads)]

    def head_rows(ref, nb, h):
        if not per_head:
            return ref[nb, :, h * hd:(h + 1) * hd].astype(BF16)
        nt = hd // 128
        period = heads * nt
        mlen = ref.shape[1] // period
        parts = [ref[nb, pl.ds(t * heads + h, mlen, stride=period), :] for t in range(nt)]
        return jnp.concatenate(parts, axis=1).astype(BF16)

    scores = [_dot_nt(q_ref[nb * tq:(nb + 1) * tq, h * hd:(h + 1) * hd].astype(BF16),
                      head_rows(k_ref, nb, h)) * scale for nb, h in units]
    probs = []
    for s in scores:
        p = jnp.exp(s - jnp.max(s, axis=-1, keepdims=True))
        probs.append((p / jnp.sum(p, axis=-1, keepdims=True)).astype(BF16))
    for (nb, h), p in zip(units, probs):
        o_ref[nb * tq:(nb + 1) * tq, h * hd:(h + 1) * hd] = _dot(p, head_rows(v_ref, nb, h)).astype(o_ref.dtype)


def _interleaved_rows(mem):
    nl, b, m, heads, hd = mem.shape
    nt = hd // 128
    return mem.reshape(nl, b, m, heads, nt, 128).transpose(0, 1, 2, 4, 3, 5).reshape(nl, b, m * nt * heads, 128)


def _attention(q, mem_k, mem_v, layer, *, batch, seq, heads):
    n, d = q.shape
    per_head = mem_k.shape[3] != d
    if seq >= 256:
        tq, nbatch = 512, 1
    else:
        tq, nbatch = seq, 2
    nblk = seq // tq
    mem_spec = pl.BlockSpec((None, nbatch) + mem_k.shape[2:], lambda b, i: (layer, b, 0, 0))
    qspec = pl.BlockSpec((nbatch * tq, d), lambda b, i: (b * nblk + i, 0))
    return pl.pallas_call(
        functools.partial(_attn_kernel, heads=heads, nbatch=nbatch, tq=tq, per_head=per_head),
        out_shape=jax.ShapeDtypeStruct((n, d), q.dtype),
        grid=(batch // nbatch, nblk),
        in_specs=[qspec, mem_spec, mem_spec],
        out_specs=qspec,
        compiler_params=_params(("arbitrary", "arbitrary")),
        name="mem_attention",
    )(q, mem_k, mem_v)


def _router_kernel(x_ref, wt_ref, bias_ref, cls_ref):
    logits = lax.dot_general(wt_ref[...], x_ref[...], (((1,), (1,)), ((), ())),
                             precision=_HI, preferred_element_type=F32)
    sel = jax.nn.sigmoid(logits) + bias_ref[...]
    rows = [sel[e:e + 1, :] for e in range(N_GROUPS * GROUP_SIZE)]

    def first_argmax(vals):
        best_v, best_i = vals[0], jnp.zeros(vals[0].shape, I32)
        for i in range(1, len(vals)):
            better = vals[i] > best_v
            best_i = jnp.where(better, i, best_i)
            best_v = jnp.where(better, vals[i], best_v)
        return best_i

    group_scores = []
    for gi in range(N_GROUPS):
        a = rows[gi * GROUP_SIZE:(gi + 1) * GROUP_SIZE]
        top2 = None
        for i in range(GROUP_SIZE):
            for j in range(i + 1, GROUP_SIZE):
                s = a[i] + a[j]
                top2 = s if top2 is None else jnp.maximum(top2, s)
        group_scores.append(top2)
    best = first_argmax(group_scores)
    cand = []
    for j in range(GROUP_SIZE):
        cj = rows[j]
        for gi in range(1, N_GROUPS):
            cj = jnp.where(best == gi, rows[gi * GROUP_SIZE + j], cj)
        cand.append(cj)
    i1 = first_argmax(cand)
    i2 = first_argmax([jnp.where(i1 == j, -jnp.inf, cand[j]) for j in range(GROUP_SIZE)])
    lo = jnp.minimum(i1, i2)
    hi = jnp.maximum(i1, i2)
    pair = jnp.where(lo == 0, hi - 1, jnp.where(lo == 1, hi + 1, 5))
    cls_ref[...] = best * 6 + pair


def _router(x, router_wt, bias_col, tm):
    n, d = x.shape
    ne = router_wt.shape[0]
    return pl.pallas_call(
        _router_kernel,
        out_shape=jax.ShapeDtypeStruct((1, n), I32),
        grid=(n // tm,),
        in_specs=[pl.BlockSpec((tm, d), lambda i: (i, 0)),
                  pl.BlockSpec((ne, d), lambda i: (0, 0)),
                  pl.BlockSpec((ne, 1), lambda i: (0, 0))],
        out_specs=pl.BlockSpec((1, tm), lambda i: (0, i)),
        compiler_params=_params(("arbitrary",)),
        name="router",
    )(x, router_wt, bias_col)


_PAIR_LO = (0, 0, 0, 1, 1, 2)
_PAIR_HI = (1, 2, 3, 2, 3, 3)


def _moe_plan(cls, tile):
    n = cls.shape[0]
    ncls = N_GROUPS * 6
    max_tiles = n // tile + ncls
    shift = max(n - 1, 1).bit_length()
    keys = jnp.sort(cls * (1 << shift) + jnp.arange(n, dtype=I32))
    src = keys & ((1 << shift) - 1)
    cid = jnp.arange(ncls, dtype=I32)
    count = jnp.sum((cls[None, :] == cid[:, None]).astype(I32), axis=1)
    cstart = jnp.cumsum(count) - count
    ntile = (count + tile - 1) // tile
    tend = jnp.cumsum(ntile)
    tid = jnp.arange(max_tiles, dtype=I32)
    n_used = tend[-1]
    tcls = jnp.sum((tid[:, None] >= tend[None, :]).astype(I32), axis=1)
    last_cls = jnp.sum((n_used - 1 >= tend).astype(I32))
    tcls = jnp.where(tid < n_used, tcls, last_cls)
    within = tid - (tend - ntile)[tcls]
    tstart = cstart[tcls] + within * tile
    tcnt = jnp.clip(count[tcls] - within * tile, 0, tile)
    tcnt = jnp.where(tid < n_used, tcnt, 0)
    grp = tcls // 6
    e_lo = grp * GROUP_SIZE + jnp.asarray(_PAIR_LO, I32)[tcls % 6]
    e_hi = grp * GROUP_SIZE + jnp.asarray(_PAIR_HI, I32)[tcls % 6]
    return src, tstart, tcnt, e_lo, e_hi, n_used.reshape(1)


def _moe_kernel(src_ref, tstart_ref, tcnt_ref, elo_ref, ehi_ref, nused_ref,
                x_hbm, rw_lo_ref, rw_hi_ref, wg_lo_ref, wu_lo_ref, wd_lo_ref,
                wg_hi_ref, wu_hi_ref, wd_hi_ref, g_ref, b_ref,
                out_hbm, xbuf, ybuf, gsem, ssem, *, alpha):
    j = pl.program_id(0)
    n_used = nused_ref[0]
    slot = lax.rem(j, 2)
    ngroups, _, d = xbuf.shape[1:]

    def for_rows(cnt, fn):
        ngrp = lax.shift_right_logical(cnt, 3)

        def group(gi, c):
            base = gi * 8
            for u in range(8):
                fn(gi, u, base + u, u % 2)
            return c

        def single(r, c):
            fn(lax.shift_right_logical(r, 3), jnp.bitwise_and(r, 7), r, 0)
            return c

        lax.fori_loop(0, ngrp, group, 0)
        lax.fori_loop(ngrp * 8, cnt, single, 0)

    def wait_rows(cnt, group_copy, row_copy):
        ngrp = lax.shift_right_logical(cnt, 3)

        @pl.when(ngrp > 0)
        def _():
            group_copy(ngrp).wait()

        def single(r, c):
            row_copy(lax.shift_right_logical(r, 3), jnp.bitwise_and(r, 7)).wait()
            return c
        lax.fori_loop(ngrp * 8, cnt, single, 0)

    def start_gather(t, s):
        start = tstart_ref[t]

        def fn(gi, u, r, prio):
            tok = src_ref[start + r]
            pltpu.make_async_copy(x_hbm.at[pl.ds(tok, 1)], xbuf.at[s, gi, pl.ds(u, 1)],
                                  gsem.at[s]).start(priority=prio)
        for_rows(tcnt_ref[t], fn)

    def wait_gather(t, s):
        wait_rows(tcnt_ref[t],
                  lambda n: pltpu.make_async_copy(xbuf.at[s, pl.ds(0, n)], xbuf.at[s, pl.ds(0, n)], gsem.at[s]),
                  lambda gi, u: pltpu.make_async_copy(xbuf.at[s, gi, pl.ds(u, 1)], xbuf.at[s, gi, pl.ds(u, 1)],
                                                      gsem.at[s]))

    def start_scatter(t, s):
        start = tstart_ref[t]

        def fn(gi, u, r, prio):
            tok = src_ref[start + r]
            pltpu.make_async_copy(ybuf.at[s, gi, pl.ds(u, 1)], out_hbm.at[pl.ds(tok, 1)],
                                  ssem.at[s]).start(priority=prio)
        for_rows(tcnt_ref[t], fn)

    def wait_scatter(t, s):
        wait_rows(tcnt_ref[t],
                  lambda n: pltpu.make_async_copy(ybuf.at[s, pl.ds(0, n)], ybuf.at[s, pl.ds(0, n)], ssem.at[s]),
                  lambda gi, u: pltpu.make_async_copy(ybuf.at[s, gi, pl.ds(u, 1)], ybuf.at[s, gi, pl.ds(u, 1)],
                                                      ssem.at[s]))

    @pl.when(j == 0)
    def _():
        xbuf[...] = jnp.zeros(xbuf.shape, F32)
        start_gather(0, 0)

    @pl.when(j < n_used)
    def _():
        @pl.when(j + 1 < n_used)
        def _():
            start_gather(j + 1, 1 - slot)

        wait_gather(j, slot)

        @pl.when(j >= 2)
        def _():
            wait_scatter(j - 2, slot)

        x = xbuf[slot].reshape(ngroups * 8, d)
        xb = x.astype(BF16)

        def expert(rw_ref, wg_ref, wu_ref, wd_ref):
            h = _dot(xb, wg_ref[...])
            h = h * _sigmoid(h) * _dot(xb, wu_ref[...])
            y = _dot(h.astype(BF16), wd_ref[...])
            score = _sigmoid(jnp.sum(x * rw_ref[...], axis=-1, keepdims=True))
            return y, score

        y_lo, s_lo = expert(rw_lo_ref, wg_lo_ref, wu_lo_ref, wd_lo_ref)
        y_hi, s_hi = expert(rw_hi_ref, wg_hi_ref, wu_hi_ref, wd_hi_ref)
        tot = s_lo + s_hi
        m = (s_lo / tot) * y_lo + (s_hi / tot) * y_hi
        ybuf[slot] = _layer_norm(alpha * x + m, g_ref[...], b_ref[...]).reshape(ngroups, 8, d)
        start_scatter(j, slot)

        @pl.when(j == n_used - 1)
        def _():
            @pl.when(j >= 1)
            def _():
                wait_scatter(j - 1, 1 - slot)
            wait_scatter(j, slot)


def _moe(x, cls, router_wt3, wg, wu, wd, layer, g, b, alpha):
    n, d = x.shape
    de = wg.shape[3]
    tile = MOE_TILE
    src, tstart, tcnt, e_lo, e_hi, n_used = _moe_plan(cls, tile)
    max_tiles = tstart.shape[0]

    def by_lo(shape):
        return pl.BlockSpec(shape, lambda j, s, ts, tc, el, eh, nu: (el[j], 0, 0))

    def by_hi(shape):
        return pl.BlockSpec(shape, lambda j, s, ts, tc, el, eh, nu: (eh[j], 0, 0))

    def w_lo(r, c):
        return pl.BlockSpec((None, None, r, c), lambda j, s, ts, tc, el, eh, nu: (layer, el[j], 0, 0))

    def w_hi(r, c):
        return pl.BlockSpec((None, None, r, c), lambda j, s, ts, tc, el, eh, nu: (layer, eh[j], 0, 0))

    row = pl.BlockSpec((1, d), lambda j, s, ts, tc, el, eh, nu: (0, 0))
    grid_spec = pltpu.PrefetchScalarGridSpec(
        num_scalar_prefetch=6,
        grid=(max_tiles,),
        in_specs=[pl.BlockSpec(memory_space=pl.ANY),
                  by_lo((None, 1, d)), by_hi((None, 1, d)),
                  w_lo(d, de), w_lo(d, de), w_lo(de, d),
                  w_hi(d, de), w_hi(d, de), w_hi(de, d),
                  row, row],
        out_specs=pl.BlockSpec(memory_space=pl.ANY),
        scratch_shapes=[pltpu.VMEM((2, tile // 8, 8, d), F32), pltpu.VMEM((2, tile // 8, 8, d), F32),
                        pltpu.SemaphoreType.DMA((2,)), pltpu.SemaphoreType.DMA((2,))],
    )
    return pl.pallas_call(
        functools.partial(_moe_kernel, alpha=alpha),
        out_shape=jax.ShapeDtypeStruct((n, d), F32),
        grid_spec=grid_spec,
        compiler_params=_params(("arbitrary",)),
        name="grouped_moe",
    )(src, tstart, tcnt, e_lo, e_hi, n_used,
      x, router_wt3, router_wt3, wg, wu, wd, wg, wu, wd, g, b)


def kernel(x_prompt, x_sample, state_hgrn, state_gla, cache_mem_k, cache_mem_v, mem_prompt, hgrn_w_in, hgrn_lb_logits, hgrn_norm_g, hgrn_w_out, gla_w_in, gla_w_gk2, gla_b_gk2, gla_norm_g, gla_w_out, xattn_w_q, xattn_w_kv, xattn_w_o, router_w, router_bias, moe_w_gate, moe_w_up, moe_w_down, ln_g, ln_b):
    batch, seq, d = x_prompt.shape
    dec_batch, dec_seq, _ = x_sample.shape
    depth = ln_g.shape[0]
    alpha = (2 * depth) ** 0.25
    a_heads, a_key, a_val = state_hgrn.shape[2:]
    b_heads, b_key, b_val = state_gla.shape[2:]
    mem_len, x_heads = cache_mem_k.shape[2], cache_mem_k.shape[3]
    rank = gla_w_gk2.shape[1]
    b_main = 2 * b_heads * b_key + b_heads * b_val + d

    hgrn_w_in_b = hgrn_w_in.astype(BF16)
    hgrn_w_out_b = hgrn_w_out.astype(BF16)
    gla_w_in_b = jnp.concatenate(
        [gla_w_in, jnp.zeros(gla_w_in.shape[:2] + (GLA_RANK_PAD - rank,), F32)], axis=-1).astype(BF16)
    gla_w_gk2_p = jnp.concatenate(
        [gla_w_gk2, jnp.zeros((gla_w_gk2.shape[0], GLA_RANK_PAD - rank, gla_w_gk2.shape[2]), F32)], axis=1)
    gla_w_out_b = gla_w_out.astype(BF16)
    w_q_b = xattn_w_q.astype(BF16)
    w_k_b = xattn_w_kv[:, :, :d].astype(BF16)
    w_v_b = xattn_w_kv[:, :, d:].astype(BF16)
    w_o_b = xattn_w_o.astype(BF16)
    wg_b = moe_w_gate.astype(BF16)
    wu_b = moe_w_up.astype(BF16)
    wd_b = moe_w_down.astype(BF16)
    router_wt = router_w.T
    router_wt3 = router_wt[:, None, :]
    bias_col = router_bias[:, None]
    assert b_main + rank == gla_w_in.shape[2]

    mem_k_p, mem_v_p = _kv_proj(mem_prompt.reshape(batch * mem_len, d), w_k_b, w_v_b, 512)
    mem_k_prompt = mem_k_p.reshape(depth, batch, mem_len, x_heads, d // x_heads)
    mem_v_prompt = mem_v_p.reshape(depth, batch, mem_len, x_heads, d // x_heads)

    def run_trunk(x3, states_a, states_b, mem_ks, mem_vs):
        nb, ns, _ = x3.shape
        x = x3.reshape(nb * ns, d)
        new_a, new_b = [], []
        for l in range(depth):
            j = l // 2
            row = lambda a, i: a[l, i][None, :]
            if l % 2 == 0:
                w_in, w_out, states, new = hgrn_w_in_b[j], hgrn_w_out_b[j], states_a, new_a
                aux = (hgrn_lb_logits, hgrn_lb_logits[:1], hgrn_norm_g[j][None, :])
                cfg = dict(mode="hgrn", batch=nb, seq=ns, heads=a_heads, kd=a_key, vd=a_val, lb_row=l)
            else:
                w_in, w_out, states, new = gla_w_in_b[j], gla_w_out_b[j], states_b, new_b
                aux = (gla_w_gk2_p[j], gla_b_gk2[j][None, :], gla_norm_g[j][None, :])
                cfg = dict(mode="gla", batch=nb, seq=ns, heads=b_heads, kd=b_key, vd=b_val)
            if states is None:
                x, s = _mixer(x, w_in, *aux, w_out, row(ln_g, 0), row(ln_b, 0), alpha, **cfg)
            else:
                proj = _linear(x, w_in, F32, 512)
                o, s = _recurrence(proj, states[j], *aux, **cfg)
                x = _linear_res_ln(o, w_out, x, row(ln_g, 0), row(ln_b, 0), alpha, 512)
            new.append(s)
            q = _linear(x, w_q_b[l], BF16 if ns >= 256 else F32, 512)
            c = _attention(q, mem_ks, mem_vs, l, batch=nb, seq=ns, heads=x_heads)
            x = _linear_res_ln(c, w_o_b[l], x, row(ln_g, 1), row(ln_b, 1), alpha, 512)
            cls = _router(x, router_wt, bias_col, 512)
            x = _moe(x, cls[0], router_wt3, wg_b, wu_b, wd_b, l, row(ln_g, 2), row(ln_b, 2), alpha)
        return x.reshape(nb, ns, d), jnp.stack(new_a), jnp.stack(new_b)

    y_prompt, state_hgrn_prompt, state_gla_prompt = run_trunk(
        x_prompt, None, None, mem_k_p.reshape(depth, batch, mem_len, d), mem_v_p.reshape(depth, batch, mem_len, d))
    y_sample, state_hgrn_sample, state_gla_sample = run_trunk(
        x_sample, state_hgrn, state_gla, _interleaved_rows(cache_mem_k), _interleaved_rows(cache_mem_v))
    return (y_prompt, y_sample, state_hgrn_prompt, state_gla_prompt, mem_k_prompt, mem_v_prompt,
            state_hgrn_sample, state_gla_sample)
```

```python
import functools

import jax
import jax.numpy as jnp
from jax import lax
from jax.experimental import pallas as pl
from jax.experimental.pallas import tpu as pltpu

F32 = jnp.float32
BF16 = jnp.bfloat16
I32 = jnp.int32

N_GROUPS = 4
GROUP_SIZE = 4
GLA_GATE_NORMALIZER = 16.0
LN_EPS = 1e-5
RMS_EPS = 1e-6
GLA_RANK_PAD = 128

VMEM_LIMIT_BYTES = 56 * 1024 * 1024
REC_CHUNK = 64
MOE_TILE = 256

_HI = lax.Precision.HIGHEST


def _params(sem):
    return pltpu.CompilerParams(dimension_semantics=sem, vmem_limit_bytes=VMEM_LIMIT_BYTES)


def _dot(a, b):
    return jnp.dot(a, b, preferred_element_type=F32)


def _dot_nt(a, b):
    return lax.dot_general(a, b, (((1,), (1,)), ((), ())), preferred_element_type=F32)


def _dot_tn(a, b):
    return lax.dot_general(a, b, (((0,), (0,)), ((), ())), preferred_element_type=F32)


def _layer_norm(z, g, b):
    mu = jnp.mean(z, axis=-1, keepdims=True)
    zc = z - mu
    var = jnp.mean(zc * zc, axis=-1, keepdims=True)
    return zc * lax.rsqrt(var + LN_EPS) * g + b


def _sigmoid(x):
    return 0.5 * jnp.tanh(0.5 * x) + 0.5


def _log_sigmoid(x):
    return jnp.minimum(x, 0.0) - jnp.log(1.0 + jnp.exp(-jnp.abs(x)))


def _linear_kernel(x_ref, w_ref, o_ref):
    o_ref[...] = _dot(x_ref[...].astype(BF16), w_ref[...]).astype(o_ref.dtype)


def _linear(x, w, out_dtype, tm):
    m, k = x.shape
    n = w.shape[1]
    return pl.pallas_call(
        _linear_kernel,
        out_shape=jax.ShapeDtypeStruct((m, n), out_dtype),
        grid=(m // tm,),
        in_specs=[pl.BlockSpec((tm, k), lambda i: (i, 0)),
                  pl.BlockSpec((k, n), lambda i: (0, 0))],
        out_specs=pl.BlockSpec((tm, n), lambda i: (i, 0)),
        compiler_params=_params(("arbitrary",)),
        name="linear",
    )(x, w)


def _kv_proj_kernel(x_ref, wk_ref, wv_ref, k_ref, v_ref):
    x = x_ref[...].astype(BF16)
    k_ref[...] = _dot(x, wk_ref[...])
    v_ref[...] = _dot(x, wv_ref[...])


def _kv_proj(mem, wk, wv, tm):
    r, d = mem.shape
    nl = wk.shape[0]
    out = jax.ShapeDtypeStruct((nl, r, d), F32)
    wspec = pl.BlockSpec((None, d, d), lambda l, i: (l, 0, 0))
    ospec = pl.BlockSpec((None, tm, d), lambda l, i: (l, i, 0))
    return pl.pallas_call(
        _kv_proj_kernel,
        out_shape=(out, out),
        grid=(nl, r // tm),
        in_specs=[pl.BlockSpec((tm, d), lambda l, i: (i, 0)), wspec, wspec],
        out_specs=(ospec, ospec),
        compiler_params=_params(("arbitrary", "arbitrary")),
        name="kv_proj",
    )(mem, wk, wv)


def _linear_res_ln_kernel(h_ref, w_ref, x_ref, g_ref, b_ref, o_ref, *, alpha):
    c = _dot(h_ref[...].astype(BF16), w_ref[...])
    o_ref[...] = _layer_norm(alpha * x_ref[...] + c, g_ref[...], b_ref[...])


def _linear_res_ln(h, w, x, g, b, alpha, tm):
    m, k = h.shape
    d = w.shape[1]
    row = pl.BlockSpec((1, d), lambda i: (0, 0))
    return pl.pallas_call(
        functools.partial(_linear_res_ln_kernel, alpha=alpha),
        out_shape=jax.ShapeDtypeStruct((m, d), F32),
        grid=(m // tm,),
        in_specs=[pl.BlockSpec((tm, k), lambda i: (i, 0)),
                  pl.BlockSpec((k, d), lambda i: (0, 0)),
                  pl.BlockSpec((tm, d), lambda i: (i, 0)), row, row],
        out_specs=pl.BlockSpec((tm, d), lambda i: (i, 0)),
        compiler_params=_params(("arbitrary",)),
        name="linear_res_ln",
    )(h, w, x, g, b)


def _cumsum_rows(x):
    n = x.shape[0]
    row = lax.broadcasted_iota(I32, x.shape, 0)
    s = 1
    while s < n:
        x = x + jnp.where(row >= s, pltpu.roll(x, s, 0), 0.0)
        s *= 2
    return x


def _chunk_prepare(q, k, v, g):
    c, kd = q.shape
    vd = v.shape[1]
    b = _cumsum_rows(g)
    b_last = b[c - 1:c, :]
    b_mid = b[c // 2:c // 2 + 1, :]
    qa = q * jnp.exp(b - b_mid)
    ka = k * jnp.exp(b_mid - b)
    scores = _dot_nt(qa.astype(BF16), ka.astype(BF16))
    ri = lax.broadcasted_iota(I32, (c, c), 0)
    ci = lax.broadcasted_iota(I32, (c, c), 1)
    scores = jnp.where(ri >= ci, scores, 0.0).astype(BF16)
    decay = jnp.broadcast_to(jnp.exp(b_last), (kd, kd)).T
    if vd != kd:
        decay = jnp.concatenate([decay] * (vd // kd), axis=1)
    return ((qa * jnp.exp(b_mid)).astype(BF16), scores,
            (ka * jnp.exp(b_last - b_mid)).astype(BF16), v.astype(BF16), decay)


def _chunk_local(prep):
    _, scores, ks, vb, _ = prep
    return _dot(scores, vb), _dot_tn(ks, vb)


def _chunk_finish(prep, local, state):
    q_in, _, _, _, decay = prep
    o_local, increment = local
    return _dot(q_in, state.astype(BF16)) + o_local, state * decay + increment


def _gated_rmsnorm(o, gate, gain):
    o = o * lax.rsqrt(jnp.mean(o * o, axis=-1, keepdims=True) + RMS_EPS)
    return o * gain * (gate * _sigmoid(gate))


def _lower_bound(logits, lb_row):
    e = jnp.exp(logits - jnp.max(logits, axis=0, keepdims=True))
    return jnp.sum(e[:lb_row + 1], axis=0, keepdims=True) / jnp.sum(e, axis=0, keepdims=True)


def _gla_gates(lr, w_gk2, b_gk2):
    z = jnp.dot(lr, w_gk2, precision=_HI, preferred_element_type=F32) + b_gk2
    return _log_sigmoid(z) * (1.0 / GLA_GATE_NORMALIZER)


def _head_inputs(mode, proj_ref, rows, h, heads, kd, vd, lb, gk_all):
    hk = heads * kd
    hv = heads * vd
    v = proj_ref[rows, 2 * hk + h * vd:2 * hk + (h + 1) * vd]
    gate = proj_ref[rows, 2 * hk + hv + h * vd:2 * hk + hv + (h + 1) * vd]
    if mode == "hgrn":
        q = proj_ref[rows, h * kd:(h + 1) * kd]
        f = proj_ref[rows, hk + h * kd:hk + (h + 1) * kd]
        lbh = lb[:, h * kd:(h + 1) * kd]
        q = q * _sigmoid(q) * (kd ** -0.5)
        fg = lbh + (1.0 - lbh) * _sigmoid(f)
        return q, 1.0 - fg, v, jnp.log(fg), gate
    q = proj_ref[rows, h * kd:(h + 1) * kd] * (kd ** -0.5)
    k = proj_ref[rows, hk + h * kd:hk + (h + 1) * kd]
    return q, k, v, gk_all[:, h * kd:(h + 1) * kd], gate


def _rec_kernel(proj_ref, s0_ref, aux0_ref, aux1_ref, gain_ref, o_ref, s_ref, *,
                mode, heads, kd, vd, chunk, tb, nbatch, lb_row):
    @pl.when(pl.program_id(1) == 0)
    def _():
        s_ref[...] = s0_ref[...]

    hk = heads * kd
    hv = heads * vd
    gain = gain_ref[...]
    lb = _lower_bound(aux0_ref[...], lb_row) if mode == "hgrn" else None

    def one_batch(nb):
        for sc in range(tb // chunk):
            r0 = nb * tb + sc * chunk
            rows = pl.ds(r0, chunk) if isinstance(r0, int) else pl.ds(pl.multiple_of(r0, 8), chunk)
            gk_all = None
            if mode == "gla":
                lr = proj_ref[rows, 2 * hk + 2 * hv:2 * hk + 2 * hv + GLA_RANK_PAD]
                gk_all = _gla_gates(lr, aux0_ref[...], aux1_ref[...])
            inputs = [_head_inputs(mode, proj_ref, rows, h, heads, kd, vd, lb, gk_all) for h in range(heads)]
            preps = [_chunk_prepare(q, k, v, g) for q, k, v, g, _ in inputs]
            local = [_chunk_local(p) for p in preps]
            for h in range(heads):
                o, s_new = _chunk_finish(preps[h], local[h], s_ref[nb, h])
                s_ref[nb, h] = s_new
                o_ref[rows, h * vd:(h + 1) * vd] = _gated_rmsnorm(o, inputs[h][4], gain)

    if nbatch == 1:
        one_batch(0)
    else:
        def body(nb, carry):
            one_batch(nb)
            return carry
        lax.fori_loop(0, nbatch, body, 0)


def _recurrence(proj, s0, aux0, aux1, gain, *, mode, batch, seq, heads, kd, vd, lb_row=0):
    n, width = proj.shape
    chunk = min(REC_CHUNK, seq)
    tb = min(2 * chunk, seq)
    nbatch = 1 if seq > tb else min(8, batch)
    nblk = seq // tb
    grid = (batch // nbatch, nblk)
    state_spec = pl.BlockSpec((nbatch, heads, kd, vd), lambda b, c: (b, 0, 0, 0))
    full2 = lambda a: pl.BlockSpec(a.shape, lambda b, c: (0, 0))
    kern = functools.partial(_rec_kernel, mode=mode, heads=heads, kd=kd, vd=vd, chunk=chunk,
                             tb=tb, nbatch=nbatch, lb_row=lb_row)
    return pl.pallas_call(
        kern,
        out_shape=(jax.ShapeDtypeStruct((n, heads * vd), F32),
                   jax.ShapeDtypeStruct((batch, heads, kd, vd), F32)),
        grid=grid,
        in_specs=[pl.BlockSpec((nbatch * tb, width), lambda b, c: (b * nblk + c, 0)), state_spec,
                  full2(aux0), full2(aux1), full2(gain)],
        out_specs=(pl.BlockSpec((nbatch * tb, heads * vd), lambda b, c: (b * nblk + c, 0)),
                   state_spec),
        compiler_params=_params(("arbitrary", "arbitrary")),
        name="recurrence_" + mode,
    )(proj, s0, aux0, aux1, gain)


def _mixer_kernel(x_ref, w_in_ref, aux0_ref, aux1_ref, gain_ref, w_out_ref, g_ref, b_ref,
                  y_ref, s_ref, proj_scr, o_scr, *, mode, heads, kd, vd, chunk, lb_row, alpha):
    @pl.when(pl.program_id(1) == 0)
    def _():
        s_ref[...] = jnp.zeros(s_ref.shape, F32)

    tb = x_ref.shape[0]
    hk = heads * kd
    hv = heads * vd
    gain = gain_ref[...]
    x = x_ref[...]
    xb = x.astype(BF16)
    group = 256 // kd
    seg_starts = (0, hk, 2 * hk, 2 * hk + hv)
    seg_widths = (kd, kd, vd, vd)

    def project(gi):
        for start, w in zip(seg_starts, seg_widths):
            cols = slice(start + gi * group * w, start + (gi + 1) * group * w)
            proj_scr[:, cols] = _dot(xb, w_in_ref[:, cols])

    lb = None
    gk_blocks = None
    if mode == "hgrn":
        lb = _lower_bound(aux0_ref[...], lb_row)
    else:
        lr = _dot(xb, w_in_ref[:, 2 * hk + 2 * hv:2 * hk + 2 * hv + GLA_RANK_PAD])
        gk_full = _gla_gates(lr, aux0_ref[...], aux1_ref[...])
        gk_blocks = [gk_full[sc * chunk:(sc + 1) * chunk, :] for sc in range(tb // chunk)]

    n_groups = heads // group
    project(0)
    for gi in range(n_groups):
        if gi + 1 < n_groups:
            project(gi + 1)
        units = [(h, sc) for h in range(gi * group, (gi + 1) * group) for sc in range(tb // chunk)]
        inputs = [_head_inputs(mode, proj_scr, pl.ds(sc * chunk, chunk), h, heads, kd, vd, lb,
                               None if gk_blocks is None else gk_blocks[sc]) for h, sc in units]
        preps = [_chunk_prepare(q, k, v, g) for q, k, v, g, _ in inputs]
        local = [_chunk_local(p) for p in preps]
        for u, (h, sc) in enumerate(units):
            o, s_new = _chunk_finish(preps[u], local[u], s_ref[0, h])
            s_ref[0, h] = s_new
            o_scr[pl.ds(sc * chunk, chunk), h * vd:(h + 1) * vd] = (
                _gated_rmsnorm(o, inputs[u][4], gain).astype(BF16))
    c = _dot(o_scr[...], w_out_ref[...])
    y_ref[...] = _layer_norm(alpha * x + c, g_ref[...], b_ref[...])


def _mixer(x, w_in, aux0, aux1, gain, w_out, g, b, alpha, *, mode, batch, seq, heads, kd, vd, lb_row=0):
    n, d = x.shape
    width = w_in.shape[1]
    tb = 4 * REC_CHUNK
    nblk = seq // tb
    const = lambda a: pl.BlockSpec(a.shape, lambda bb, c: (0, 0))
    rows = pl.BlockSpec((tb, d), lambda bb, c: (bb * nblk + c, 0))
    kern = functools.partial(_mixer_kernel, mode=mode, heads=heads, kd=kd, vd=vd, chunk=REC_CHUNK,
                             lb_row=lb_row, alpha=alpha)
    return pl.pallas_call(
        kern,
        out_shape=(jax.ShapeDtypeStruct((n, d), F32),
                   jax.ShapeDtypeStruct((batch, heads, kd, vd), F32)),
        grid=(batch, nblk),
        in_specs=[rows, const(w_in), const(aux0), const(aux1), const(gain), const(w_out), const(g), const(b)],
        out_specs=(rows, pl.BlockSpec((1, heads, kd, vd), lambda bb, c: (bb, 0, 0, 0))),
        scratch_shapes=[pltpu.VMEM((tb, width), F32), pltpu.VMEM((tb, heads * vd), BF16)],
        compiler_params=_params(("arbitrary", "arbitrary")),
        name="mixer_" + mode,
    )(x, w_in, aux0, aux1, gain, w_out, g, b)


def _attn_kernel(q_ref, k_ref, v_ref, o_ref, *, heads, nbatch, tq, per_head):
    hd = q_ref.shape[1] // heads
    scale = hd ** -0.5
    units = [(nb, h) for nb in range(nbatch) for h in range(heads)]

    def head_rows(ref, nb, h):
        if not per_head:
            return ref[nb, :, h * hd:(h + 1) * hd].astype(BF16)
        nt = hd // 128
        period = heads * nt
        mlen = ref.shape[1] // period
        parts = [ref[nb, pl.ds(t * heads + h, mlen, stride=period), :] for t in range(nt)]
        return jnp.concatenate(parts, axis=1).astype(BF16)

    scores = [_dot_nt(q_ref[nb * tq:(nb + 1) * tq, h * hd:(h + 1) * hd].astype(BF16),
                      head_rows(k_ref, nb, h)) * scale for nb, h in units]
    probs = []
    for s in scores:
        p = jnp.exp(s - jnp.max(s, axis=-1, keepdims=True))
        probs.append((p / jnp.sum(p, axis=-1, keepdims=True)).astype(BF16))
    for (nb, h), p in zip(units, probs):
        o_ref[nb * tq:(nb + 1) * tq, h * hd:(h + 1) * hd] = _dot(p, head_rows(v_ref, nb, h)).astype(o_ref.dtype)


def _xattn_kernel(x_ref, wq_ref, k_ref, v_ref, wo_ref, g_ref, b_ref, y_ref, *, heads, alpha):
    x = x_ref[...]
    hd = x.shape[1] // heads
    scale = hd ** -0.5
    q = _dot(x.astype(BF16), wq_ref[...]).astype(BF16)
    kb = k_ref[...].astype(BF16)
    vb = v_ref[...].astype(BF16)
    cols = [slice(h * hd, (h + 1) * hd) for h in range(heads)]
    scores = [_dot_nt(q[:, c], kb[:, c]) * scale for c in cols]
    probs = []
    for s in scores:
        p = jnp.exp(s - jnp.max(s, axis=-1, keepdims=True))
        probs.append((p / jnp.sum(p, axis=-1, keepdims=True)).astype(BF16))
    o = jnp.concatenate([_dot(p, vb[:, c]).astype(BF16) for p, c in zip(probs, cols)], axis=1)
    y_ref[...] = _layer_norm(alpha * x + _dot(o, wo_ref[...]), g_ref[...], b_ref[...])


def _xattn(x, wq, mem_k, mem_v, wo, layer, g, b, alpha, *, batch, seq, heads, tq):
    n, d = x.shape
    mlen = mem_k.shape[2]
    nblk = seq // tq
    const = lambda a: pl.BlockSpec(a.shape, lambda bb, i: (0, 0))
    rows = pl.BlockSpec((tq, d), lambda bb, i: (bb * nblk + i, 0))
    mem = pl.BlockSpec((None, None, mlen, d), lambda bb, i: (layer, bb, 0, 0))
    return pl.pallas_call(
        functools.partial(_xattn_kernel, heads=heads, alpha=alpha),
        out_shape=jax.ShapeDtypeStruct((n, d), F32),
        grid=(batch, nblk),
        in_specs=[rows, const(wq), mem, mem, const(wo), const(g), const(b)],
        out_specs=rows,
        compiler_params=_params(("arbitrary", "arbitrary")),
        name="xattn_block",
    )(x, wq, mem_k, mem_v, wo, g, b)


def _interleaved_rows(mem):
    nl, b, m, heads, hd = mem.shape
    nt = hd // 128
    return mem.reshape(nl, b, m, heads, nt, 128).transpose(0, 1, 2, 4, 3, 5).reshape(nl, b, m * nt * heads, 128)


def _attention(q, mem_k, mem_v, layer, *, batch, seq, heads):
    n, d = q.shape
    per_head = mem_k.shape[3] != d
    tq, nbatch = seq, 2
    nblk = 1
    mem_spec = pl.BlockSpec((None, nbatch) + mem_k.shape[2:], lambda b, i: (layer, b, 0, 0))
    qspec = pl.BlockSpec((nbatch * tq, d), lambda b, i: (b * nblk + i, 0))
    return pl.pallas_call(
        functools.partial(_attn_kernel, heads=heads, nbatch=nbatch, tq=tq, per_head=per_head),
        out_shape=jax.ShapeDtypeStruct((n, d), q.dtype),
        grid=(batch // nbatch, nblk),
        in_specs=[qspec, mem_spec, mem_spec],
        out_specs=qspec,
        compiler_params=_params(("arbitrary", "arbitrary")),
        name="mem_attention",
    )(q, mem_k, mem_v)


def _router_kernel(x_ref, wt_ref, bias_ref, cls_ref):
    logits = lax.dot_general(wt_ref[...], x_ref[...], (((1,), (1,)), ((), ())),
                             precision=_HI, preferred_element_type=F32)
    sel = jax.nn.sigmoid(logits) + bias_ref[...]
    rows = [sel[e:e + 1, :] for e in range(N_GROUPS * GROUP_SIZE)]

    def first_argmax(vals):
        best_v, best_i = vals[0], jnp.zeros(vals[0].shape, I32)
        for i in range(1, len(vals)):
            better = vals[i] > best_v
            best_i = jnp.where(better, i, best_i)
            best_v = jnp.where(better, vals[i], best_v)
        return best_i

    group_scores = []
    for gi in range(N_GROUPS):
        a = rows[gi * GROUP_SIZE:(gi + 1) * GROUP_SIZE]
        top2 = None
        for i in range(GROUP_SIZE):
            for j in range(i + 1, GROUP_SIZE):
                s = a[i] + a[j]
                top2 = s if top2 is None else jnp.maximum(top2, s)
        group_scores.append(top2)
    best = first_argmax(group_scores)
    cand = []
    for j in range(GROUP_SIZE):
        cj = rows[j]
        for gi in range(1, N_GROUPS):
            cj = jnp.where(best == gi, rows[gi * GROUP_SIZE + j], cj)
        cand.append(cj)
    i1 = first_argmax(cand)
    i2 = first_argmax([jnp.where(i1 == j, -jnp.inf, cand[j]) for j in range(GROUP_SIZE)])
    lo = jnp.minimum(i1, i2)
    hi = jnp.maximum(i1, i2)
    pair = jnp.where(lo == 0, hi - 1, jnp.where(lo == 1, hi + 1, 5))
    cls_ref[...] = best * 6 + pair


def _router(x, router_wt, bias_col, tm):
    n, d = x.shape
    ne = router_wt.shape[0]
    return pl.pallas_call(
        _router_kernel,
        out_shape=jax.ShapeDtypeStruct((1, n), I32),
        grid=(n // tm,),
        in_specs=[pl.BlockSpec((tm, d), lambda i: (i, 0)),
                  pl.BlockSpec((ne, d), lambda i: (0, 0)),
                  pl.BlockSpec((ne, 1), lambda i: (0, 0))],
        out_specs=pl.BlockSpec((1, tm), lambda i: (0, i)),
        compiler_params=_params(("arbitrary",)),
        name="router",
    )(x, router_wt, bias_col)


_PAIR_LO = (0, 0, 0, 1, 1, 2)
_PAIR_HI = (1, 2, 3, 2, 3, 3)


def _moe_plan(cls, tile):
    n = cls.shape[0]
    ncls = N_GROUPS * 6
    max_tiles = n // tile + ncls
    shift = max(n - 1, 1).bit_length()
    keys = jnp.sort(cls * (1 << shift) + jnp.arange(n, dtype=I32))
    src = keys & ((1 << shift) - 1)
    cid = jnp.arange(ncls, dtype=I32)
    count = jnp.sum((cls[None, :] == cid[:, None]).astype(I32), axis=1)
    cstart = jnp.cumsum(count) - count
    ntile = (count + tile - 1) // tile
    tend = jnp.cumsum(ntile)
    tid = jnp.arange(max_tiles, dtype=I32)
    n_used = tend[-1]
    tcls = jnp.sum((tid[:, None] >= tend[None, :]).astype(I32), axis=1)
    last_cls = jnp.sum((n_used - 1 >= tend).astype(I32))
    tcls = jnp.where(tid < n_used, tcls, last_cls)
    within = tid - (tend - ntile)[tcls]
    tstart = cstart[tcls] + within * tile
    tcnt = jnp.clip(count[tcls] - within * tile, 0, tile)
    tcnt = jnp.where(tid < n_used, tcnt, 0)
    grp = tcls // 6
    e_lo = grp * GROUP_SIZE + jnp.asarray(_PAIR_LO, I32)[tcls % 6]
    e_hi = grp * GROUP_SIZE + jnp.asarray(_PAIR_HI, I32)[tcls % 6]
    return src, tstart, tcnt, e_lo, e_hi, n_used.reshape(1)


def _moe_kernel(src_ref, tstart_ref, tcnt_ref, elo_ref, ehi_ref, nused_ref,
                x_hbm, rw_lo_ref, rw_hi_ref, wg_lo_ref, wu_lo_ref, wd_lo_ref,
                wg_hi_ref, wu_hi_ref, wd_hi_ref, g_ref, b_ref,
                out_hbm, xbuf, ybuf, gsem, ssem, *, alpha):
    j = pl.program_id(0)
    n_used = nused_ref[0]
    slot = lax.rem(j, 2)
    ngroups, _, d = xbuf.shape[1:]

    def for_rows(cnt, fn):
        ngrp = lax.shift_right_logical(cnt, 3)

        def group(gi, c):
            base = gi * 8
            for u in range(8):
                fn(gi, u, base + u, u % 2)
            return c

        def single(r, c):
            fn(lax.shift_right_logical(r, 3), jnp.bitwise_and(r, 7), r, 0)
            return c

        lax.fori_loop(0, ngrp, group, 0)
        lax.fori_loop(ngrp * 8, cnt, single, 0)

    def wait_rows(cnt, group_copy, row_copy):
        ngrp = lax.shift_right_logical(cnt, 3)

        @pl.when(ngrp > 0)
        def _():
            group_copy(ngrp).wait()

        def single(r, c):
            row_copy(lax.shift_right_logical(r, 3), jnp.bitwise_and(r, 7)).wait()
            return c
        lax.fori_loop(ngrp * 8, cnt, single, 0)

    def start_gather(t, s):
        start = tstart_ref[t]

        def fn(gi, u, r, prio):
            tok = src_ref[start + r]
            pltpu.make_async_copy(x_hbm.at[pl.ds(tok, 1)], xbuf.at[s, gi, pl.ds(u, 1)],
                                  gsem.at[s]).start(priority=prio)
        for_rows(tcnt_ref[t], fn)

    def wait_gather(t, s):
        wait_rows(tcnt_ref[t],
                  lambda n: pltpu.make_async_copy(xbuf.at[s, pl.ds(0, n)], xbuf.at[s, pl.ds(0, n)], gsem.at[s]),
                  lambda gi, u: pltpu.make_async_copy(xbuf.at[s, gi, pl.ds(u, 1)], xbuf.at[s, gi, pl.ds(u, 1)],
                                                      gsem.at[s]))

    def start_scatter(t, s):
        start = tstart_ref[t]

        def fn(gi, u, r, prio):
            tok = src_ref[start + r]
            pltpu.make_async_copy(ybuf.at[s, gi, pl.ds(u, 1)], out_hbm.at[pl.ds(tok, 1)],
                                  ssem.at[s]).start(priority=prio)
        for_rows(tcnt_ref[t], fn)

    def wait_scatter(t, s):
        wait_rows(tcnt_ref[t],
                  lambda n: pltpu.make_async_copy(ybuf.at[s, pl.ds(0, n)], ybuf.at[s, pl.ds(0, n)], ssem.at[s]),
                  lambda gi, u: pltpu.make_async_copy(ybuf.at[s, gi, pl.ds(u, 1)], ybuf.at[s, gi, pl.ds(u, 1)],
                                                      ssem.at[s]))

    @pl.when(j == 0)
    def _():
        xbuf[...] = jnp.zeros(xbuf.shape, F32)
        start_gather(0, 0)

    @pl.when(j < n_used)
    def _():
        @pl.when(j + 1 < n_used)
        def _():
            start_gather(j + 1, 1 - slot)

        wait_gather(j, slot)

        @pl.when(j >= 2)
        def _():
            wait_scatter(j - 2, slot)

        x = xbuf[slot].reshape(ngroups * 8, d)
        xb = x.astype(BF16)

        def expert(rw_ref, wg_ref, wu_ref, wd_ref):
            h = _dot(xb, wg_ref[...])
            h = h * _sigmoid(h) * _dot(xb, wu_ref[...])
            y = _dot(h.astype(BF16), wd_ref[...])
            score = _sigmoid(jnp.sum(x * rw_ref[...], axis=-1, keepdims=True))
            return y, score

        y_lo, s_lo = expert(rw_lo_ref, wg_lo_ref, wu_lo_ref, wd_lo_ref)
        y_hi, s_hi = expert(rw_hi_ref, wg_hi_ref, wu_hi_ref, wd_hi_ref)
        tot = s_lo + s_hi
        m = (s_lo / tot) * y_lo + (s_hi / tot) * y_hi
        ybuf[slot] = _layer_norm(alpha * x + m, g_ref[...], b_ref[...]).reshape(ngroups, 8, d)
        start_scatter(j, slot)

        @pl.when(j == n_used - 1)
        def _():
            @pl.when(j >= 1)
            def _():
                wait_scatter(j - 1, 1 - slot)
            wait_scatter(j, slot)


def _moe(x, cls, router_wt3, wg, wu, wd, layer, g, b, alpha):
    n, d = x.shape
    de = wg.shape[3]
    tile = MOE_TILE
    src, tstart, tcnt, e_lo, e_hi, n_used = _moe_plan(cls, tile)
    max_tiles = tstart.shape[0]

    def by_lo(shape):
        return pl.BlockSpec(shape, lambda j, s, ts, tc, el, eh, nu: (el[j], 0, 0))

    def by_hi(shape):
        return pl.BlockSpec(shape, lambda j, s, ts, tc, el, eh, nu: (eh[j], 0, 0))

    def w_lo(r, c):
        return pl.BlockSpec((None, None, r, c), lambda j, s, ts, tc, el, eh, nu: (layer, el[j], 0, 0))

    def w_hi(r, c):
        return pl.BlockSpec((None, None, r, c), lambda j, s, ts, tc, el, eh, nu: (layer, eh[j], 0, 0))

    row = pl.BlockSpec((1, d), lambda j, s, ts, tc, el, eh, nu: (0, 0))
    grid_spec = pltpu.PrefetchScalarGridSpec(
        num_scalar_prefetch=6,
        grid=(max_tiles,),
        in_specs=[pl.BlockSpec(memory_space=pl.ANY),
                  by_lo((None, 1, d)), by_hi((None, 1, d)),
                  w_lo(d, de), w_lo(d, de), w_lo(de, d),
                  w_hi(d, de), w_hi(d, de), w_hi(de, d),
                  row, row],
        out_specs=pl.BlockSpec(memory_space=pl.ANY),
        scratch_shapes=[pltpu.VMEM((2, tile // 8, 8, d), F32), pltpu.VMEM((2, tile // 8, 8, d), F32),
                        pltpu.SemaphoreType.DMA((2,)), pltpu.SemaphoreType.DMA((2,))],
    )
    return pl.pallas_call(
        functools.partial(_moe_kernel, alpha=alpha),
        out_shape=jax.ShapeDtypeStruct((n, d), F32),
        grid_spec=grid_spec,
        compiler_params=_params(("arbitrary",)),
        name="grouped_moe",
    )(src, tstart, tcnt, e_lo, e_hi, n_used,
      x, router_wt3, router_wt3, wg, wu, wd, wg, wu, wd, g, b)


def kernel(x_prompt, x_sample, state_hgrn, state_gla, cache_mem_k, cache_mem_v, mem_prompt, hgrn_w_in, hgrn_lb_logits, hgrn_norm_g, hgrn_w_out, gla_w_in, gla_w_gk2, gla_b_gk2, gla_norm_g, gla_w_out, xattn_w_q, xattn_w_kv, xattn_w_o, router_w, router_bias, moe_w_gate, moe_w_up, moe_w_down, ln_g, ln_b):
    batch, seq, d = x_prompt.shape
    dec_batch, dec_seq, _ = x_sample.shape
    depth = ln_g.shape[0]
    alpha = (2 * depth) ** 0.25
    a_heads, a_key, a_val = state_hgrn.shape[2:]
    b_heads, b_key, b_val = state_gla.shape[2:]
    mem_len, x_heads = cache_mem_k.shape[2], cache_mem_k.shape[3]
    rank = gla_w_gk2.shape[1]
    b_main = 2 * b_heads * b_key + b_heads * b_val + d

    hgrn_w_in_b = hgrn_w_in.astype(BF16)
    hgrn_w_out_b = hgrn_w_out.astype(BF16)
    gla_w_in_b = jnp.concatenate(
        [gla_w_in, jnp.zeros(gla_w_in.shape[:2] + (GLA_RANK_PAD - rank,), F32)], axis=-1).astype(BF16)
    gla_w_gk2_p = jnp.concatenate(
        [gla_w_gk2, jnp.zeros((gla_w_gk2.shape[0], GLA_RANK_PAD - rank, gla_w_gk2.shape[2]), F32)], axis=1)
    gla_w_out_b = gla_w_out.astype(BF16)
    w_q_b = xattn_w_q.astype(BF16)
    w_k_b = xattn_w_kv[:, :, :d].astype(BF16)
    w_v_b = xattn_w_kv[:, :, d:].astype(BF16)
    w_o_b = xattn_w_o.astype(BF16)
    wg_b = moe_w_gate.astype(BF16)
    wu_b = moe_w_up.astype(BF16)
    wd_b = moe_w_down.astype(BF16)
    router_wt = router_w.T
    router_wt3 = router_wt[:, None, :]
    bias_col = router_bias[:, None]
    assert b_main + rank == gla_w_in.shape[2]

    mem_k_p, mem_v_p = _kv_proj(mem_prompt.reshape(batch * mem_len, d), w_k_b, w_v_b, 512)
    mem_k_prompt = mem_k_p.reshape(depth, batch, mem_len, x_heads, d // x_heads)
    mem_v_prompt = mem_v_p.reshape(depth, batch, mem_len, x_heads, d // x_heads)

    def run_trunk(x3, states_a, states_b, mem_ks, mem_vs):
        nb, ns, _ = x3.shape
        x = x3.reshape(nb * ns, d)
        new_a, new_b = [], []
        for l in range(depth):
            j = l // 2
            row = lambda a, i: a[l, i][None, :]
            if l % 2 == 0:
                w_in, w_out, states, new = hgrn_w_in_b[j], hgrn_w_out_b[j], states_a, new_a
                aux = (hgrn_lb_logits, hgrn_lb_logits[:1], hgrn_norm_g[j][None, :])
                cfg = dict(mode="hgrn", batch=nb, seq=ns, heads=a_heads, kd=a_key, vd=a_val, lb_row=l)
            else:
                w_in, w_out, states, new = gla_w_in_b[j], gla_w_out_b[j], states_b, new_b
                aux = (gla_w_gk2_p[j], gla_b_gk2[j][None, :], gla_norm_g[j][None, :])
                cfg = dict(mode="gla", batch=nb, seq=ns, heads=b_heads, kd=b_key, vd=b_val)
            if states is None:
                x, s = _mixer(x, w_in, *aux, w_out, row(ln_g, 0), row(ln_b, 0), alpha, **cfg)
            else:
                proj = _linear(x, w_in, F32, 512)
                o, s = _recurrence(proj, states[j], *aux, **cfg)
                x = _linear_res_ln(o, w_out, x, row(ln_g, 0), row(ln_b, 0), alpha, 512)
            new.append(s)
            if ns >= 512:
                x = _xattn(x, w_q_b[l], mem_ks, mem_vs, w_o_b[l], l, row(ln_g, 1), row(ln_b, 1), alpha,
                           batch=nb, seq=ns, heads=x_heads, tq=512)
            else:
                q = _linear(x, w_q_b[l], F32, 512)
                c = _attention(q, mem_ks, mem_vs, l, batch=nb, seq=ns, heads=x_heads)
                x = _linear_res_ln(c, w_o_b[l], x, row(ln_g, 1), row(ln_b, 1), alpha, 512)
            cls = _router(x, router_wt, bias_col, 512)
            x = _moe(x, cls[0], router_wt3, wg_b, wu_b, wd_b, l, row(ln_g, 2), row(ln_b, 2), alpha)
        return x.reshape(nb, ns, d), jnp.stack(new_a), jnp.stack(new_b)

    y_prompt, state_hgrn_prompt, state_gla_prompt = run_trunk(
        x_prompt, None, None, mem_k_p.reshape(depth, batch, mem_len, d), mem_v_p.reshape(depth, batch, mem_len, d))
    y_sample, state_hgrn_sample, state_gla_sample = run_trunk(
        x_sample, state_hgrn, state_gla, _interleaved_rows(cache_mem_k), _interleaved_rows(cache_mem_v))
    return (y_prompt, y_sample, state_hgrn_prompt, state_gla_prompt, mem_k_prompt, mem_v_prompt,
            state_hgrn_sample, state_gla_sample)
```

```python
import functools

import jax
import jax.numpy as jnp
from jax import lax
from jax.experimental import pallas as pl
from jax.experimental.pallas import tpu as pltpu

F32 = jnp.float32
BF16 = jnp.bfloat16
I32 = jnp.int32

N_GROUPS = 4
GROUP_SIZE = 4
GLA_GATE_NORMALIZER = 16.0
LN_EPS = 1e-5
RMS_EPS = 1e-6
GLA_RANK_PAD = 128

VMEM_LIMIT_BYTES = 56 * 1024 * 1024
REC_CHUNK = 64
MOE_TILE = 256

_HI = lax.Precision.HIGHEST


def _params(sem):
    return pltpu.CompilerParams(dimension_semantics=sem, vmem_limit_bytes=VMEM_LIMIT_BYTES)


def _dot(a, b):
    return jnp.dot(a, b, preferred_element_type=F32)


def _dot_nt(a, b):
    return lax.dot_general(a, b, (((1,), (1,)), ((), ())), preferred_element_type=F32)


def _dot_tn(a, b):
    return lax.dot_general(a, b, (((0,), (0,)), ((), ())), preferred_element_type=F32)


def _layer_norm(z, g, b):
    mu = jnp.mean(z, axis=-1, keepdims=True)
    zc = z - mu
    var = jnp.mean(zc * zc, axis=-1, keepdims=True)
    return zc * lax.rsqrt(var + LN_EPS) * g + b


def _sigmoid(x):
    return 0.5 * jnp.tanh(0.5 * x) + 0.5


def _log_sigmoid(x):
    return jnp.minimum(x, 0.0) - jnp.log(1.0 + jnp.exp(-jnp.abs(x)))


def _linear_kernel(x_ref, w_ref, o_ref):
    o_ref[...] = _dot(x_ref[...].astype(BF16), w_ref[...]).astype(o_ref.dtype)


def _linear(x, w, out_dtype, tm):
    m, k = x.shape
    n = w.shape[1]
    return pl.pallas_call(
        _linear_kernel,
        out_shape=jax.ShapeDtypeStruct((m, n), out_dtype),
        grid=(m // tm,),
        in_specs=[pl.BlockSpec((tm, k), lambda i: (i, 0)),
                  pl.BlockSpec((k, n), lambda i: (0, 0))],
        out_specs=pl.BlockSpec((tm, n), lambda i: (i, 0)),
        compiler_params=_params(("arbitrary",)),
        name="linear",
    )(x, w)


def _kv_proj_kernel(x_ref, wk_ref, wv_ref, k_ref, v_ref, *, heads):
    x = x_ref[...].astype(BF16)
    tm, d = x.shape
    hd = d // heads
    nt = hd // 128
    for w_ref, o_ref in ((wk_ref, k_ref), (wv_ref, v_ref)):
        y = _dot(x, w_ref[...])
        for h in range(heads):
            for t in range(nt):
                o_ref[pl.ds(t * heads + h, tm, stride=heads * nt), :] = y[:, h * hd + t * 128:h * hd + (t + 1) * 128]


def _kv_proj(mem, wk, wv, tm, heads):
    r, d = mem.shape
    nl = wk.shape[0]
    out = jax.ShapeDtypeStruct((nl, r * d // 128, 128), F32)
    wspec = pl.BlockSpec((None, d, d), lambda l, i: (l, 0, 0))
    ospec = pl.BlockSpec((None, tm * d // 128, 128), lambda l, i: (l, i, 0))
    return pl.pallas_call(
        functools.partial(_kv_proj_kernel, heads=heads),
        out_shape=(out, out),
        grid=(nl, r // tm),
        in_specs=[pl.BlockSpec((tm, d), lambda l, i: (i, 0)), wspec, wspec],
        out_specs=(ospec, ospec),
        compiler_params=_params(("arbitrary", "arbitrary")),
        name="kv_proj",
    )(mem, wk, wv)


def _linear_res_ln_kernel(h_ref, w_ref, x_ref, g_ref, b_ref, o_ref, *, alpha):
    c = _dot(h_ref[...].astype(BF16), w_ref[...])
    o_ref[...] = _layer_norm(alpha * x_ref[...] + c, g_ref[...], b_ref[...])


def _linear_res_ln(h, w, x, g, b, alpha, tm):
    m, k = h.shape
    d = w.shape[1]
    row = pl.BlockSpec((1, d), lambda i: (0, 0))
    return pl.pallas_call(
        functools.partial(_linear_res_ln_kernel, alpha=alpha),
        out_shape=jax.ShapeDtypeStruct((m, d), F32),
        grid=(m // tm,),
        in_specs=[pl.BlockSpec((tm, k), lambda i: (i, 0)),
                  pl.BlockSpec((k, d), lambda i: (0, 0)),
                  pl.BlockSpec((tm, d), lambda i: (i, 0)), row, row],
        out_specs=pl.BlockSpec((tm, d), lambda i: (i, 0)),
        compiler_params=_params(("arbitrary",)),
        name="linear_res_ln",
    )(h, w, x, g, b)


def _cumsum_rows(x):
    n = x.shape[0]
    row = lax.broadcasted_iota(I32, x.shape, 0)
    s = 1
    while s < n:
        x = x + jnp.where(row >= s, pltpu.roll(x, s, 0), 0.0)
        s *= 2
    return x


def _chunk_prepare(q, k, v, g):
    c, kd = q.shape
    vd = v.shape[1]
    b = _cumsum_rows(g)
    b_last = b[c - 1:c, :]
    b_mid = b[c // 2:c // 2 + 1, :]
    qa = q * jnp.exp(b - b_mid)
    ka = k * jnp.exp(b_mid - b)
    scores = _dot_nt(qa.astype(BF16), ka.astype(BF16))
    ri = lax.broadcasted_iota(I32, (c, c), 0)
    ci = lax.broadcasted_iota(I32, (c, c), 1)
    scores = jnp.where(ri >= ci, scores, 0.0).astype(BF16)
    decay = jnp.broadcast_to(jnp.exp(b_last), (kd, kd)).T
    if vd != kd:
        decay = jnp.concatenate([decay] * (vd // kd), axis=1)
    return ((qa * jnp.exp(b_mid)).astype(BF16), scores,
            (ka * jnp.exp(b_last - b_mid)).astype(BF16), v.astype(BF16), decay)


def _chunk_local(prep):
    _, scores, ks, vb, _ = prep
    return _dot(scores, vb), _dot_tn(ks, vb)


def _chunk_finish(prep, local, state):
    q_in, _, _, _, decay = prep
    o_local, increment = local
    return _dot(q_in, state.astype(BF16)) + o_local, state * decay + increment


def _gated_rmsnorm(o, gate, gain):
    o = o * lax.rsqrt(jnp.mean(o * o, axis=-1, keepdims=True) + RMS_EPS)
    return o * gain * (gate * _sigmoid(gate))


def _lower_bound(logits, lb_row):
    e = jnp.exp(logits - jnp.max(logits, axis=0, keepdims=True))
    return jnp.sum(e[:lb_row + 1], axis=0, keepdims=True) / jnp.sum(e, axis=0, keepdims=True)


def _gla_gates(lr, w_gk2, b_gk2):
    z = jnp.dot(lr, w_gk2, precision=_HI, preferred_element_type=F32) + b_gk2
    return _log_sigmoid(z) * (1.0 / GLA_GATE_NORMALIZER)


def _head_inputs(mode, proj_ref, rows, h, heads, kd, vd, lb, gk_all):
    hk = heads * kd
    hv = heads * vd
    v = proj_ref[rows, 2 * hk + h * vd:2 * hk + (h + 1) * vd]
    gate = proj_ref[rows, 2 * hk + hv + h * vd:2 * hk + hv + (h + 1) * vd]
    if mode == "hgrn":
        q = proj_ref[rows, h * kd:(h + 1) * kd]
        f = proj_ref[rows, hk + h * kd:hk + (h + 1) * kd]
        lbh = lb[:, h * kd:(h + 1) * kd]
        q = q * _sigmoid(q) * (kd ** -0.5)
        fg = lbh + (1.0 - lbh) * _sigmoid(f)
        return q, 1.0 - fg, v, jnp.log(fg), gate
    q = proj_ref[rows, h * kd:(h + 1) * kd] * (kd ** -0.5)
    k = proj_ref[rows, hk + h * kd:hk + (h + 1) * kd]
    return q, k, v, gk_all[:, h * kd:(h + 1) * kd], gate


def _rec_kernel(proj_ref, s0_ref, aux0_ref, aux1_ref, gain_ref, o_ref, s_ref, *,
                mode, heads, kd, vd, chunk, tb, nbatch, lb_row):
    @pl.when(pl.program_id(1) == 0)
    def _():
        s_ref[...] = s0_ref[...]

    hk = heads * kd
    hv = heads * vd
    gain = gain_ref[...]
    lb = _lower_bound(aux0_ref[...], lb_row) if mode == "hgrn" else None

    def one_batch(nb):
        for sc in range(tb // chunk):
            r0 = nb * tb + sc * chunk
            rows = pl.ds(r0, chunk) if isinstance(r0, int) else pl.ds(pl.multiple_of(r0, 8), chunk)
            gk_all = None
            if mode == "gla":
                lr = proj_ref[rows, 2 * hk + 2 * hv:2 * hk + 2 * hv + GLA_RANK_PAD]
                gk_all = _gla_gates(lr, aux0_ref[...], aux1_ref[...])
            inputs = [_head_inputs(mode, proj_ref, rows, h, heads, kd, vd, lb, gk_all) for h in range(heads)]
            preps = [_chunk_prepare(q, k, v, g) for q, k, v, g, _ in inputs]
            local = [_chunk_local(p) for p in preps]
            for h in range(heads):
                o, s_new = _chunk_finish(preps[h], local[h], s_ref[nb, h])
                s_ref[nb, h] = s_new
                o_ref[rows, h * vd:(h + 1) * vd] = _gated_rmsnorm(o, inputs[h][4], gain)

    if nbatch == 1:
        one_batch(0)
    else:
        def body(nb, carry):
            one_batch(nb)
            return carry
        lax.fori_loop(0, nbatch, body, 0, unroll=2)


def _recurrence(proj, s0, aux0, aux1, gain, *, mode, batch, seq, heads, kd, vd, lb_row=0):
    n, width = proj.shape
    chunk = min(REC_CHUNK, seq)
    tb = min(2 * chunk, seq)
    nbatch = 1 if seq > tb else min(8, batch)
    nblk = seq // tb
    grid = (batch // nbatch, nblk)
    state_spec = pl.BlockSpec((nbatch, heads, kd, vd), lambda b, c: (b, 0, 0, 0))
    full2 = lambda a: pl.BlockSpec(a.shape, lambda b, c: (0, 0))
    kern = functools.partial(_rec_kernel, mode=mode, heads=heads, kd=kd, vd=vd, chunk=chunk,
                             tb=tb, nbatch=nbatch, lb_row=lb_row)
    return pl.pallas_call(
        kern,
        out_shape=(jax.ShapeDtypeStruct((n, heads * vd), F32),
                   jax.ShapeDtypeStruct((batch, heads, kd, vd), F32)),
        grid=grid,
        in_specs=[pl.BlockSpec((nbatch * tb, width), lambda b, c: (b * nblk + c, 0)), state_spec,
                  full2(aux0), full2(aux1), full2(gain)],
        out_specs=(pl.BlockSpec((nbatch * tb, heads * vd), lambda b, c: (b * nblk + c, 0)),
                   state_spec),
        compiler_params=_params(("arbitrary", "arbitrary")),
        name="recurrence_" + mode,
    )(proj, s0, aux0, aux1, gain)


def _mixer_kernel(x_ref, w_in_ref, aux0_ref, aux1_ref, gain_ref, w_out_ref, g_ref, b_ref,
                  y_ref, s_ref, proj_scr, o_scr, *, mode, heads, kd, vd, chunk, lb_row, alpha):
    @pl.when(pl.program_id(1) == 0)
    def _():
        s_ref[...] = jnp.zeros(s_ref.shape, F32)

    tb = x_ref.shape[0]
    hk = heads * kd
    hv = heads * vd
    gain = gain_ref[...]
    x = x_ref[...]
    xb = x.astype(BF16)
    group = 256 // kd
    seg_starts = (0, hk, 2 * hk, 2 * hk + hv)
    seg_widths = (kd, kd, vd, vd)

    def project(gi):
        for start, w in zip(seg_starts, seg_widths):
            cols = slice(start + gi * group * w, start + (gi + 1) * group * w)
            proj_scr[:, cols] = _dot(xb, w_in_ref[:, cols])

    lb = None
    gk_blocks = None
    if mode == "hgrn":
        lb = _lower_bound(aux0_ref[...], lb_row)
    else:
        lr = _dot(xb, w_in_ref[:, 2 * hk + 2 * hv:2 * hk + 2 * hv + GLA_RANK_PAD])
        gk_full = _gla_gates(lr, aux0_ref[...], aux1_ref[...])
        gk_blocks = [gk_full[sc * chunk:(sc + 1) * chunk, :] for sc in range(tb // chunk)]

    n_groups = heads // group
    project(0)
    for gi in range(n_groups):
        if gi + 1 < n_groups:
            project(gi + 1)
        units = [(h, sc) for h in range(gi * group, (gi + 1) * group) for sc in range(tb // chunk)]
        inputs = [_head_inputs(mode, proj_scr, pl.ds(sc * chunk, chunk), h, heads, kd, vd, lb,
                               None if gk_blocks is None else gk_blocks[sc]) for h, sc in units]
        preps = [_chunk_prepare(q, k, v, g) for q, k, v, g, _ in inputs]
        local = [_chunk_local(p) for p in preps]
        for u, (h, sc) in enumerate(units):
            o, s_new = _chunk_finish(preps[u], local[u], s_ref[0, h])
            s_ref[0, h] = s_new
            o_scr[pl.ds(sc * chunk, chunk), h * vd:(h + 1) * vd] = (
                _gated_rmsnorm(o, inputs[u][4], gain).astype(BF16))
    c = _dot(o_scr[...], w_out_ref[...])
    y_ref[...] = _layer_norm(alpha * x + c, g_ref[...], b_ref[...])


def _mixer(x, w_in, aux0, aux1, gain, w_out, g, b, alpha, *, mode, batch, seq, heads, kd, vd, lb_row=0):
    n, d = x.shape
    width = w_in.shape[1]
    tb = 4 * REC_CHUNK
    nblk = seq // tb
    const = lambda a: pl.BlockSpec(a.shape, lambda bb, c: (0, 0))
    rows = pl.BlockSpec((tb, d), lambda bb, c: (bb * nblk + c, 0))
    kern = functools.partial(_mixer_kernel, mode=mode, heads=heads, kd=kd, vd=vd, chunk=REC_CHUNK,
                             lb_row=lb_row, alpha=alpha)
    return pl.pallas_call(
        kern,
        out_shape=(jax.ShapeDtypeStruct((n, d), F32),
                   jax.ShapeDtypeStruct((batch, heads, kd, vd), F32)),
        grid=(batch, nblk),
        in_specs=[rows, const(w_in), const(aux0), const(aux1), const(gain), const(w_out), const(g), const(b)],
        out_specs=(rows, pl.BlockSpec((1, heads, kd, vd), lambda bb, c: (bb, 0, 0, 0))),
        scratch_shapes=[pltpu.VMEM((tb, width), F32), pltpu.VMEM((tb, heads * vd), BF16)],
        compiler_params=_params(("arbitrary", "arbitrary")),
        name="mixer_" + mode,
    )(x, w_in, aux0, aux1, gain, w_out, g, b)


def _attn_kernel(q_ref, k_ref, v_ref, o_ref, *, heads, nbatch, tq, per_head):
    hd = q_ref.shape[1] // heads
    scale = hd ** -0.5
    units = [(nb, h) for nb in range(nbatch) for h in range(heads)]

    def head_rows(ref, nb, h):
        if not per_head:
            return ref[nb, :, h * hd:(h + 1) * hd].astype(BF16)
        nt = hd // 128
        period = heads * nt
        mlen = ref.shape[1] // period
        parts = [ref[nb, pl.ds(t * heads + h, mlen, stride=period), :] for t in range(nt)]
        return jnp.concatenate(parts, axis=1).astype(BF16)

    scores = [_dot_nt(q_ref[nb * tq:(nb + 1) * tq, h * hd:(h + 1) * hd].astype(BF16),
                      head_rows(k_ref, nb, h)) * scale for nb, h in units]
    probs = []
    for s in scores:
        p = jnp.exp(s - jnp.max(s, axis=-1, keepdims=True))
        probs.append((p / jnp.sum(p, axis=-1, keepdims=True)).astype(BF16))
    for (nb, h), p in zip(units, probs):
        o_ref[nb * tq:(nb + 1) * tq, h * hd:(h + 1) * hd] = _dot(p, head_rows(v_ref, nb, h)).astype(o_ref.dtype)


def _xattn_kernel(x_ref, wq_ref, k_ref, v_ref, wo_ref, g_ref, b_ref, y_ref, *, heads, alpha):
    x = x_ref[...]
    hd = x.shape[1] // heads
    scale = hd ** -0.5
    q = _dot(x.astype(BF16), wq_ref[...]).astype(BF16)
    nt = hd // 128
    period = heads * nt
    mlen = k_ref.shape[0] // period

    def head_rows(ref, h):
        parts = [ref[pl.ds(t * heads + h, mlen, stride=period), :] for t in range(nt)]
        return jnp.concatenate(parts, axis=1).astype(BF16)

    scores = [_dot_nt(q[:, h * hd:(h + 1) * hd], head_rows(k_ref, h)) * scale for h in range(heads)]
    probs = []
    for s in scores:
        p = jnp.exp(s - jnp.max(s, axis=-1, keepdims=True))
        probs.append((p / jnp.sum(p, axis=-1, keepdims=True)).astype(BF16))
    o = jnp.concatenate([_dot(p, head_rows(v_ref, h)).astype(BF16) for h, p in enumerate(probs)], axis=1)
    y_ref[...] = _layer_norm(alpha * x + _dot(o, wo_ref[...]), g_ref[...], b_ref[...])


def _xattn(x, wq, mem_k, mem_v, wo, layer, g, b, alpha, *, batch, seq, heads, tq):
    n, d = x.shape
    nblk = seq // tq
    const = lambda a: pl.BlockSpec(a.shape, lambda bb, i: (0, 0))
    rows = pl.BlockSpec((tq, d), lambda bb, i: (bb * nblk + i, 0))
    mem = pl.BlockSpec((None, None) + mem_k.shape[2:], lambda bb, i: (layer, bb, 0, 0))
    return pl.pallas_call(
        functools.partial(_xattn_kernel, heads=heads, alpha=alpha),
        out_shape=jax.ShapeDtypeStruct((n, d), F32),
        grid=(batch, nblk),
        in_specs=[rows, const(wq), mem, mem, const(wo), const(g), const(b)],
        out_specs=rows,
        compiler_params=_params(("arbitrary", "arbitrary")),
        name="xattn_block",
    )(x, wq, mem_k, mem_v, wo, g, b)


def _interleaved_rows(mem):
    nl, b, m, heads, hd = mem.shape
    nt = hd // 128
    return mem.reshape(nl, b, m, heads, nt, 128).transpose(0, 1, 2, 4, 3, 5).reshape(nl, b, m * nt * heads, 128)


def _attention(q, mem_k, mem_v, layer, *, batch, seq, heads):
    n, d = q.shape
    per_head = mem_k.shape[3] != d
    tq, nbatch = seq, 2
    nblk = 1
    mem_spec = pl.BlockSpec((None, nbatch) + mem_k.shape[2:], lambda b, i: (layer, b, 0, 0))
    qspec = pl.BlockSpec((nbatch * tq, d), lambda b, i: (b * nblk + i, 0))
    return pl.pallas_call(
        functools.partial(_attn_kernel, heads=heads, nbatch=nbatch, tq=tq, per_head=per_head),
        out_shape=jax.ShapeDtypeStruct((n, d), q.dtype),
        grid=(batch // nbatch, nblk),
        in_specs=[qspec, mem_spec, mem_spec],
        out_specs=qspec,
        compiler_params=_params(("arbitrary", "arbitrary")),
        name="mem_attention",
    )(q, mem_k, mem_v)


def _router_kernel(x_ref, wt_ref, bias_ref, cls_ref):
    logits = lax.dot_general(wt_ref[...], x_ref[...], (((1,), (1,)), ((), ())),
                             precision=_HI, preferred_element_type=F32)
    sel = jax.nn.sigmoid(logits) + bias_ref[...]
    rows = [sel[e:e + 1, :] for e in range(N_GROUPS * GROUP_SIZE)]

    def first_argmax(vals):
        best_v, best_i = vals[0], jnp.zeros(vals[0].shape, I32)
        for i in range(1, len(vals)):
            better = vals[i] > best_v
            best_i = jnp.where(better, i, best_i)
            best_v = jnp.where(better, vals[i], best_v)
        return best_i

    group_scores = []
    for gi in range(N_GROUPS):
        a = rows[gi * GROUP_SIZE:(gi + 1) * GROUP_SIZE]
        top2 = None
        for i in range(GROUP_SIZE):
            for j in range(i + 1, GROUP_SIZE):
                s = a[i] + a[j]
                top2 = s if top2 is None else jnp.maximum(top2, s)
        group_scores.append(top2)
    best = first_argmax(group_scores)
    cand = []
    for j in range(GROUP_SIZE):
        cj = rows[j]
        for gi in range(1, N_GROUPS):
            cj = jnp.where(best == gi, rows[gi * GROUP_SIZE + j], cj)
        cand.append(cj)
    i1 = first_argmax(cand)
    i2 = first_argmax([jnp.where(i1 == j, -jnp.inf, cand[j]) for j in range(GROUP_SIZE)])
    lo = jnp.minimum(i1, i2)
    hi = jnp.maximum(i1, i2)
    pair = jnp.where(lo == 0, hi - 1, jnp.where(lo == 1, hi + 1, 5))
    cls_ref[...] = best * 6 + pair


def _router(x, router_wt, bias_col, tm):
    n, d = x.shape
    ne = router_wt.shape[0]
    return pl.pallas_call(
        _router_kernel,
        out_shape=jax.ShapeDtypeStruct((1, n), I32),
        grid=(n // tm,),
        in_specs=[pl.BlockSpec((tm, d), lambda i: (i, 0)),
                  pl.BlockSpec((ne, d), lambda i: (0, 0)),
                  pl.BlockSpec((ne, 1), lambda i: (0, 0))],
        out_specs=pl.BlockSpec((1, tm), lambda i: (0, i)),
        compiler_params=_params(("arbitrary",)),
        name="router",
    )(x, router_wt, bias_col)


_PAIR_LO = (0, 0, 0, 1, 1, 2)
_PAIR_HI = (1, 2, 3, 2, 3, 3)


def _moe_plan(cls, tile):
    n = cls.shape[0]
    ncls = N_GROUPS * 6
    max_tiles = n // tile + ncls
    shift = max(n - 1, 1).bit_length()
    keys = jnp.sort(cls * (1 << shift) + jnp.arange(n, dtype=I32))
    src = keys & ((1 << shift) - 1)
    cid = jnp.arange(ncls, dtype=I32)
    count = jnp.sum((cls[None, :] == cid[:, None]).astype(I32), axis=1)
    cstart = jnp.cumsum(count) - count
    ntile = (count + tile - 1) // tile
    tend = jnp.cumsum(ntile)
    tid = jnp.arange(max_tiles, dtype=I32)
    n_used = tend[-1]
    tcls = jnp.sum((tid[:, None] >= tend[None, :]).astype(I32), axis=1)
    last_cls = jnp.sum((n_used - 1 >= tend).astype(I32))
    tcls = jnp.where(tid < n_used, tcls, last_cls)
    within = tid - (tend - ntile)[tcls]
    tstart = cstart[tcls] + within * tile
    tcnt = jnp.clip(count[tcls] - within * tile, 0, tile)
    tcnt = jnp.where(tid < n_used, tcnt, 0)
    grp = tcls // 6
    e_lo = grp * GROUP_SIZE + jnp.asarray(_PAIR_LO, I32)[tcls % 6]
    e_hi = grp * GROUP_SIZE + jnp.asarray(_PAIR_HI, I32)[tcls % 6]
    return src, tstart, tcnt, e_lo, e_hi, n_used.reshape(1)


def _moe_kernel(src_ref, tstart_ref, tcnt_ref, elo_ref, ehi_ref, nused_ref,
                x_hbm, rw_lo_ref, rw_hi_ref, wg_lo_ref, wu_lo_ref, wd_lo_ref,
                wg_hi_ref, wu_hi_ref, wd_hi_ref, g_ref, b_ref,
                out_hbm, xbuf, ybuf, gsem, ssem, *, alpha):
    j = pl.program_id(0)
    n_used = nused_ref[0]
    slot = lax.rem(j, 2)
    ngroups, _, d = xbuf.shape[1:]

    def for_rows(cnt, fn):
        ngrp = lax.shift_right_logical(cnt, 3)

        def group(gi, c):
            base = gi * 8
            for u in range(8):
                fn(gi, u, base + u, u % 2)
            return c

        def single(r, c):
            fn(lax.shift_right_logical(r, 3), jnp.bitwise_and(r, 7), r, 0)
            return c

        lax.fori_loop(0, ngrp, group, 0)
        lax.fori_loop(ngrp * 8, cnt, single, 0)

    def wait_rows(cnt, group_copy, row_copy):
        ngrp = lax.shift_right_logical(cnt, 3)

        @pl.when(ngrp > 0)
        def _():
            group_copy(ngrp).wait()

        def single(r, c):
            row_copy(lax.shift_right_logical(r, 3), jnp.bitwise_and(r, 7)).wait()
            return c
        lax.fori_loop(ngrp * 8, cnt, single, 0)

    def start_gather(t, s):
        start = tstart_ref[t]

        def fn(gi, u, r, prio):
            tok = src_ref[start + r]
            pltpu.make_async_copy(x_hbm.at[pl.ds(tok, 1)], xbuf.at[s, gi, pl.ds(u, 1)],
                                  gsem.at[s]).start(priority=prio)
        for_rows(tcnt_ref[t], fn)

    def wait_gather(t, s):
        wait_rows(tcnt_ref[t],
                  lambda n: pltpu.make_async_copy(xbuf.at[s, pl.ds(0, n)], xbuf.at[s, pl.ds(0, n)], gsem.at[s]),
                  lambda gi, u: pltpu.make_async_copy(xbuf.at[s, gi, pl.ds(u, 1)], xbuf.at[s, gi, pl.ds(u, 1)],
                                                      gsem.at[s]))

    def start_scatter(t, s):
        start = tstart_ref[t]

        def fn(gi, u, r, prio):
            tok = src_ref[start + r]
            pltpu.make_async_copy(ybuf.at[s, gi, pl.ds(u, 1)], out_hbm.at[pl.ds(tok, 1)],
                                  ssem.at[s]).start(priority=prio)
        for_rows(tcnt_ref[t], fn)

    def wait_scatter(t, s):
        wait_rows(tcnt_ref[t],
                  lambda n: pltpu.make_async_copy(ybuf.at[s, pl.ds(0, n)], ybuf.at[s, pl.ds(0, n)], ssem.at[s]),
                  lambda gi, u: pltpu.make_async_copy(ybuf.at[s, gi, pl.ds(u, 1)], ybuf.at[s, gi, pl.ds(u, 1)],
                                                      ssem.at[s]))

    @pl.when(j == 0)
    def _():
        xbuf[...] = jnp.zeros(xbuf.shape, F32)
        start_gather(0, 0)

    @pl.when(j < n_used)
    def _():
        @pl.when(j + 1 < n_used)
        def _():
            start_gather(j + 1, 1 - slot)

        wait_gather(j, slot)

        @pl.when(j >= 2)
        def _():
            wait_scatter(j - 2, slot)

        x = xbuf[slot].reshape(ngroups * 8, d)
        xb = x.astype(BF16)

        def expert(rw_ref, wg_ref, wu_ref, wd_ref):
            h = _dot(xb, wg_ref[...])
            h = h * _sigmoid(h) * _dot(xb, wu_ref[...])
            y = _dot(h.astype(BF16), wd_ref[...])
            score = _sigmoid(jnp.sum(x * rw_ref[...], axis=-1, keepdims=True))
            return y, score

        y_lo, s_lo = expert(rw_lo_ref, wg_lo_ref, wu_lo_ref, wd_lo_ref)
        y_hi, s_hi = expert(rw_hi_ref, wg_hi_ref, wu_hi_ref, wd_hi_ref)
        tot = s_lo + s_hi
        m = (s_lo / tot) * y_lo + (s_hi / tot) * y_hi
        ybuf[slot] = _layer_norm(alpha * x + m, g_ref[...], b_ref[...]).reshape(ngroups, 8, d)
        start_scatter(j, slot)

        @pl.when(j == n_used - 1)
        def _():
            @pl.when(j >= 1)
            def _():
                wait_scatter(j - 1, 1 - slot)
            wait_scatter(j, slot)


def _moe(x, cls, router_wt3, wg, wu, wd, layer, g, b, alpha):
    n, d = x.shape
    de = wg.shape[3]
    tile = MOE_TILE if n >= 32 * MOE_TILE else MOE_TILE // 4
    src, tstart, tcnt, e_lo, e_hi, n_used = _moe_plan(cls, tile)
    max_tiles = tstart.shape[0]

    def by_lo(shape):
        return pl.BlockSpec(shape, lambda j, s, ts, tc, el, eh, nu: (el[j], 0, 0))

    def by_hi(shape):
        return pl.BlockSpec(shape, lambda j, s, ts, tc, el, eh, nu: (eh[j], 0, 0))

    def w_lo(r, c):
        return pl.BlockSpec((None, None, r, c), lambda j, s, ts, tc, el, eh, nu: (layer, el[j], 0, 0))

    def w_hi(r, c):
        return pl.BlockSpec((None, None, r, c), lambda j, s, ts, tc, el, eh, nu: (layer, eh[j], 0, 0))

    row = pl.BlockSpec((1, d), lambda j, s, ts, tc, el, eh, nu: (0, 0))
    grid_spec = pltpu.PrefetchScalarGridSpec(
        num_scalar_prefetch=6,
        grid=(max_tiles,),
        in_specs=[pl.BlockSpec(memory_space=pl.ANY),
                  by_lo((None, 1, d)), by_hi((None, 1, d)),
                  w_lo(d, de), w_lo(d, de), w_lo(de, d),
                  w_hi(d, de), w_hi(d, de), w_hi(de, d),
                  row, row],
        out_specs=pl.BlockSpec(memory_space=pl.ANY),
        scratch_shapes=[pltpu.VMEM((2, tile // 8, 8, d), F32), pltpu.VMEM((2, tile // 8, 8, d), F32),
                        pltpu.SemaphoreType.DMA((2,)), pltpu.SemaphoreType.DMA((2,))],
    )
    return pl.pallas_call(
        functools.partial(_moe_kernel, alpha=alpha),
        out_shape=jax.ShapeDtypeStruct((n, d), F32),
        grid_spec=grid_spec,
        compiler_params=_params(("arbitrary",)),
        name="grouped_moe",
    )(src, tstart, tcnt, e_lo, e_hi, n_used,
      x, router_wt3, router_wt3, wg, wu, wd, wg, wu, wd, g, b)


def kernel(x_prompt, x_sample, state_hgrn, state_gla, cache_mem_k, cache_mem_v, mem_prompt, hgrn_w_in, hgrn_lb_logits, hgrn_norm_g, hgrn_w_out, gla_w_in, gla_w_gk2, gla_b_gk2, gla_norm_g, gla_w_out, xattn_w_q, xattn_w_kv, xattn_w_o, router_w, router_bias, moe_w_gate, moe_w_up, moe_w_down, ln_g, ln_b):
    batch, seq, d = x_prompt.shape
    dec_batch, dec_seq, _ = x_sample.shape
    depth = ln_g.shape[0]
    alpha = (2 * depth) ** 0.25
    a_heads, a_key, a_val = state_hgrn.shape[2:]
    b_heads, b_key, b_val = state_gla.shape[2:]
    mem_len, x_heads = cache_mem_k.shape[2], cache_mem_k.shape[3]
    rank = gla_w_gk2.shape[1]
    b_main = 2 * b_heads * b_key + b_heads * b_val + d

    hgrn_w_in_b = hgrn_w_in.astype(BF16)
    hgrn_w_out_b = hgrn_w_out.astype(BF16)
    gla_w_in_b = jnp.concatenate(
        [gla_w_in, jnp.zeros(gla_w_in.shape[:2] + (GLA_RANK_PAD - rank,), F32)], axis=-1).astype(BF16)
    gla_w_gk2_p = jnp.concatenate(
        [gla_w_gk2, jnp.zeros((gla_w_gk2.shape[0], GLA_RANK_PAD - rank, gla_w_gk2.shape[2]), F32)], axis=1)
    gla_w_out_b = gla_w_out.astype(BF16)
    w_q_b = xattn_w_q.astype(BF16)
    w_k_b = xattn_w_kv[:, :, :d].astype(BF16)
    w_v_b = xattn_w_kv[:, :, d:].astype(BF16)
    w_o_b = xattn_w_o.astype(BF16)
    wg_b = moe_w_gate.astype(BF16)
    wu_b = moe_w_up.astype(BF16)
    wd_b = moe_w_down.astype(BF16)
    router_wt = router_w.T
    router_wt3 = router_wt[:, None, :]
    bias_col = router_bias[:, None]
    assert b_main + rank == gla_w_in.shape[2]

    mem_k_p, mem_v_p = _kv_proj(mem_prompt.reshape(batch * mem_len, d), w_k_b, w_v_b, 512, x_heads)
    rows_per_seq = mem_len * d // 128
    mem_k_p = mem_k_p.reshape(depth, batch, rows_per_seq, 128)
    mem_v_p = mem_v_p.reshape(depth, batch, rows_per_seq, 128)

    def head_major(mem):
        nt = d // x_heads // 128
        return mem.reshape(depth, batch, mem_len, nt, x_heads, 128).transpose(0, 1, 2, 4, 3, 5).reshape(
            depth, batch, mem_len, x_heads, d // x_heads)
    mem_k_prompt = head_major(mem_k_p)
    mem_v_prompt = head_major(mem_v_p)

    def run_trunk(x3, states_a, states_b, mem_ks, mem_vs):
        nb, ns, _ = x3.shape
        x = x3.reshape(nb * ns, d)
        new_a, new_b = [], []
        for l in range(depth):
            j = l // 2
            row = lambda a, i: a[l, i][None, :]
            if l % 2 == 0:
                w_in, w_out, states, new = hgrn_w_in_b[j], hgrn_w_out_b[j], states_a, new_a
                aux = (hgrn_lb_logits, hgrn_lb_logits[:1], hgrn_norm_g[j][None, :])
                cfg = dict(mode="hgrn", batch=nb, seq=ns, heads=a_heads, kd=a_key, vd=a_val, lb_row=l)
            else:
                w_in, w_out, states, new = gla_w_in_b[j], gla_w_out_b[j], states_b, new_b
                aux = (gla_w_gk2_p[j], gla_b_gk2[j][None, :], gla_norm_g[j][None, :])
                cfg = dict(mode="gla", batch=nb, seq=ns, heads=b_heads, kd=b_key, vd=b_val)
            if states is None:
                x, s = _mixer(x, w_in, *aux, w_out, row(ln_g, 0), row(ln_b, 0), alpha, **cfg)
            else:
                proj = _linear(x, w_in, F32, 512)
                o, s = _recurrence(proj, states[j], *aux, **cfg)
                x = _linear_res_ln(o, w_out, x, row(ln_g, 0), row(ln_b, 0), alpha, 512)
            new.append(s)
            if ns >= 512:
                x = _xattn(x, w_q_b[l], mem_ks, mem_vs, w_o_b[l], l, row(ln_g, 1), row(ln_b, 1), alpha,
                           batch=nb, seq=ns, heads=x_heads, tq=512)
            else:
                q = _linear(x, w_q_b[l], F32, 512)
                c = _attention(q, mem_ks, mem_vs, l, batch=nb, seq=ns, heads=x_heads)
                x = _linear_res_ln(c, w_o_b[l], x, row(ln_g, 1), row(ln_b, 1), alpha, 512)
            cls = _router(x, router_wt, bias_col, 512)
            x = _moe(x, cls[0], router_wt3, wg_b, wu_b, wd_b, l, row(ln_g, 2), row(ln_b, 2), alpha)
        return x.reshape(nb, ns, d), jnp.stack(new_a), jnp.stack(new_b)

    y_prompt, state_hgrn_prompt, state_gla_prompt = run_trunk(x_prompt, None, None, mem_k_p, mem_v_p)
    y_sample, state_hgrn_sample, state_gla_sample = run_trunk(
        x_sample, state_hgrn, state_gla, _interleaved_rows(cache_mem_k), _interleaved_rows(cache_mem_v))
    return (y_prompt, y_sample, state_hgrn_prompt, state_gla_prompt, mem_k_prompt, mem_v_prompt,
            state_hgrn_sample, state_gla_sample)
```

```python
import functools

import jax
import jax.numpy as jnp
from jax import lax
from jax.experimental import pallas as pl
from jax.experimental.pallas import tpu as pltpu

F32 = jnp.float32
BF16 = jnp.bfloat16
I32 = jnp.int32

N_GROUPS = 4
GROUP_SIZE = 4
GLA_GATE_NORMALIZER = 16.0
LN_EPS = 1e-5
RMS_EPS = 1e-6
GLA_RANK_PAD = 128

VMEM_LIMIT_BYTES = 56 * 1024 * 1024
REC_CHUNK = 64
MOE_TILE = 256

def _params(sem):
    return pltpu.CompilerParams(dimension_semantics=sem, vmem_limit_bytes=VMEM_LIMIT_BYTES)


def _dot(a, b):
    return jnp.dot(a, b, preferred_element_type=F32)


def _dot_nt(a, b):
    return lax.dot_general(a, b, (((1,), (1,)), ((), ())), preferred_element_type=F32)


def _dot_precise(a, b, dot=_dot):
    a_hi = a.astype(BF16)
    b_hi = b.astype(BF16)
    a_lo = (a - a_hi.astype(F32)).astype(BF16)
    b_lo = (b - b_hi.astype(F32)).astype(BF16)
    return dot(a_hi, b_hi) + dot(a_hi, b_lo) + dot(a_lo, b_hi)


def _dot_tn(a, b):
    return lax.dot_general(a, b, (((0,), (0,)), ((), ())), preferred_element_type=F32)


def _layer_norm(z, g, b):
    mu = jnp.mean(z, axis=-1, keepdims=True)
    zc = z - mu
    var = jnp.mean(zc * zc, axis=-1, keepdims=True)
    return zc * lax.rsqrt(var + LN_EPS) * g + b


def _sigmoid(x):
    return 0.5 * jnp.tanh(0.5 * x) + 0.5


def _log_sigmoid(x):
    return jnp.minimum(x, 0.0) - jnp.log(1.0 + jnp.exp(-jnp.abs(x)))


def _linear_kernel(x_ref, w_ref, o_ref):
    o_ref[...] = _dot(x_ref[...].astype(BF16), w_ref[...]).astype(o_ref.dtype)


def _linear(x, w, out_dtype, tm):
    m, k = x.shape
    n = w.shape[1]
    return pl.pallas_call(
        _linear_kernel,
        out_shape=jax.ShapeDtypeStruct((m, n), out_dtype),
        grid=(m // tm,),
        in_specs=[pl.BlockSpec((tm, k), lambda i: (i, 0)),
                  pl.BlockSpec((k, n), lambda i: (0, 0))],
        out_specs=pl.BlockSpec((tm, n), lambda i: (i, 0)),
        compiler_params=_params(("arbitrary",)),
        name="linear",
    )(x, w)


def _kv_proj_kernel(x_ref, wk_ref, wv_ref, k_ref, v_ref, *, heads):
    x = x_ref[...].astype(BF16)
    tm, d = x.shape
    hd = d // heads
    nt = hd // 128
    for w_ref, o_ref in ((wk_ref, k_ref), (wv_ref, v_ref)):
        y = _dot(x, w_ref[...])
        for h in range(heads):
            for t in range(nt):
                o_ref[pl.ds(t * heads + h, tm, stride=heads * nt), :] = y[:, h * hd + t * 128:h * hd + (t + 1) * 128]


def _kv_proj(mem, wk, wv, tm, heads):
    r, d = mem.shape
    nl = wk.shape[0]
    out = jax.ShapeDtypeStruct((nl, r * d // 128, 128), F32)
    wspec = pl.BlockSpec((None, d, d), lambda l, i: (l, 0, 0))
    ospec = pl.BlockSpec((None, tm * d // 128, 128), lambda l, i: (l, i, 0))
    return pl.pallas_call(
        functools.partial(_kv_proj_kernel, heads=heads),
        out_shape=(out, out),
        grid=(nl, r // tm),
        in_specs=[pl.BlockSpec((tm, d), lambda l, i: (i, 0)), wspec, wspec],
        out_specs=(ospec, ospec),
        compiler_params=_params(("arbitrary", "arbitrary")),
        name="kv_proj",
    )(mem, wk, wv)


def _linear_res_ln_kernel(h_ref, w_ref, x_ref, g_ref, b_ref, o_ref, *, alpha):
    c = _dot(h_ref[...].astype(BF16), w_ref[...])
    o_ref[...] = _layer_norm(alpha * x_ref[...] + c, g_ref[...], b_ref[...])


def _linear_res_ln(h, w, x, g, b, alpha, tm):
    m, k = h.shape
    d = w.shape[1]
    row = pl.BlockSpec((1, d), lambda i: (0, 0))
    return pl.pallas_call(
        functools.partial(_linear_res_ln_kernel, alpha=alpha),
        out_shape=jax.ShapeDtypeStruct((m, d), F32),
        grid=(m // tm,),
        in_specs=[pl.BlockSpec((tm, k), lambda i: (i, 0)),
                  pl.BlockSpec((k, d), lambda i: (0, 0)),
                  pl.BlockSpec((tm, d), lambda i: (i, 0)), row, row],
        out_specs=pl.BlockSpec((tm, d), lambda i: (i, 0)),
        compiler_params=_params(("arbitrary",)),
        name="linear_res_ln",
    )(h, w, x, g, b)


def _cumsum_rows(x):
    n = x.shape[0]
    row = lax.broadcasted_iota(I32, x.shape, 0)
    s = 1
    while s < n:
        x = x + jnp.where(row >= s, pltpu.roll(x, s, 0), 0.0)
        s *= 2
    return x


def _chunk_prepare(q, k, v, g):
    c, kd = q.shape
    vd = v.shape[1]
    b = _cumsum_rows(g)
    b_last = b[c - 1:c, :]
    b_mid = b[c // 2:c // 2 + 1, :]
    qa = q * jnp.exp(b - b_mid)
    ka = k * jnp.exp(b_mid - b)
    scores = _dot_nt(qa.astype(BF16), ka.astype(BF16))
    ri = lax.broadcasted_iota(I32, (c, c), 0)
    ci = lax.broadcasted_iota(I32, (c, c), 1)
    scores = jnp.where(ri >= ci, scores, 0.0).astype(BF16)
    decay = jnp.broadcast_to(jnp.exp(b_last), (kd, kd)).T
    if vd != kd:
        decay = jnp.concatenate([decay] * (vd // kd), axis=1)
    return ((qa * jnp.exp(b_mid)).astype(BF16), scores,
            (ka * jnp.exp(b_last - b_mid)).astype(BF16), v.astype(BF16), decay)


def _chunk_local(prep):
    _, scores, ks, vb, _ = prep
    return _dot(scores, vb), _dot_tn(ks, vb)


def _chunk_finish(prep, local, state):
    q_in, _, _, _, decay = prep
    o_local, increment = local
    return _dot(q_in, state.astype(BF16)) + o_local, state * decay + increment


def _gated_rmsnorm(o, gate, gain):
    o = o * lax.rsqrt(jnp.mean(o * o, axis=-1, keepdims=True) + RMS_EPS)
    return o * gain * (gate * _sigmoid(gate))


def _lower_bound(logits, lb_row):
    e = jnp.exp(logits - jnp.max(logits, axis=0, keepdims=True))
    return jnp.sum(e[:lb_row + 1], axis=0, keepdims=True) / jnp.sum(e, axis=0, keepdims=True)


def _gla_gates(lr, w_gk2, b_gk2):
    z = _dot_precise(lr, w_gk2) + b_gk2
    return _log_sigmoid(z) * (1.0 / GLA_GATE_NORMALIZER)


def _head_inputs(mode, proj_ref, rows, h, heads, kd, vd, lb, gk_all):
    hk = heads * kd
    hv = heads * vd
    v = proj_ref[rows, 2 * hk + h * vd:2 * hk + (h + 1) * vd]
    gate = proj_ref[rows, 2 * hk + hv + h * vd:2 * hk + hv + (h + 1) * vd]
    if mode == "hgrn":
        q = proj_ref[rows, h * kd:(h + 1) * kd]
        f = proj_ref[rows, hk + h * kd:hk + (h + 1) * kd]
        lbh = lb[:, h * kd:(h + 1) * kd]
        q = q * _sigmoid(q) * (kd ** -0.5)
        fg = lbh + (1.0 - lbh) * _sigmoid(f)
        return q, 1.0 - fg, v, jnp.log(fg), gate
    q = proj_ref[rows, h * kd:(h + 1) * kd] * (kd ** -0.5)
    k = proj_ref[rows, hk + h * kd:hk + (h + 1) * kd]
    return q, k, v, gk_all[:, h * kd:(h + 1) * kd], gate


def _rec_kernel(proj_ref, s0_ref, aux0_ref, aux1_ref, gain_ref, o_ref, s_ref, *,
                mode, heads, kd, vd, chunk, tb, nbatch, lb_row):
    @pl.when(pl.program_id(1) == 0)
    def _():
        s_ref[...] = s0_ref[...]

    hk = heads * kd
    hv = heads * vd
    gain = gain_ref[...]
    lb = _lower_bound(aux0_ref[...], lb_row) if mode == "hgrn" else None

    def one_batch(nb):
        for sc in range(tb // chunk):
            r0 = nb * tb + sc * chunk
            rows = pl.ds(r0, chunk) if isinstance(r0, int) else pl.ds(pl.multiple_of(r0, 8), chunk)
            gk_all = None
            if mode == "gla":
                lr = proj_ref[rows, 2 * hk + 2 * hv:2 * hk + 2 * hv + GLA_RANK_PAD]
                gk_all = _gla_gates(lr, aux0_ref[...], aux1_ref[...])
            inputs = [_head_inputs(mode, proj_ref, rows, h, heads, kd, vd, lb, gk_all) for h in range(heads)]
            preps = [_chunk_prepare(q, k, v, g) for q, k, v, g, _ in inputs]
            local = [_chunk_local(p) for p in preps]
            for h in range(heads):
                o, s_new = _chunk_finish(preps[h], local[h], s_ref[nb, h])
                s_ref[nb, h] = s_new
                o_ref[rows, h * vd:(h + 1) * vd] = _gated_rmsnorm(o, inputs[h][4], gain)

    if nbatch == 1:
        one_batch(0)
    else:
        def body(nb, carry):
            one_batch(nb)
            return carry
        lax.fori_loop(0, nbatch, body, 0, unroll=2)


def _recurrence(proj, s0, aux0, aux1, gain, *, mode, batch, seq, heads, kd, vd, lb_row=0):
    n, width = proj.shape
    chunk = min(REC_CHUNK, seq)
    tb = min(2 * chunk, seq)
    nbatch = 1 if seq > tb else min(8, batch)
    nblk = seq // tb
    grid = (batch // nbatch, nblk)
    state_spec = pl.BlockSpec((nbatch, heads, kd, vd), lambda b, c: (b, 0, 0, 0))
    full2 = lambda a: pl.BlockSpec(a.shape, lambda b, c: (0, 0))
    kern = functools.partial(_rec_kernel, mode=mode, heads=heads, kd=kd, vd=vd, chunk=chunk,
                             tb=tb, nbatch=nbatch, lb_row=lb_row)
    return pl.pallas_call(
        kern,
        out_shape=(jax.ShapeDtypeStruct((n, heads * vd), F32),
                   jax.ShapeDtypeStruct((batch, heads, kd, vd), F32)),
        grid=grid,
        in_specs=[pl.BlockSpec((nbatch * tb, width), lambda b, c: (b * nblk + c, 0)), state_spec,
                  full2(aux0), full2(aux1), full2(gain)],
        out_specs=(pl.BlockSpec((nbatch * tb, heads * vd), lambda b, c: (b * nblk + c, 0)),
                   state_spec),
        compiler_params=_params(("arbitrary", "arbitrary")),
        name="recurrence_" + mode,
    )(proj, s0, aux0, aux1, gain)


def _mixer_kernel(x_ref, w_in_ref, aux0_ref, aux1_ref, gain_ref, w_out_ref, g_ref, b_ref,
                  y_ref, s_ref, proj_scr, o_scr, *, mode, heads, kd, vd, chunk, lb_row, alpha):
    @pl.when(pl.program_id(1) == 0)
    def _():
        s_ref[...] = jnp.zeros(s_ref.shape, F32)

    tb = x_ref.shape[0]
    hk = heads * kd
    hv = heads * vd
    gain = gain_ref[...]
    x = x_ref[...]
    xb = x.astype(BF16)
    group = 256 // kd
    seg_starts = (0, hk, 2 * hk, 2 * hk + hv)
    seg_widths = (kd, kd, vd, vd)

    def project(gi):
        for start, w in zip(seg_starts, seg_widths):
            cols = slice(start + gi * group * w, start + (gi + 1) * group * w)
            proj_scr[:, cols] = _dot(xb, w_in_ref[:, cols])

    lb = None
    gk_blocks = None
    if mode == "hgrn":
        lb = _lower_bound(aux0_ref[...], lb_row)
    else:
        lr = _dot(xb, w_in_ref[:, 2 * hk + 2 * hv:2 * hk + 2 * hv + GLA_RANK_PAD])
        gk_full = _gla_gates(lr, aux0_ref[...], aux1_ref[...])
        gk_blocks = [gk_full[sc * chunk:(sc + 1) * chunk, :] for sc in range(tb // chunk)]

    n_groups = heads // group
    project(0)
    for gi in range(n_groups):
        if gi + 1 < n_groups:
            project(gi + 1)
        units = [(h, sc) for h in range(gi * group, (gi + 1) * group) for sc in range(tb // chunk)]
        inputs = [_head_inputs(mode, proj_scr, pl.ds(sc * chunk, chunk), h, heads, kd, vd, lb,
                               None if gk_blocks is None else gk_blocks[sc]) for h, sc in units]
        preps = [_chunk_prepare(q, k, v, g) for q, k, v, g, _ in inputs]
        local = [_chunk_local(p) for p in preps]
        for u, (h, sc) in enumerate(units):
            o, s_new = _chunk_finish(preps[u], local[u], s_ref[0, h])
            s_ref[0, h] = s_new
            o_scr[pl.ds(sc * chunk, chunk), h * vd:(h + 1) * vd] = (
                _gated_rmsnorm(o, inputs[u][4], gain).astype(BF16))
    c = _dot(o_scr[...], w_out_ref[...])
    y_ref[...] = _layer_norm(alpha * x + c, g_ref[...], b_ref[...])


def _mixer(x, w_in, aux0, aux1, gain, w_out, g, b, alpha, *, mode, batch, seq, heads, kd, vd, lb_row=0):
    n, d = x.shape
    width = w_in.shape[1]
    tb = 4 * REC_CHUNK
    nblk = seq // tb
    const = lambda a: pl.BlockSpec(a.shape, lambda bb, c: (0, 0))
    rows = pl.BlockSpec((tb, d), lambda bb, c: (bb * nblk + c, 0))
    kern = functools.partial(_mixer_kernel, mode=mode, heads=heads, kd=kd, vd=vd, chunk=REC_CHUNK,
                             lb_row=lb_row, alpha=alpha)
    return pl.pallas_call(
        kern,
        out_shape=(jax.ShapeDtypeStruct((n, d), F32),
                   jax.ShapeDtypeStruct((batch, heads, kd, vd), F32)),
        grid=(batch, nblk),
        in_specs=[rows, const(w_in), const(aux0), const(aux1), const(gain), const(w_out), const(g), const(b)],
        out_specs=(rows, pl.BlockSpec((1, heads, kd, vd), lambda bb, c: (bb, 0, 0, 0))),
        scratch_shapes=[pltpu.VMEM((tb, width), F32), pltpu.VMEM((tb, heads * vd), BF16)],
        compiler_params=_params(("arbitrary", "arbitrary")),
        name="mixer_" + mode,
    )(x, w_in, aux0, aux1, gain, w_out, g, b)


def _attn_kernel(q_ref, k_ref, v_ref, o_ref, *, heads, nbatch, tq, per_head):
    hd = q_ref.shape[1] // heads
    scale = hd ** -0.5
    units = [(nb, h) for nb in range(nbatch) for h in range(heads)]

    def head_rows(ref, nb, h):
        if not per_head:
            return ref[nb, :, h * hd:(h + 1) * hd].astype(BF16)
        nt = hd // 128
        period = heads * nt
        mlen = ref.shape[1] // period
        parts = [ref[nb, pl.ds(t * heads + h, mlen, stride=period), :] for t in range(nt)]
        return jnp.concatenate(parts, axis=1).astype(BF16)

    scores = [_dot_nt(q_ref[nb * tq:(nb + 1) * tq, h * hd:(h + 1) * hd].astype(BF16),
                      head_rows(k_ref, nb, h)) * scale for nb, h in units]
    probs = []
    for s in scores:
        p = jnp.exp(s - jnp.max(s, axis=-1, keepdims=True))
        probs.append((p / jnp.sum(p, axis=-1, keepdims=True)).astype(BF16))
    for (nb, h), p in zip(units, probs):
        o_ref[nb * tq:(nb + 1) * tq, h * hd:(h + 1) * hd] = _dot(p, head_rows(v_ref, nb, h)).astype(o_ref.dtype)


def _xattn_kernel(x_ref, wq_ref, k_ref, v_ref, wo_ref, g_ref, b_ref, rw_ref, rb_ref, y_ref, cls_ref, *,
                  heads, alpha):
    x = x_ref[...]
    hd = x.shape[1] // heads
    scale = hd ** -0.5
    q = _dot(x.astype(BF16), wq_ref[...]).astype(BF16)
    nt = hd // 128
    period = heads * nt
    mlen = k_ref.shape[0] // period

    def head_rows(ref, h):
        parts = [ref[pl.ds(t * heads + h, mlen, stride=period), :] for t in range(nt)]
        return jnp.concatenate(parts, axis=1).astype(BF16)

    scores = [_dot_nt(q[:, h * hd:(h + 1) * hd], head_rows(k_ref, h)) * scale for h in range(heads)]
    probs = []
    for s in scores:
        p = jnp.exp(s - jnp.max(s, axis=-1, keepdims=True))
        probs.append((p / jnp.sum(p, axis=-1, keepdims=True)).astype(BF16))
    o = jnp.concatenate([_dot(p, head_rows(v_ref, h)).astype(BF16) for h, p in enumerate(probs)], axis=1)
    y = _layer_norm(alpha * x + _dot(o, wo_ref[...]), g_ref[...], b_ref[...])
    y_ref[...] = y
    cls_ref[...] = _routing_class(y, rw_ref[...], rb_ref[...])


def _xattn(x, wq, mem_k, mem_v, wo, layer, g, b, router_wt, bias_col, alpha, *, batch, seq, heads, tq):
    n, d = x.shape
    nblk = seq // tq
    const = lambda a: pl.BlockSpec(a.shape, lambda bb, i: (0, 0))
    rows = pl.BlockSpec((tq, d), lambda bb, i: (bb * nblk + i, 0))
    mem = pl.BlockSpec((None, None) + mem_k.shape[2:], lambda bb, i: (layer, bb, 0, 0))
    return pl.pallas_call(
        functools.partial(_xattn_kernel, heads=heads, alpha=alpha),
        out_shape=(jax.ShapeDtypeStruct((n, d), F32), jax.ShapeDtypeStruct((1, n), I32)),
        grid=(batch, nblk),
        in_specs=[rows, const(wq), mem, mem, const(wo), const(g), const(b), const(router_wt), const(bias_col)],
        out_specs=(rows, pl.BlockSpec((1, tq), lambda bb, i: (0, bb * nblk + i))),
        compiler_params=_params(("arbitrary", "arbitrary")),
        name="xattn_block",
    )(x, wq, mem_k, mem_v, wo, g, b, router_wt, bias_col)


def _interleaved_rows(mem):
    nl, b, m, heads, hd = mem.shape
    nt = hd // 128
    return mem.reshape(nl, b, m, heads, nt, 128).transpose(0, 1, 2, 4, 3, 5).reshape(nl, b, m * nt * heads, 128)


def _attention(q, mem_k, mem_v, layer, *, batch, seq, heads):
    n, d = q.shape
    per_head = mem_k.shape[3] != d
    tq, nbatch = seq, 2
    nblk = 1
    mem_spec = pl.BlockSpec((None, nbatch) + mem_k.shape[2:], lambda b, i: (layer, b, 0, 0))
    qspec = pl.BlockSpec((nbatch * tq, d), lambda b, i: (b * nblk + i, 0))
    return pl.pallas_call(
        functools.partial(_attn_kernel, heads=heads, nbatch=nbatch, tq=tq, per_head=per_head),
        out_shape=jax.ShapeDtypeStruct((n, d), q.dtype),
        grid=(batch // nbatch, nblk),
        in_specs=[qspec, mem_spec, mem_spec],
        out_specs=qspec,
        compiler_params=_params(("arbitrary", "arbitrary")),
        name="mem_attention",
    )(q, mem_k, mem_v)


def _routing_class(x, wt, bias_col):
    logits = _dot_precise(wt, x, _dot_nt)
    sel = jax.nn.sigmoid(logits) + bias_col
    rows = [sel[e:e + 1, :] for e in range(N_GROUPS * GROUP_SIZE)]

    def first_argmax(vals):
        best_v, best_i = vals[0], jnp.zeros(vals[0].shape, I32)
        for i in range(1, len(vals)):
            better = vals[i] > best_v
            best_i = jnp.where(better, i, best_i)
            best_v = jnp.where(better, vals[i], best_v)
        return best_i

    group_scores = []
    for gi in range(N_GROUPS):
        a = rows[gi * GROUP_SIZE:(gi + 1) * GROUP_SIZE]
        top2 = None
        for i in range(GROUP_SIZE):
            for j in range(i + 1, GROUP_SIZE):
                s = a[i] + a[j]
                top2 = s if top2 is None else jnp.maximum(top2, s)
        group_scores.append(top2)
    best = first_argmax(group_scores)
    cand = []
    for j in range(GROUP_SIZE):
        cj = rows[j]
        for gi in range(1, N_GROUPS):
            cj = jnp.where(best == gi, rows[gi * GROUP_SIZE + j], cj)
        cand.append(cj)
    i1 = first_argmax(cand)
    i2 = first_argmax([jnp.where(i1 == j, -jnp.inf, cand[j]) for j in range(GROUP_SIZE)])
    lo = jnp.minimum(i1, i2)
    hi = jnp.maximum(i1, i2)
    pair = jnp.where(lo == 0, hi - 1, jnp.where(lo == 1, hi + 1, 5))
    return best * 6 + pair


def _router_kernel(x_ref, wt_ref, bias_ref, cls_ref):
    cls_ref[...] = _routing_class(x_ref[...], wt_ref[...], bias_ref[...])


def _router(x, router_wt, bias_col, tm):
    n, d = x.shape
    ne = router_wt.shape[0]
    return pl.pallas_call(
        _router_kernel,
        out_shape=jax.ShapeDtypeStruct((1, n), I32),
        grid=(n // tm,),
        in_specs=[pl.BlockSpec((tm, d), lambda i: (i, 0)),
                  pl.BlockSpec((ne, d), lambda i: (0, 0)),
                  pl.BlockSpec((ne, 1), lambda i: (0, 0))],
        out_specs=pl.BlockSpec((1, tm), lambda i: (0, i)),
        compiler_params=_params(("arbitrary",)),
        name="router",
    )(x, router_wt, bias_col)


_PAIR_LO = (0, 0, 0, 1, 1, 2)
_PAIR_HI = (1, 2, 3, 2, 3, 3)


def _moe_plan(cls, tile):
    n = cls.shape[0]
    ncls = N_GROUPS * 6
    max_tiles = n // tile + ncls
    shift = max(n - 1, 1).bit_length()
    keys = jnp.sort(cls * (1 << shift) + jnp.arange(n, dtype=I32))
    src = keys & ((1 << shift) - 1)
    cid = jnp.arange(ncls, dtype=I32)
    count = jnp.sum((cls[None, :] == cid[:, None]).astype(I32), axis=1)
    cstart = jnp.cumsum(count) - count
    ntile = (count + tile - 1) // tile
    tend = jnp.cumsum(ntile)
    tid = jnp.arange(max_tiles, dtype=I32)
    n_used = tend[-1]
    tcls = jnp.sum((tid[:, None] >= tend[None, :]).astype(I32), axis=1)
    last_cls = jnp.sum((n_used - 1 >= tend).astype(I32))
    tcls = jnp.where(tid < n_used, tcls, last_cls)
    within = tid - (tend - ntile)[tcls]
    tstart = cstart[tcls] + within * tile
    tcnt = jnp.clip(count[tcls] - within * tile, 0, tile)
    tcnt = jnp.where(tid < n_used, tcnt, 0)
    grp = tcls // 6
    e_lo = grp * GROUP_SIZE + jnp.asarray(_PAIR_LO, I32)[tcls % 6]
    e_hi = grp * GROUP_SIZE + jnp.asarray(_PAIR_HI, I32)[tcls % 6]
    return src, tstart, tcnt, e_lo, e_hi, n_used.reshape(1)


def _moe_kernel(src_ref, tstart_ref, tcnt_ref, elo_ref, ehi_ref, nused_ref,
                x_hbm, rw_lo_ref, rw_hi_ref, wg_lo_ref, wu_lo_ref, wd_lo_ref,
                wg_hi_ref, wu_hi_ref, wd_hi_ref, g_ref, b_ref,
                out_hbm, xbuf, ybuf, gsem, ssem, *, alpha):
    j = pl.program_id(0)
    n_used = nused_ref[0]
    slot = lax.rem(j, 2)
    ngroups, _, d = xbuf.shape[1:]

    def for_rows(cnt, fn):
        ngrp = lax.shift_right_logical(cnt, 3)

        def group(gi, c):
            base = gi * 8
            for u in range(8):
                fn(gi, u, base + u, u % 2)
            return c

        def single(r, c):
            fn(lax.shift_right_logical(r, 3), jnp.bitwise_and(r, 7), r, 0)
            return c

        lax.fori_loop(0, ngrp, group, 0)
        lax.fori_loop(ngrp * 8, cnt, single, 0)

    def wait_rows(cnt, group_copy, row_copy):
        ngrp = lax.shift_right_logical(cnt, 3)

        @pl.when(ngrp > 0)
        def _():
            group_copy(ngrp).wait()

        def single(r, c):
            row_copy(lax.shift_right_logical(r, 3), jnp.bitwise_and(r, 7)).wait()
            return c
        lax.fori_loop(ngrp * 8, cnt, single, 0)

    def start_gather(t, s):
        start = tstart_ref[t]

        def fn(gi, u, r, prio):
            tok = src_ref[start + r]
            pltpu.make_async_copy(x_hbm.at[pl.ds(tok, 1)], xbuf.at[s, gi, pl.ds(u, 1)],
                                  gsem.at[s]).start(priority=prio)
        for_rows(tcnt_ref[t], fn)

    def wait_gather(t, s):
        wait_rows(tcnt_ref[t],
                  lambda n: pltpu.make_async_copy(xbuf.at[s, pl.ds(0, n)], xbuf.at[s, pl.ds(0, n)], gsem.at[s]),
                  lambda gi, u: pltpu.make_async_copy(xbuf.at[s, gi, pl.ds(u, 1)], xbuf.at[s, gi, pl.ds(u, 1)],
                                                      gsem.at[s]))

    def start_scatter(t, s):
        start = tstart_ref[t]

        def fn(gi, u, r, prio):
            tok = src_ref[start + r]
            pltpu.make_async_copy(ybuf.at[s, gi, pl.ds(u, 1)], out_hbm.at[pl.ds(tok, 1)],
                                  ssem.at[s]).start(priority=prio)
        for_rows(tcnt_ref[t], fn)

    def wait_scatter(t, s):
        wait_rows(tcnt_ref[t],
                  lambda n: pltpu.make_async_copy(ybuf.at[s, pl.ds(0, n)], ybuf.at[s, pl.ds(0, n)], ssem.at[s]),
                  lambda gi, u: pltpu.make_async_copy(ybuf.at[s, gi, pl.ds(u, 1)], ybuf.at[s, gi, pl.ds(u, 1)],
                                                      ssem.at[s]))

    @pl.when(j == 0)
    def _():
        xbuf[...] = jnp.zeros(xbuf.shape, F32)
        start_gather(0, 0)

    @pl.when(j < n_used)
    def _():
        @pl.when(j + 1 < n_used)
        def _():
            start_gather(j + 1, 1 - slot)

        wait_gather(j, slot)

        @pl.when(j >= 2)
        def _():
            wait_scatter(j - 2, slot)

        x = xbuf[slot].reshape(ngroups * 8, d)
        xb = x.astype(BF16)

        def expert(rw_ref, wg_ref, wu_ref, wd_ref):
            h = _dot(xb, wg_ref[...].astype(BF16))
            h = h * _sigmoid(h) * _dot(xb, wu_ref[...].astype(BF16))
            y = _dot(h.astype(BF16), wd_ref[...].astype(BF16))
            score = _sigmoid(jnp.sum(x * rw_ref[...], axis=-1, keepdims=True))
            return y, score

        y_lo, s_lo = expert(rw_lo_ref, wg_lo_ref, wu_lo_ref, wd_lo_ref)
        y_hi, s_hi = expert(rw_hi_ref, wg_hi_ref, wu_hi_ref, wd_hi_ref)
        tot = s_lo + s_hi
        m = (s_lo / tot) * y_lo + (s_hi / tot) * y_hi
        ybuf[slot] = _layer_norm(alpha * x + m, g_ref[...], b_ref[...]).reshape(ngroups, 8, d)
        start_scatter(j, slot)

        @pl.when(j == n_used - 1)
        def _():
            @pl.when(j >= 1)
            def _():
                wait_scatter(j - 1, 1 - slot)
            wait_scatter(j, slot)


def _moe(x, cls, router_wt3, wg, wu, wd, layer, g, b, alpha):
    n, d = x.shape
    de = wg.shape[3]
    tile = MOE_TILE if n >= 32 * MOE_TILE else MOE_TILE // 4
    src, tstart, tcnt, e_lo, e_hi, n_used = _moe_plan(cls, tile)
    max_tiles = tstart.shape[0]

    def by_lo(shape):
        return pl.BlockSpec(shape, lambda j, s, ts, tc, el, eh, nu: (el[j], 0, 0))

    def by_hi(shape):
        return pl.BlockSpec(shape, lambda j, s, ts, tc, el, eh, nu: (eh[j], 0, 0))

    def w_lo(r, c):
        return pl.BlockSpec((None, None, r, c), lambda j, s, ts, tc, el, eh, nu: (layer, el[j], 0, 0))

    def w_hi(r, c):
        return pl.BlockSpec((None, None, r, c), lambda j, s, ts, tc, el, eh, nu: (layer, eh[j], 0, 0))

    row = pl.BlockSpec((1, d), lambda j, s, ts, tc, el, eh, nu: (0, 0))
    grid_spec = pltpu.PrefetchScalarGridSpec(
        num_scalar_prefetch=6,
        grid=(max_tiles,),
        in_specs=[pl.BlockSpec(memory_space=pl.ANY),
                  by_lo((None, 1, d)), by_hi((None, 1, d)),
                  w_lo(d, de), w_lo(d, de), w_lo(de, d),
                  w_hi(d, de), w_hi(d, de), w_hi(de, d),
                  row, row],
        out_specs=pl.BlockSpec(memory_space=pl.ANY),
        scratch_shapes=[pltpu.VMEM((2, tile // 8, 8, d), F32), pltpu.VMEM((2, tile // 8, 8, d), F32),
                        pltpu.SemaphoreType.DMA((2,)), pltpu.SemaphoreType.DMA((2,))],
    )
    return pl.pallas_call(
        functools.partial(_moe_kernel, alpha=alpha),
        out_shape=jax.ShapeDtypeStruct((n, d), F32),
        grid_spec=grid_spec,
        compiler_params=_params(("arbitrary",)),
        name="grouped_moe",
    )(src, tstart, tcnt, e_lo, e_hi, n_used,
      x, router_wt3, router_wt3, wg, wu, wd, wg, wu, wd, g, b)


def kernel(x_prompt, x_sample, state_hgrn, state_gla, cache_mem_k, cache_mem_v, mem_prompt, hgrn_w_in, hgrn_lb_logits, hgrn_norm_g, hgrn_w_out, gla_w_in, gla_w_gk2, gla_b_gk2, gla_norm_g, gla_w_out, xattn_w_q, xattn_w_kv, xattn_w_o, router_w, router_bias, moe_w_gate, moe_w_up, moe_w_down, ln_g, ln_b):
    batch, seq, d = x_prompt.shape
    dec_batch, dec_seq, _ = x_sample.shape
    depth = ln_g.shape[0]
    alpha = (2 * depth) ** 0.25
    a_heads, a_key, a_val = state_hgrn.shape[2:]
    b_heads, b_key, b_val = state_gla.shape[2:]
    mem_len, x_heads = cache_mem_k.shape[2], cache_mem_k.shape[3]
    rank = gla_w_gk2.shape[1]
    b_main = 2 * b_heads * b_key + b_heads * b_val + d

    hgrn_w_in_b = hgrn_w_in.astype(BF16)
    hgrn_w_out_b = hgrn_w_out.astype(BF16)
    gla_w_in_b = jnp.concatenate(
        [gla_w_in, jnp.zeros(gla_w_in.shape[:2] + (GLA_RANK_PAD - rank,), F32)], axis=-1).astype(BF16)
    gla_w_gk2_p = jnp.concatenate(
        [gla_w_gk2, jnp.zeros((gla_w_gk2.shape[0], GLA_RANK_PAD - rank, gla_w_gk2.shape[2]), F32)], axis=1)
    gla_w_out_b = gla_w_out.astype(BF16)
    w_q_b = xattn_w_q.astype(BF16)
    w_k_b = xattn_w_kv[:, :, :d].astype(BF16)
    w_v_b = xattn_w_kv[:, :, d:].astype(BF16)
    w_o_b = xattn_w_o.astype(BF16)
    wg_b, wu_b, wd_b = moe_w_gate, moe_w_up, moe_w_down
    router_wt = router_w.T
    router_wt3 = router_wt[:, None, :]
    bias_col = router_bias[:, None]
    assert b_main + rank == gla_w_in.shape[2]

    mem_k_p, mem_v_p = _kv_proj(mem_prompt.reshape(batch * mem_len, d), w_k_b, w_v_b, 512, x_heads)
    rows_per_seq = mem_len * d // 128
    mem_k_p = mem_k_p.reshape(depth, batch, rows_per_seq, 128)
    mem_v_p = mem_v_p.reshape(depth, batch, rows_per_seq, 128)

    def head_major(mem):
        nt = d // x_heads // 128
        return mem.reshape(depth, batch, mem_len, nt, x_heads, 128).transpose(0, 1, 2, 4, 3, 5).reshape(
            depth, batch, mem_len, x_heads, d // x_heads)
    mem_k_prompt = head_major(mem_k_p)
    mem_v_prompt = head_major(mem_v_p)

    def run_trunk(x3, states_a, states_b, mem_ks, mem_vs):
        nb, ns, _ = x3.shape
        x = x3.reshape(nb * ns, d)
        new_a, new_b = [], []
        for l in range(depth):
            j = l // 2
            row = lambda a, i: a[l, i][None, :]
            if l % 2 == 0:
                w_in, w_out, states, new = hgrn_w_in_b[j], hgrn_w_out_b[j], states_a, new_a
                aux = (hgrn_lb_logits, hgrn_lb_logits[:1], hgrn_norm_g[j][None, :])
                cfg = dict(mode="hgrn", batch=nb, seq=ns, heads=a_heads, kd=a_key, vd=a_val, lb_row=l)
            else:
                w_in, w_out, states, new = gla_w_in_b[j], gla_w_out_b[j], states_b, new_b
                aux = (gla_w_gk2_p[j], gla_b_gk2[j][None, :], gla_norm_g[j][None, :])
                cfg = dict(mode="gla", batch=nb, seq=ns, heads=b_heads, kd=b_key, vd=b_val)
            if states is None:
                x, s = _mixer(x, w_in, *aux, w_out, row(ln_g, 0), row(ln_b, 0), alpha, **cfg)
            else:
                proj = _linear(x, w_in, F32, 512)
                o, s = _recurrence(proj, states[j], *aux, **cfg)
                x = _linear_res_ln(o, w_out, x, row(ln_g, 0), row(ln_b, 0), alpha, 512)
            new.append(s)
            if ns >= 512:
                x, cls = _xattn(x, w_q_b[l], mem_ks, mem_vs, w_o_b[l], l, row(ln_g, 1), row(ln_b, 1),
                                router_wt, bias_col, alpha, batch=nb, seq=ns, heads=x_heads, tq=512)
            else:
                q = _linear(x, w_q_b[l], F32, 512)
                c = _attention(q, mem_ks, mem_vs, l, batch=nb, seq=ns, heads=x_heads)
                x = _linear_res_ln(c, w_o_b[l], x, row(ln_g, 1), row(ln_b, 1), alpha, 512)
                cls = _router(x, router_wt, bias_col, 512)
            x = _moe(x, cls[0], router_wt3, wg_b, wu_b, wd_b, l, row(ln_g, 2), row(ln_b, 2), alpha)
        return x.reshape(nb, ns, d), jnp.stack(new_a), jnp.stack(new_b)

    y_prompt, state_hgrn_prompt, state_gla_prompt = run_trunk(x_prompt, None, None, mem_k_p, mem_v_p)
    y_sample, state_hgrn_sample, state_gla_sample = run_trunk(
        x_sample, state_hgrn, state_gla, _interleaved_rows(cache_mem_k), _interleaved_rows(cache_mem_v))
    return (y_prompt, y_sample, state_hgrn_prompt, state_gla_prompt, mem_k_prompt, mem_v_prompt,
            state_hgrn_sample, state_gla_sample)
```

```python
import functools

import jax
import jax.numpy as jnp
from jax import lax
from jax.experimental import pallas as pl
from jax.experimental.pallas import tpu as pltpu

F32 = jnp.float32
BF16 = jnp.bfloat16
I32 = jnp.int32

N_GROUPS = 4
GROUP_SIZE = 4
GLA_GATE_NORMALIZER = 16.0
LN_EPS = 1e-5
RMS_EPS = 1e-6
GLA_RANK_PAD = 128

VMEM_LIMIT_BYTES = 56 * 1024 * 1024
REC_CHUNK = 64
MOE_TILE = 256

def _params(sem):
    return pltpu.CompilerParams(dimension_semantics=sem, vmem_limit_bytes=VMEM_LIMIT_BYTES)


def _dot(a, b):
    return jnp.dot(a, b, preferred_element_type=F32)


def _dot_nt(a, b):
    return lax.dot_general(a, b, (((1,), (1,)), ((), ())), preferred_element_type=F32)


def _dot_precise(a, b, dot=_dot):
    a_hi = a.astype(BF16)
    b_hi = b.astype(BF16)
    a_lo = (a - a_hi.astype(F32)).astype(BF16)
    b_lo = (b - b_hi.astype(F32)).astype(BF16)
    return dot(a_hi, b_hi) + dot(a_hi, b_lo) + dot(a_lo, b_hi)


def _dot_tn(a, b):
    return lax.dot_general(a, b, (((0,), (0,)), ((), ())), preferred_element_type=F32)


def _layer_norm(z, g, b):
    mu = jnp.mean(z, axis=-1, keepdims=True)
    zc = z - mu
    var = jnp.mean(zc * zc, axis=-1, keepdims=True)
    return zc * lax.rsqrt(var + LN_EPS) * g + b


def _sigmoid(x):
    return 0.5 * jnp.tanh(0.5 * x) + 0.5


def _log_sigmoid(x):
    return jnp.minimum(x, 0.0) - jnp.log(1.0 + jnp.exp(-jnp.abs(x)))


def _linear_kernel(x_ref, w_ref, o_ref):
    o_ref[...] = _dot(x_ref[...].astype(BF16), w_ref[...]).astype(o_ref.dtype)


def _linear(x, w, out_dtype, tm):
    m, k = x.shape
    n = w.shape[1]
    return pl.pallas_call(
        _linear_kernel,
        out_shape=jax.ShapeDtypeStruct((m, n), out_dtype),
        grid=(m // tm,),
        in_specs=[pl.BlockSpec((tm, k), lambda i: (i, 0)),
                  pl.BlockSpec((k, n), lambda i: (0, 0))],
        out_specs=pl.BlockSpec((tm, n), lambda i: (i, 0)),
        compiler_params=_params(("arbitrary",)),
        name="linear",
    )(x, w)


def _kv_proj_kernel(x_ref, wk_ref, wv_ref, k_ref, v_ref, *, heads):
    x = x_ref[...].astype(BF16)
    tm, d = x.shape
    hd = d // heads
    nt = hd // 128
    for w_ref, o_ref in ((wk_ref, k_ref), (wv_ref, v_ref)):
        y = _dot(x, w_ref[...])
        for h in range(heads):
            for t in range(nt):
                o_ref[pl.ds(t * heads + h, tm, stride=heads * nt), :] = y[:, h * hd + t * 128:h * hd + (t + 1) * 128]


def _kv_proj(mem, wk, wv, tm, heads):
    r, d = mem.shape
    nl = wk.shape[0]
    out = jax.ShapeDtypeStruct((nl, r * d // 128, 128), F32)
    wspec = pl.BlockSpec((None, d, d), lambda l, i: (l, 0, 0))
    ospec = pl.BlockSpec((None, tm * d // 128, 128), lambda l, i: (l, i, 0))
    return pl.pallas_call(
        functools.partial(_kv_proj_kernel, heads=heads),
        out_shape=(out, out),
        grid=(nl, r // tm),
        in_specs=[pl.BlockSpec((tm, d), lambda l, i: (i, 0)), wspec, wspec],
        out_specs=(ospec, ospec),
        compiler_params=_params(("arbitrary", "arbitrary")),
        name="kv_proj",
    )(mem, wk, wv)


def _linear_res_ln_kernel(h_ref, w_ref, x_ref, g_ref, b_ref, o_ref, *, alpha):
    c = _dot(h_ref[...].astype(BF16), w_ref[...])
    o_ref[...] = _layer_norm(alpha * x_ref[...] + c, g_ref[...], b_ref[...])


def _linear_res_ln(h, w, x, g, b, alpha, tm):
    m, k = h.shape
    d = w.shape[1]
    row = pl.BlockSpec((1, d), lambda i: (0, 0))
    return pl.pallas_call(
        functools.partial(_linear_res_ln_kernel, alpha=alpha),
        out_shape=jax.ShapeDtypeStruct((m, d), F32),
        grid=(m // tm,),
        in_specs=[pl.BlockSpec((tm, k), lambda i: (i, 0)),
                  pl.BlockSpec((k, d), lambda i: (0, 0)),
                  pl.BlockSpec((tm, d), lambda i: (i, 0)), row, row],
        out_specs=pl.BlockSpec((tm, d), lambda i: (i, 0)),
        compiler_params=_params(("arbitrary",)),
        name="linear_res_ln",
    )(h, w, x, g, b)


def _cumsum_rows(x):
    n = x.shape[0]
    row = lax.broadcasted_iota(I32, x.shape, 0)
    s = 1
    while s < n:
        x = x + jnp.where(row >= s, pltpu.roll(x, s, 0), 0.0)
        s *= 2
    return x


def _chunk_prepare(q, k, v, g):
    c, kd = q.shape
    vd = v.shape[1]
    b = _cumsum_rows(g)
    b_last = b[c - 1:c, :]
    b_mid = b[c // 2:c // 2 + 1, :]
    qa = q * jnp.exp(b - b_mid)
    ka = k * jnp.exp(b_mid - b)
    scores = _dot_nt(qa.astype(BF16), ka.astype(BF16))
    ri = lax.broadcasted_iota(I32, (c, c), 0)
    ci = lax.broadcasted_iota(I32, (c, c), 1)
    scores = jnp.where(ri >= ci, scores, 0.0).astype(BF16)
    decay = jnp.broadcast_to(jnp.exp(b_last), (kd, kd)).T
    if vd != kd:
        decay = jnp.concatenate([decay] * (vd // kd), axis=1)
    return ((qa * jnp.exp(b_mid)).astype(BF16), scores,
            (ka * jnp.exp(b_last - b_mid)).astype(BF16), v.astype(BF16), decay)


def _chunk_local(prep):
    _, scores, ks, vb, _ = prep
    return _dot(scores, vb), _dot_tn(ks, vb)


def _chunk_finish(prep, local, state):
    q_in, _, _, _, decay = prep
    o_local, increment = local
    return _dot(q_in, state.astype(BF16)) + o_local, state * decay + increment


def _gated_rmsnorm(o, gate, gain):
    o = o * lax.rsqrt(jnp.mean(o * o, axis=-1, keepdims=True) + RMS_EPS)
    return o * gain * (gate * _sigmoid(gate))


def _lower_bound(logits, lb_row):
    e = jnp.exp(logits - jnp.max(logits, axis=0, keepdims=True))
    return jnp.sum(e[:lb_row + 1], axis=0, keepdims=True) / jnp.sum(e, axis=0, keepdims=True)


def _gla_gates(lr, w_gk2, b_gk2):
    z = _dot_precise(lr, w_gk2) + b_gk2
    return _log_sigmoid(z) * (1.0 / GLA_GATE_NORMALIZER)


def _head_inputs(mode, proj_ref, rows, h, heads, kd, vd, lb, gk_all):
    hk = heads * kd
    hv = heads * vd
    v = proj_ref[rows, 2 * hk + h * vd:2 * hk + (h + 1) * vd]
    gate = proj_ref[rows, 2 * hk + hv + h * vd:2 * hk + hv + (h + 1) * vd]
    if mode == "hgrn":
        q = proj_ref[rows, h * kd:(h + 1) * kd]
        f = proj_ref[rows, hk + h * kd:hk + (h + 1) * kd]
        lbh = lb[:, h * kd:(h + 1) * kd]
        q = q * _sigmoid(q) * (kd ** -0.5)
        fg = lbh + (1.0 - lbh) * _sigmoid(f)
        return q, 1.0 - fg, v, jnp.log(fg), gate
    q = proj_ref[rows, h * kd:(h + 1) * kd] * (kd ** -0.5)
    k = proj_ref[rows, hk + h * kd:hk + (h + 1) * kd]
    return q, k, v, gk_all[:, h * kd:(h + 1) * kd], gate


def _rec_kernel(proj_ref, s0_ref, aux0_ref, aux1_ref, gain_ref, o_ref, s_ref, *,
                mode, heads, kd, vd, chunk, tb, nbatch, lb_row):
    @pl.when(pl.program_id(1) == 0)
    def _():
        s_ref[...] = s0_ref[...]

    hk = heads * kd
    hv = heads * vd
    gain = gain_ref[...]
    lb = _lower_bound(aux0_ref[...], lb_row) if mode == "hgrn" else None

    def one_batch(nb):
        for sc in range(tb // chunk):
            r0 = nb * tb + sc * chunk
            rows = pl.ds(r0, chunk) if isinstance(r0, int) else pl.ds(pl.multiple_of(r0, 8), chunk)
            gk_all = None
            if mode == "gla":
                lr = proj_ref[rows, 2 * hk + 2 * hv:2 * hk + 2 * hv + GLA_RANK_PAD]
                gk_all = _gla_gates(lr, aux0_ref[...], aux1_ref[...])
            inputs = [_head_inputs(mode, proj_ref, rows, h, heads, kd, vd, lb, gk_all) for h in range(heads)]
            preps = [_chunk_prepare(q, k, v, g) for q, k, v, g, _ in inputs]
            local = [_chunk_local(p) for p in preps]
            for h in range(heads):
                o, s_new = _chunk_finish(preps[h], local[h], s_ref[nb, h])
                s_ref[nb, h] = s_new
                o_ref[rows, h * vd:(h + 1) * vd] = _gated_rmsnorm(o, inputs[h][4], gain)

    if nbatch == 1:
        one_batch(0)
    else:
        def body(nb, carry):
            one_batch(nb)
            return carry
        lax.fori_loop(0, nbatch, body, 0, unroll=4)


def _recurrence(proj, s0, aux0, aux1, gain, *, mode, batch, seq, heads, kd, vd, lb_row=0):
    n, width = proj.shape
    chunk = min(REC_CHUNK, seq)
    tb = min(2 * chunk, seq)
    nbatch = 1 if seq > tb else min(8, batch)
    nblk = seq // tb
    grid = (batch // nbatch, nblk)
    state_spec = pl.BlockSpec((nbatch, heads, kd, vd), lambda b, c: (b, 0, 0, 0))
    full2 = lambda a: pl.BlockSpec(a.shape, lambda b, c: (0, 0))
    kern = functools.partial(_rec_kernel, mode=mode, heads=heads, kd=kd, vd=vd, chunk=chunk,
                             tb=tb, nbatch=nbatch, lb_row=lb_row)
    return pl.pallas_call(
        kern,
        out_shape=(jax.ShapeDtypeStruct((n, heads * vd), F32),
                   jax.ShapeDtypeStruct((batch, heads, kd, vd), F32)),
        grid=grid,
        in_specs=[pl.BlockSpec((nbatch * tb, width), lambda b, c: (b * nblk + c, 0)), state_spec,
                  full2(aux0), full2(aux1), full2(gain)],
        out_specs=(pl.BlockSpec((nbatch * tb, heads * vd), lambda b, c: (b * nblk + c, 0)),
                   state_spec),
        compiler_params=_params(("arbitrary", "arbitrary")),
        name="recurrence_" + mode,
    )(proj, s0, aux0, aux1, gain)


def _mixer_kernel(x_ref, w_in_ref, aux0_ref, aux1_ref, gain_ref, w_out_ref, g_ref, b_ref,
                  y_ref, s_ref, proj_scr, o_scr, *, mode, heads, kd, vd, chunk, lb_row, alpha):
    @pl.when(pl.program_id(1) == 0)
    def _():
        s_ref[...] = jnp.zeros(s_ref.shape, F32)

    tb = x_ref.shape[0]
    hk = heads * kd
    hv = heads * vd
    gain = gain_ref[...]
    x = x_ref[...]
    xb = x.astype(BF16)
    group = 256 // kd
    seg_starts = (0, hk, 2 * hk, 2 * hk + hv)
    seg_widths = (kd, kd, vd, vd)

    def project(gi):
        for start, w in zip(seg_starts, seg_widths):
            cols = slice(start + gi * group * w, start + (gi + 1) * group * w)
            proj_scr[:, cols] = _dot(xb, w_in_ref[:, cols])

    lb = None
    gk_blocks = None
    if mode == "hgrn":
        lb = _lower_bound(aux0_ref[...], lb_row)
    else:
        lr = _dot(xb, w_in_ref[:, 2 * hk + 2 * hv:2 * hk + 2 * hv + GLA_RANK_PAD])
        gk_full = _gla_gates(lr, aux0_ref[...], aux1_ref[...])
        gk_blocks = [gk_full[sc * chunk:(sc + 1) * chunk, :] for sc in range(tb // chunk)]

    n_groups = heads // group
    project(0)
    for gi in range(n_groups):
        if gi + 1 < n_groups:
            project(gi + 1)
        units = [(h, sc) for h in range(gi * group, (gi + 1) * group) for sc in range(tb // chunk)]
        inputs = [_head_inputs(mode, proj_scr, pl.ds(sc * chunk, chunk), h, heads, kd, vd, lb,
                               None if gk_blocks is None else gk_blocks[sc]) for h, sc in units]
        preps = [_chunk_prepare(q, k, v, g) for q, k, v, g, _ in inputs]
        local = [_chunk_local(p) for p in preps]
        for u, (h, sc) in enumerate(units):
            o, s_new = _chunk_finish(preps[u], local[u], s_ref[0, h])
            s_ref[0, h] = s_new
            o_scr[pl.ds(sc * chunk, chunk), h * vd:(h + 1) * vd] = (
                _gated_rmsnorm(o, inputs[u][4], gain).astype(BF16))
    c = _dot(o_scr[...], w_out_ref[...])
    y_ref[...] = _layer_norm(alpha * x + c, g_ref[...], b_ref[...])


def _mixer(x, w_in, aux0, aux1, gain, w_out, g, b, alpha, *, mode, batch, seq, heads, kd, vd, lb_row=0):
    n, d = x.shape
    width = w_in.shape[1]
    tb = 4 * REC_CHUNK
    nblk = seq // tb
    const = lambda a: pl.BlockSpec(a.shape, lambda bb, c: (0, 0))
    rows = pl.BlockSpec((tb, d), lambda bb, c: (bb * nblk + c, 0))
    kern = functools.partial(_mixer_kernel, mode=mode, heads=heads, kd=kd, vd=vd, chunk=REC_CHUNK,
                             lb_row=lb_row, alpha=alpha)
    return pl.pallas_call(
        kern,
        out_shape=(jax.ShapeDtypeStruct((n, d), F32),
                   jax.ShapeDtypeStruct((batch, heads, kd, vd), F32)),
        grid=(batch, nblk),
        in_specs=[rows, const(w_in), const(aux0), const(aux1), const(gain), const(w_out), const(g), const(b)],
        out_specs=(rows, pl.BlockSpec((1, heads, kd, vd), lambda bb, c: (bb, 0, 0, 0))),
        scratch_shapes=[pltpu.VMEM((tb, width), F32), pltpu.VMEM((tb, heads * vd), BF16)],
        compiler_params=_params(("arbitrary", "arbitrary")),
        name="mixer_" + mode,
    )(x, w_in, aux0, aux1, gain, w_out, g, b)


def _attn_kernel(q_ref, k_ref, v_ref, o_ref, *, heads, nbatch, tq, per_head):
    hd = q_ref.shape[1] // heads
    scale = hd ** -0.5
    units = [(nb, h) for nb in range(nbatch) for h in range(heads)]

    def head_rows(ref, nb, h):
        if not per_head:
            return ref[nb, :, h * hd:(h + 1) * hd].astype(BF16)
        nt = hd // 128
        period = heads * nt
        mlen = ref.shape[1] // period
        parts = [ref[nb, pl.ds(t * heads + h, mlen, stride=period), :] for t in range(nt)]
        return jnp.concatenate(parts, axis=1).astype(BF16)

    scores = [_dot_nt(q_ref[nb * tq:(nb + 1) * tq, h * hd:(h + 1) * hd].astype(BF16),
                      head_rows(k_ref, nb, h)) * scale for nb, h in units]
    probs = []
    for s in scores:
        p = jnp.exp(s - jnp.max(s, axis=-1, keepdims=True))
        probs.append((p / jnp.sum(p, axis=-1, keepdims=True)).astype(BF16))
    for (nb, h), p in zip(units, probs):
        o_ref[nb * tq:(nb + 1) * tq, h * hd:(h + 1) * hd] = _dot(p, head_rows(v_ref, nb, h)).astype(o_ref.dtype)


def _xattn_kernel(x_ref, wq_ref, k_ref, v_ref, wo_ref, g_ref, b_ref, rw_ref, rb_ref, y_ref, cls_ref, *,
                  heads, alpha):
    x = x_ref[...]
    hd = x.shape[1] // heads
    scale = hd ** -0.5
    q = _dot(x.astype(BF16), wq_ref[...]).astype(BF16)
    nt = hd // 128
    period = heads * nt
    mlen = k_ref.shape[0] // period

    def head_rows(ref, h):
        parts = [ref[pl.ds(t * heads + h, mlen, stride=period), :] for t in range(nt)]
        return jnp.concatenate(parts, axis=1).astype(BF16)

    scores = [_dot_nt(q[:, h * hd:(h + 1) * hd], head_rows(k_ref, h)) * scale for h in range(heads)]
    probs = []
    for s in scores:
        p = jnp.exp(s - jnp.max(s, axis=-1, keepdims=True))
        probs.append((p / jnp.sum(p, axis=-1, keepdims=True)).astype(BF16))
    o = jnp.concatenate([_dot(p, head_rows(v_ref, h)).astype(BF16) for h, p in enumerate(probs)], axis=1)
    y = _layer_norm(alpha * x + _dot(o, wo_ref[...]), g_ref[...], b_ref[...])
    y_ref[...] = y
    cls_ref[...] = _routing_class(y, rw_ref[...], rb_ref[...])


def _xattn(x, wq, mem_k, mem_v, wo, layer, g, b, router_wt, bias_col, alpha, *, batch, seq, heads, tq):
    n, d = x.shape
    nblk = seq // tq
    const = lambda a: pl.BlockSpec(a.shape, lambda bb, i: (0, 0))
    rows = pl.BlockSpec((tq, d), lambda bb, i: (bb * nblk + i, 0))
    mem = pl.BlockSpec((None, None) + mem_k.shape[2:], lambda bb, i: (layer, bb, 0, 0))
    return pl.pallas_call(
        functools.partial(_xattn_kernel, heads=heads, alpha=alpha),
        out_shape=(jax.ShapeDtypeStruct((n, d), F32), jax.ShapeDtypeStruct((1, n), I32)),
        grid=(batch, nblk),
        in_specs=[rows, const(wq), mem, mem, const(wo), const(g), const(b), const(router_wt), const(bias_col)],
        out_specs=(rows, pl.BlockSpec((1, tq), lambda bb, i: (0, bb * nblk + i))),
        compiler_params=_params(("arbitrary", "arbitrary")),
        name="xattn_block",
    )(x, wq, mem_k, mem_v, wo, g, b, router_wt, bias_col)


def _interleaved_rows(mem):
    nl, b, m, heads, hd = mem.shape
    nt = hd // 128
    return mem.reshape(nl, b, m, heads, nt, 128).transpose(0, 1, 2, 4, 3, 5).reshape(nl, b, m * nt * heads, 128)


def _attention(q, mem_k, mem_v, layer, *, batch, seq, heads):
    n, d = q.shape
    per_head = mem_k.shape[3] != d
    tq, nbatch = seq, 4
    nblk = 1
    mem_spec = pl.BlockSpec((None, nbatch) + mem_k.shape[2:], lambda b, i: (layer, b, 0, 0))
    qspec = pl.BlockSpec((nbatch * tq, d), lambda b, i: (b * nblk + i, 0))
    return pl.pallas_call(
        functools.partial(_attn_kernel, heads=heads, nbatch=nbatch, tq=tq, per_head=per_head),
        out_shape=jax.ShapeDtypeStruct((n, d), q.dtype),
        grid=(batch // nbatch, nblk),
        in_specs=[qspec, mem_spec, mem_spec],
        out_specs=qspec,
        compiler_params=_params(("arbitrary", "arbitrary")),
        name="mem_attention",
    )(q, mem_k, mem_v)


def _routing_class(x, wt, bias_col):
    logits = _dot_precise(wt, x, _dot_nt)
    sel = jax.nn.sigmoid(logits) + bias_col
    rows = [sel[e:e + 1, :] for e in range(N_GROUPS * GROUP_SIZE)]

    def first_argmax(vals):
        best_v, best_i = vals[0], jnp.zeros(vals[0].shape, I32)
        for i in range(1, len(vals)):
            better = vals[i] > best_v
            best_i = jnp.where(better, i, best_i)
            best_v = jnp.where(better, vals[i], best_v)
        return best_i

    group_scores = []
    for gi in range(N_GROUPS):
        a = rows[gi * GROUP_SIZE:(gi + 1) * GROUP_SIZE]
        top2 = None
        for i in range(GROUP_SIZE):
            for j in range(i + 1, GROUP_SIZE):
                s = a[i] + a[j]
                top2 = s if top2 is None else jnp.maximum(top2, s)
        group_scores.append(top2)
    best = first_argmax(group_scores)
    cand = []
    for j in range(GROUP_SIZE):
        cj = rows[j]
        for gi in range(1, N_GROUPS):
            cj = jnp.where(best == gi, rows[gi * GROUP_SIZE + j], cj)
        cand.append(cj)
    i1 = first_argmax(cand)
    i2 = first_argmax([jnp.where(i1 == j, -jnp.inf, cand[j]) for j in range(GROUP_SIZE)])
    lo = jnp.minimum(i1, i2)
    hi = jnp.maximum(i1, i2)
    pair = jnp.where(lo == 0, hi - 1, jnp.where(lo == 1, hi + 1, 5))
    return best * 6 + pair


def _router_kernel(x_ref, wt_ref, bias_ref, cls_ref):
    cls_ref[...] = _routing_class(x_ref[...], wt_ref[...], bias_ref[...])


def _router(x, router_wt, bias_col, tm):
    n, d = x.shape
    ne = router_wt.shape[0]
    return pl.pallas_call(
        _router_kernel,
        out_shape=jax.ShapeDtypeStruct((1, n), I32),
        grid=(n // tm,),
        in_specs=[pl.BlockSpec((tm, d), lambda i: (i, 0)),
                  pl.BlockSpec((ne, d), lambda i: (0, 0)),
                  pl.BlockSpec((ne, 1), lambda i: (0, 0))],
        out_specs=pl.BlockSpec((1, tm), lambda i: (0, i)),
        compiler_params=_params(("arbitrary",)),
        name="router",
    )(x, router_wt, bias_col)


_PAIR_LO = (0, 0, 0, 1, 1, 2)
_PAIR_HI = (1, 2, 3, 2, 3, 3)


def _moe_plan(cls, tile):
    n = cls.shape[0]
    ncls = N_GROUPS * 6
    max_tiles = n // tile + ncls
    shift = max(n - 1, 1).bit_length()
    keys = jnp.sort(cls * (1 << shift) + jnp.arange(n, dtype=I32))
    src = keys & ((1 << shift) - 1)
    cid = jnp.arange(ncls, dtype=I32)
    count = jnp.sum((cls[None, :] == cid[:, None]).astype(I32), axis=1)
    cstart = jnp.cumsum(count) - count
    ntile = (count + tile - 1) // tile
    tend = jnp.cumsum(ntile)
    tid = jnp.arange(max_tiles, dtype=I32)
    n_used = tend[-1]
    tcls = jnp.sum((tid[:, None] >= tend[None, :]).astype(I32), axis=1)
    last_cls = jnp.sum((n_used - 1 >= tend).astype(I32))
    tcls = jnp.where(tid < n_used, tcls, last_cls)
    within = tid - (tend - ntile)[tcls]
    tstart = cstart[tcls] + within * tile
    tcnt = jnp.clip(count[tcls] - within * tile, 0, tile)
    tcnt = jnp.where(tid < n_used, tcnt, 0)
    grp = tcls // 6
    e_lo = grp * GROUP_SIZE + jnp.asarray(_PAIR_LO, I32)[tcls % 6]
    e_hi = grp * GROUP_SIZE + jnp.asarray(_PAIR_HI, I32)[tcls % 6]
    return src, tstart, tcnt, e_lo, e_hi, n_used.reshape(1)


def _moe_kernel(src_ref, tstart_ref, tcnt_ref, elo_ref, ehi_ref, nused_ref,
                x_hbm, rw_lo_ref, rw_hi_ref, wg_lo_ref, wu_lo_ref, wd_lo_ref,
                wg_hi_ref, wu_hi_ref, wd_hi_ref, g_ref, b_ref,
                out_hbm, xbuf, ybuf, gsem, ssem, *, alpha):
    j = pl.program_id(0)
    n_used = nused_ref[0]
    parity = lax.rem(j, 2)
    ngroups, _, d = xbuf.shape[1:]
    block = 16

    def for_rows(cnt, fn):
        nblk = lax.shift_right_logical(cnt, 4)
        for blk in range(ngroups * 8 // block):
            @pl.when(blk < nblk)
            def _():
                for u in range(block):
                    r = blk * block + u
                    fn(r // 8, r % 8, r, u % 2)

        def single(r, c):
            fn(lax.shift_right_logical(r, 3), jnp.bitwise_and(r, 7), r, 0)
            return c

        lax.fori_loop(nblk * block, cnt, single, 0)

    def wait_rows(cnt, group_copy, row_copy):
        ngrp = lax.shift_right_logical(cnt, 3)

        @pl.when(ngrp > 0)
        def _():
            group_copy(ngrp).wait()

        def single(r, c):
            row_copy(lax.shift_right_logical(r, 3), jnp.bitwise_and(r, 7)).wait()
            return c
        lax.fori_loop(ngrp * 8, cnt, single, 0)

    def start_gather(t, s):
        start = tstart_ref[t]

        def fn(gi, u, r, prio):
            tok = src_ref[start + r]
            pltpu.make_async_copy(x_hbm.at[pl.ds(tok, 1)], xbuf.at[s, gi, pl.ds(u, 1)],
                                  gsem.at[s]).start(priority=prio)
        for_rows(tcnt_ref[t], fn)

    def wait_gather(t, s):
        wait_rows(tcnt_ref[t],
                  lambda n: pltpu.make_async_copy(xbuf.at[s, pl.ds(0, n)], xbuf.at[s, pl.ds(0, n)], gsem.at[s]),
                  lambda gi, u: pltpu.make_async_copy(xbuf.at[s, gi, pl.ds(u, 1)], xbuf.at[s, gi, pl.ds(u, 1)],
                                                      gsem.at[s]))

    def start_scatter(t, s):
        start = tstart_ref[t]

        def fn(gi, u, r, prio):
            tok = src_ref[start + r]
            pltpu.make_async_copy(ybuf.at[s, gi, pl.ds(u, 1)], out_hbm.at[pl.ds(tok, 1)],
                                  ssem.at[s]).start(priority=prio)
        for_rows(tcnt_ref[t], fn)

    def wait_scatter(t, s):
        wait_rows(tcnt_ref[t],
                  lambda n: pltpu.make_async_copy(ybuf.at[s, pl.ds(0, n)], ybuf.at[s, pl.ds(0, n)], ssem.at[s]),
                  lambda gi, u: pltpu.make_async_copy(ybuf.at[s, gi, pl.ds(u, 1)], ybuf.at[s, gi, pl.ds(u, 1)],
                                                      ssem.at[s]))

    @pl.when(j == 0)
    def _():
        xbuf[...] = jnp.zeros(xbuf.shape, F32)
        start_gather(0, 0)

    def step(slot):
        @pl.when(j + 1 < n_used)
        def _():
            start_gather(j + 1, 1 - slot)

        wait_gather(j, slot)

        @pl.when(j >= 2)
        def _():
            wait_scatter(j - 2, slot)

        x = xbuf[slot].reshape(ngroups * 8, d)
        xb = x.astype(BF16)

        def expert(rw_ref, wg_ref, wu_ref, wd_ref):
            h = _dot(xb, wg_ref[...].astype(BF16))
            h = h * _sigmoid(h) * _dot(xb, wu_ref[...].astype(BF16))
            y = _dot(h.astype(BF16), wd_ref[...].astype(BF16))
            score = _sigmoid(jnp.sum(x * rw_ref[...], axis=-1, keepdims=True))
            return y, score

        y_lo, s_lo = expert(rw_lo_ref, wg_lo_ref, wu_lo_ref, wd_lo_ref)
        y_hi, s_hi = expert(rw_hi_ref, wg_hi_ref, wu_hi_ref, wd_hi_ref)
        tot = s_lo + s_hi
        m = (s_lo / tot) * y_lo + (s_hi / tot) * y_hi
        ybuf[slot] = _layer_norm(alpha * x + m, g_ref[...], b_ref[...]).reshape(ngroups, 8, d)
        start_scatter(j, slot)

        @pl.when(j == n_used - 1)
        def _():
            @pl.when(j >= 1)
            def _():
                wait_scatter(j - 1, 1 - slot)
            wait_scatter(j, slot)

    for s in range(2):
        pl.when(jnp.logical_and(j < n_used, parity == s))(functools.partial(step, s))


def _moe(x, cls, router_wt3, wg, wu, wd, layer, g, b, alpha):
    n, d = x.shape
    de = wg.shape[3]
    tile = MOE_TILE if n >= 32 * MOE_TILE else MOE_TILE // 4
    src, tstart, tcnt, e_lo, e_hi, n_used = _moe_plan(cls, tile)
    max_tiles = tstart.shape[0]

    def by_lo(shape):
        return pl.BlockSpec(shape, lambda j, s, ts, tc, el, eh, nu: (el[j], 0, 0))

    def by_hi(shape):
        return pl.BlockSpec(shape, lambda j, s, ts, tc, el, eh, nu: (eh[j], 0, 0))

    def w_lo(r, c):
        return pl.BlockSpec((None, None, r, c), lambda j, s, ts, tc, el, eh, nu: (layer, el[j], 0, 0))

    def w_hi(r, c):
        return pl.BlockSpec((None, None, r, c), lambda j, s, ts, tc, el, eh, nu: (layer, eh[j], 0, 0))

    row = pl.BlockSpec((1, d), lambda j, s, ts, tc, el, eh, nu: (0, 0))
    grid_spec = pltpu.PrefetchScalarGridSpec(
        num_scalar_prefetch=6,
        grid=(max_tiles,),
        in_specs=[pl.BlockSpec(memory_space=pl.ANY),
                  by_lo((None, 1, d)), by_hi((None, 1, d)),
                  w_lo(d, de), w_lo(d, de), w_lo(de, d),
                  w_hi(d, de), w_hi(d, de), w_hi(de, d),
                  row, row],
        out_specs=pl.BlockSpec(memory_space=pl.ANY),
        scratch_shapes=[pltpu.VMEM((2, tile // 8, 8, d), F32), pltpu.VMEM((2, tile // 8, 8, d), F32),
                        pltpu.SemaphoreType.DMA((2,)), pltpu.SemaphoreType.DMA((2,))],
    )
    return pl.pallas_call(
        functools.partial(_moe_kernel, alpha=alpha),
        out_shape=jax.ShapeDtypeStruct((n, d), F32),
        grid_spec=grid_spec,
        compiler_params=_params(("arbitrary",)),
        name="grouped_moe",
    )(src, tstart, tcnt, e_lo, e_hi, n_used,
      x, router_wt3, router_wt3, wg, wu, wd, wg, wu, wd, g, b)


def kernel(x_prompt, x_sample, state_hgrn, state_gla, cache_mem_k, cache_mem_v, mem_prompt, hgrn_w_in, hgrn_lb_logits, hgrn_norm_g, hgrn_w_out, gla_w_in, gla_w_gk2, gla_b_gk2, gla_norm_g, gla_w_out, xattn_w_q, xattn_w_kv, xattn_w_o, router_w, router_bias, moe_w_gate, moe_w_up, moe_w_down, ln_g, ln_b):
    batch, seq, d = x_prompt.shape
    dec_batch, dec_seq, _ = x_sample.shape
    depth = ln_g.shape[0]
    alpha = (2 * depth) ** 0.25
    a_heads, a_key, a_val = state_hgrn.shape[2:]
    b_heads, b_key, b_val = state_gla.shape[2:]
    mem_len, x_heads = cache_mem_k.shape[2], cache_mem_k.shape[3]
    rank = gla_w_gk2.shape[1]
    b_main = 2 * b_heads * b_key + b_heads * b_val + d

    hgrn_w_in_b = hgrn_w_in.astype(BF16)
    hgrn_w_out_b = hgrn_w_out.astype(BF16)
    gla_w_in_b = jnp.concatenate(
        [gla_w_in, jnp.zeros(gla_w_in.shape[:2] + (GLA_RANK_PAD - rank,), F32)], axis=-1).astype(BF16)
    gla_w_gk2_p = jnp.concatenate(
        [gla_w_gk2, jnp.zeros((gla_w_gk2.shape[0], GLA_RANK_PAD - rank, gla_w_gk2.shape[2]), F32)], axis=1)
    gla_w_out_b = gla_w_out.astype(BF16)
    w_q_b = xattn_w_q.astype(BF16)
    w_k_b = xattn_w_kv[:, :, :d].astype(BF16)
    w_v_b = xattn_w_kv[:, :, d:].astype(BF16)
    w_o_b = xattn_w_o.astype(BF16)
    wg_b, wu_b, wd_b = moe_w_gate, moe_w_up, moe_w_down
    router_wt = router_w.T
    router_wt3 = router_wt[:, None, :]
    bias_col = router_bias[:, None]
    assert b_main + rank == gla_w_in.shape[2]

    mem_k_p, mem_v_p = _kv_proj(mem_prompt.reshape(batch * mem_len, d), w_k_b, w_v_b, 512, x_heads)
    rows_per_seq = mem_len * d // 128
    mem_k_p = mem_k_p.reshape(depth, batch, rows_per_seq, 128)
    mem_v_p = mem_v_p.reshape(depth, batch, rows_per_seq, 128)

    def head_major(mem):
        nt = d // x_heads // 128
        return mem.reshape(depth, batch, mem_len, nt, x_heads, 128).transpose(0, 1, 2, 4, 3, 5).reshape(
            depth, batch, mem_len, x_heads, d // x_heads)
    mem_k_prompt = head_major(mem_k_p)
    mem_v_prompt = head_major(mem_v_p)

    def run_trunk(x3, states_a, states_b, mem_ks, mem_vs):
        nb, ns, _ = x3.shape
        x = x3.reshape(nb * ns, d)
        new_a, new_b = [], []
        for l in range(depth):
            j = l // 2
            row = lambda a, i: a[l, i][None, :]
            if l % 2 == 0:
                w_in, w_out, states, new = hgrn_w_in_b[j], hgrn_w_out_b[j], states_a, new_a
                aux = (hgrn_lb_logits, hgrn_lb_logits[:1], hgrn_norm_g[j][None, :])
                cfg = dict(mode="hgrn", batch=nb, seq=ns, heads=a_heads, kd=a_key, vd=a_val, lb_row=l)
            else:
                w_in, w_out, states, new = gla_w_in_b[j], gla_w_out_b[j], states_b, new_b
                aux = (gla_w_gk2_p[j], gla_b_gk2[j][None, :], gla_norm_g[j][None, :])
                cfg = dict(mode="gla", batch=nb, seq=ns, heads=b_heads, kd=b_key, vd=b_val)
            if states is None:
                x, s = _mixer(x, w_in, *aux, w_out, row(ln_g, 0), row(ln_b, 0), alpha, **cfg)
            else:
                proj = _linear(x, w_in, F32, 512)
                o, s = _recurrence(proj, states[j], *aux, **cfg)
                x = _linear_res_ln(o, w_out, x, row(ln_g, 0), row(ln_b, 0), alpha, 512)
            new.append(s)
            if ns >= 512:
                x, cls = _xattn(x, w_q_b[l], mem_ks, mem_vs, w_o_b[l], l, row(ln_g, 1), row(ln_b, 1),
                                router_wt, bias_col, alpha, batch=nb, seq=ns, heads=x_heads, tq=512)
            else:
                q = _linear(x, w_q_b[l], F32, 512)
                c = _attention(q, mem_ks, mem_vs, l, batch=nb, seq=ns, heads=x_heads)
                x = _linear_res_ln(c, w_o_b[l], x, row(ln_g, 1), row(ln_b, 1), alpha, 512)
                cls = _router(x, router_wt, bias_col, 512)
            x = _moe(x, cls[0], router_wt3, wg_b, wu_b, wd_b, l, row(ln_g, 2), row(ln_b, 2), alpha)
        return x.reshape(nb, ns, d), jnp.stack(new_a), jnp.stack(new_b)

    y_prompt, state_hgrn_prompt, state_gla_prompt = run_trunk(x_prompt, None, None, mem_k_p, mem_v_p)
    y_sample, state_hgrn_sample, state_gla_sample = run_trunk(
        x_sample, state_hgrn, state_gla, _interleaved_rows(cache_mem_k), _interleaved_rows(cache_mem_v))
    return (y_prompt, y_sample, state_hgrn_prompt, state_gla_prompt, mem_k_prompt, mem_v_prompt,
            state_hgrn_sample, state_gla_sample)
```

```python
import functools

import jax
import jax.numpy as jnp
from jax import lax
from jax.experimental import pallas as pl
from jax.experimental.pallas import tpu as pltpu

F32 = jnp.float32
BF16 = jnp.bfloat16
I32 = jnp.int32

N_GROUPS = 4
GROUP_SIZE = 4
GLA_GATE_NORMALIZER = 16.0
LN_EPS = 1e-5
RMS_EPS = 1e-6
GLA_RANK_PAD = 128

VMEM_LIMIT_BYTES = 56 * 1024 * 1024
REC_CHUNK = 64
MOE_TILE = 256

def _params(sem):
    return pltpu.CompilerParams(dimension_semantics=sem, vmem_limit_bytes=VMEM_LIMIT_BYTES)


def _dot(a, b):
    return jnp.dot(a, b, preferred_element_type=F32)


def _dot_nt(a, b):
    return lax.dot_general(a, b, (((1,), (1,)), ((), ())), preferred_element_type=F32)


def _dot_precise(a, b, dot=_dot):
    a_hi = a.astype(BF16)
    b_hi = b.astype(BF16)
    a_lo = (a - a_hi.astype(F32)).astype(BF16)
    b_lo = (b - b_hi.astype(F32)).astype(BF16)
    return dot(a_hi, b_hi) + dot(a_hi, b_lo) + dot(a_lo, b_hi)


def _dot_tn(a, b):
    return lax.dot_general(a, b, (((0,), (0,)), ((), ())), preferred_element_type=F32)


def _layer_norm(z, g, b):
    mu = jnp.mean(z, axis=-1, keepdims=True)
    zc = z - mu
    var = jnp.mean(zc * zc, axis=-1, keepdims=True)
    return zc * lax.rsqrt(var + LN_EPS) * g + b


def _sigmoid(x):
    return 0.5 * jnp.tanh(0.5 * x) + 0.5


def _log_sigmoid(x):
    return jnp.minimum(x, 0.0) - jnp.log(1.0 + jnp.exp(-jnp.abs(x)))


def _linear_kernel(x_ref, w_ref, o_ref):
    o_ref[...] = _dot(x_ref[...].astype(BF16), w_ref[...]).astype(o_ref.dtype)


def _linear(x, w, out_dtype, tm):
    m, k = x.shape
    n = w.shape[1]
    return pl.pallas_call(
        _linear_kernel,
        out_shape=jax.ShapeDtypeStruct((m, n), out_dtype),
        grid=(m // tm,),
        in_specs=[pl.BlockSpec((tm, k), lambda i: (i, 0)),
                  pl.BlockSpec((k, n), lambda i: (0, 0))],
        out_specs=pl.BlockSpec((tm, n), lambda i: (i, 0)),
        compiler_params=_params(("arbitrary",)),
        name="linear",
    )(x, w)


def _kv_proj_kernel(x_ref, wk_ref, wv_ref, k_ref, v_ref, *, heads):
    x = x_ref[...].astype(BF16)
    tm, d = x.shape
    hd = d // heads
    nt = hd // 128
    for w_ref, o_ref in ((wk_ref, k_ref), (wv_ref, v_ref)):
        y = _dot(x, w_ref[...])
        for h in range(heads):
            for t in range(nt):
                o_ref[pl.ds(t * heads + h, tm, stride=heads * nt), :] = y[:, h * hd + t * 128:h * hd + (t + 1) * 128]


def _kv_proj(mem, wk, wv, tm, heads):
    r, d = mem.shape
    nl = wk.shape[0]
    out = jax.ShapeDtypeStruct((nl, r * d // 128, 128), F32)
    wspec = pl.BlockSpec((None, d, d), lambda l, i: (l, 0, 0))
    ospec = pl.BlockSpec((None, tm * d // 128, 128), lambda l, i: (l, i, 0))
    return pl.pallas_call(
        functools.partial(_kv_proj_kernel, heads=heads),
        out_shape=(out, out),
        grid=(nl, r // tm),
        in_specs=[pl.BlockSpec((tm, d), lambda l, i: (i, 0)), wspec, wspec],
        out_specs=(ospec, ospec),
        compiler_params=_params(("arbitrary", "arbitrary")),
        name="kv_proj",
    )(mem, wk, wv)


def _linear_res_ln_kernel(h_ref, w_ref, x_ref, g_ref, b_ref, o_ref, *, alpha):
    c = _dot(h_ref[...].astype(BF16), w_ref[...])
    o_ref[...] = _layer_norm(alpha * x_ref[...] + c, g_ref[...], b_ref[...])


def _linear_res_ln(h, w, x, g, b, alpha, tm):
    m, k = h.shape
    d = w.shape[1]
    row = pl.BlockSpec((1, d), lambda i: (0, 0))
    return pl.pallas_call(
        functools.partial(_linear_res_ln_kernel, alpha=alpha),
        out_shape=jax.ShapeDtypeStruct((m, d), F32),
        grid=(m // tm,),
        in_specs=[pl.BlockSpec((tm, k), lambda i: (i, 0)),
                  pl.BlockSpec((k, d), lambda i: (0, 0)),
                  pl.BlockSpec((tm, d), lambda i: (i, 0)), row, row],
        out_specs=pl.BlockSpec((tm, d), lambda i: (i, 0)),
        compiler_params=_params(("arbitrary",)),
        name="linear_res_ln",
    )(h, w, x, g, b)


def _cumsum_rows(x):
    n = x.shape[0]
    row = lax.broadcasted_iota(I32, x.shape, 0)
    s = 1
    while s < n:
        x = x + jnp.where(row >= s, pltpu.roll(x, s, 0), 0.0)
        s *= 2
    return x


def _chunk_prepare(q, k, v, g):
    c, kd = q.shape
    vd = v.shape[1]
    b = _cumsum_rows(g)
    b_last = b[c - 1:c, :]
    b_mid = b[c // 2:c // 2 + 1, :]
    qa = q * jnp.exp(b - b_mid)
    ka = k * jnp.exp(b_mid - b)
    scores = _dot_nt(qa.astype(BF16), ka.astype(BF16))
    ri = lax.broadcasted_iota(I32, (c, c), 0)
    ci = lax.broadcasted_iota(I32, (c, c), 1)
    scores = jnp.where(ri >= ci, scores, 0.0).astype(BF16)
    decay = jnp.broadcast_to(jnp.exp(b_last), (kd, kd)).T
    if vd != kd:
        decay = jnp.concatenate([decay] * (vd // kd), axis=1)
    return ((qa * jnp.exp(b_mid)).astype(BF16), scores,
            (ka * jnp.exp(b_last - b_mid)).astype(BF16), v.astype(BF16), decay)


def _chunk_local(prep):
    _, scores, ks, vb, _ = prep
    return _dot(scores, vb), _dot_tn(ks, vb)


def _chunk_finish(prep, local, state):
    q_in, _, _, _, decay = prep
    o_local, increment = local
    return _dot(q_in, state.astype(BF16)) + o_local, state * decay + increment


def _gated_rmsnorm(o, gate, gain):
    o = o * lax.rsqrt(jnp.mean(o * o, axis=-1, keepdims=True) + RMS_EPS)
    return o * (0.5 * gain) * (gate * (jnp.tanh(0.5 * gate) + 1.0))


def _lower_bound(logits, lb_row):
    e = jnp.exp(logits - jnp.max(logits, axis=0, keepdims=True))
    return jnp.sum(e[:lb_row + 1], axis=0, keepdims=True) / jnp.sum(e, axis=0, keepdims=True)


def _gla_gates(lr, w_gk2, b_gk2):
    z = _dot_precise(lr, w_gk2) + b_gk2
    return _log_sigmoid(z) * (1.0 / GLA_GATE_NORMALIZER)


def _head_inputs(mode, proj_ref, rows, h, heads, kd, vd, lb, gk_all):
    hk = heads * kd
    hv = heads * vd
    v = proj_ref[rows, 2 * hk + h * vd:2 * hk + (h + 1) * vd]
    gate = proj_ref[rows, 2 * hk + hv + h * vd:2 * hk + hv + (h + 1) * vd]
    if mode == "hgrn":
        q = proj_ref[rows, h * kd:(h + 1) * kd]
        f = proj_ref[rows, hk + h * kd:hk + (h + 1) * kd]
        lbh = lb[:, h * kd:(h + 1) * kd]
        q = (0.5 * kd ** -0.5) * q * (jnp.tanh(0.5 * q) + 1.0)
        fg = 0.5 * (1.0 + lbh) + (0.5 * (1.0 - lbh)) * jnp.tanh(0.5 * f)
        return q, 1.0 - fg, v, jnp.log(fg), gate
    q = proj_ref[rows, h * kd:(h + 1) * kd] * (kd ** -0.5)
    k = proj_ref[rows, hk + h * kd:hk + (h + 1) * kd]
    return q, k, v, gk_all[:, h * kd:(h + 1) * kd], gate


def _rec_kernel(proj_ref, s0_ref, aux0_ref, aux1_ref, gain_ref, o_ref, s_ref, *,
                mode, heads, kd, vd, chunk, tb, nbatch, lb_row):
    @pl.when(pl.program_id(1) == 0)
    def _():
        s_ref[...] = s0_ref[...]

    hk = heads * kd
    hv = heads * vd
    gain = gain_ref[...]
    lb = _lower_bound(aux0_ref[...], lb_row) if mode == "hgrn" else None

    def one_batch(nb):
        for sc in range(tb // chunk):
            r0 = nb * tb + sc * chunk
            rows = pl.ds(r0, chunk) if isinstance(r0, int) else pl.ds(pl.multiple_of(r0, 8), chunk)
            gk_all = None
            if mode == "gla":
                lr = proj_ref[rows, 2 * hk + 2 * hv:2 * hk + 2 * hv + GLA_RANK_PAD]
                gk_all = _gla_gates(lr, aux0_ref[...], aux1_ref[...])
            inputs = [_head_inputs(mode, proj_ref, rows, h, heads, kd, vd, lb, gk_all) for h in range(heads)]
            preps = [_chunk_prepare(q, k, v, g) for q, k, v, g, _ in inputs]
            local = [_chunk_local(p) for p in preps]
            for h in range(heads):
                o, s_new = _chunk_finish(preps[h], local[h], s_ref[nb, h])
                s_ref[nb, h] = s_new
                o_ref[rows, h * vd:(h + 1) * vd] = _gated_rmsnorm(o, inputs[h][4], gain)

    if nbatch == 1:
        one_batch(0)
    else:
        def body(nb, carry):
            one_batch(nb)
            return carry
        lax.fori_loop(0, nbatch, body, 0, unroll=4)


def _recurrence(proj, s0, aux0, aux1, gain, *, mode, batch, seq, heads, kd, vd, lb_row=0):
    n, width = proj.shape
    chunk = min(REC_CHUNK, seq)
    tb = min(2 * chunk, seq)
    nbatch = 1 if seq > tb else min(8, batch)
    nblk = seq // tb
    grid = (batch // nbatch, nblk)
    state_spec = pl.BlockSpec((nbatch, heads, kd, vd), lambda b, c: (b, 0, 0, 0))
    full2 = lambda a: pl.BlockSpec(a.shape, lambda b, c: (0, 0))
    kern = functools.partial(_rec_kernel, mode=mode, heads=heads, kd=kd, vd=vd, chunk=chunk,
                             tb=tb, nbatch=nbatch, lb_row=lb_row)
    return pl.pallas_call(
        kern,
        out_shape=(jax.ShapeDtypeStruct((n, heads * vd), F32),
                   jax.ShapeDtypeStruct((batch, heads, kd, vd), F32)),
        grid=grid,
        in_specs=[pl.BlockSpec((nbatch * tb, width), lambda b, c: (b * nblk + c, 0)), state_spec,
                  full2(aux0), full2(aux1), full2(gain)],
        out_specs=(pl.BlockSpec((nbatch * tb, heads * vd), lambda b, c: (b * nblk + c, 0)),
                   state_spec),
        compiler_params=_params(("arbitrary", "arbitrary")),
        name="recurrence_" + mode,
    )(proj, s0, aux0, aux1, gain)


def _mixer_kernel(x_ref, w_in_ref, aux0_ref, aux1_ref, gain_ref, w_out_ref, g_ref, b_ref,
                  y_ref, s_ref, proj_scr, o_scr, *, mode, heads, kd, vd, chunk, lb_row, alpha):
    @pl.when(pl.program_id(1) == 0)
    def _():
        s_ref[...] = jnp.zeros(s_ref.shape, F32)

    tb = x_ref.shape[0]
    hk = heads * kd
    hv = heads * vd
    gain = gain_ref[...]
    x = x_ref[...]
    xb = x.astype(BF16)
    group = 256 // kd
    seg_starts = (0, hk, 2 * hk, 2 * hk + hv)
    seg_widths = (kd, kd, vd, vd)

    def project(gi):
        for start, w in zip(seg_starts, seg_widths):
            cols = slice(start + gi * group * w, start + (gi + 1) * group * w)
            proj_scr[:, cols] = _dot(xb, w_in_ref[:, cols])

    lb = None
    gk_blocks = None
    if mode == "hgrn":
        lb = _lower_bound(aux0_ref[...], lb_row)
    else:
        lr = _dot(xb, w_in_ref[:, 2 * hk + 2 * hv:2 * hk + 2 * hv + GLA_RANK_PAD])
        gk_full = _gla_gates(lr, aux0_ref[...], aux1_ref[...])
        gk_blocks = [gk_full[sc * chunk:(sc + 1) * chunk, :] for sc in range(tb // chunk)]

    n_groups = heads // group
    project(0)
    for gi in range(n_groups):
        if gi + 1 < n_groups:
            project(gi + 1)
        units = [(h, sc) for h in range(gi * group, (gi + 1) * group) for sc in range(tb // chunk)]
        inputs = [_head_inputs(mode, proj_scr, pl.ds(sc * chunk, chunk), h, heads, kd, vd, lb,
                               None if gk_blocks is None else gk_blocks[sc]) for h, sc in units]
        preps = [_chunk_prepare(q, k, v, g) for q, k, v, g, _ in inputs]
        local = [_chunk_local(p) for p in preps]
        for u, (h, sc) in enumerate(units):
            o, s_new = _chunk_finish(preps[u], local[u], s_ref[0, h])
            s_ref[0, h] = s_new
            o_scr[pl.ds(sc * chunk, chunk), h * vd:(h + 1) * vd] = (
                _gated_rmsnorm(o, inputs[u][4], gain).astype(BF16))
    c = _dot(o_scr[...], w_out_ref[...])
    y_ref[...] = _layer_norm(alpha * x + c, g_ref[...], b_ref[...])


def _mixer(x, w_in, aux0, aux1, gain, w_out, g, b, alpha, *, mode, batch, seq, heads, kd, vd, lb_row=0):
    n, d = x.shape
    width = w_in.shape[1]
    tb = 8 * REC_CHUNK
    nblk = seq // tb
    const = lambda a: pl.BlockSpec(a.shape, lambda bb, c: (0, 0))
    rows = pl.BlockSpec((tb, d), lambda bb, c: (bb * nblk + c, 0))
    kern = functools.partial(_mixer_kernel, mode=mode, heads=heads, kd=kd, vd=vd, chunk=REC_CHUNK,
                             lb_row=lb_row, alpha=alpha)
    return pl.pallas_call(
        kern,
        out_shape=(jax.ShapeDtypeStruct((n, d), F32),
                   jax.ShapeDtypeStruct((batch, heads, kd, vd), F32)),
        grid=(batch, nblk),
        in_specs=[rows, const(w_in), const(aux0), const(aux1), const(gain), const(w_out), const(g), const(b)],
        out_specs=(rows, pl.BlockSpec((1, heads, kd, vd), lambda bb, c: (bb, 0, 0, 0))),
        scratch_shapes=[pltpu.VMEM((tb, width), F32), pltpu.VMEM((tb, heads * vd), BF16)],
        compiler_params=_params(("arbitrary", "arbitrary")),
        name="mixer_" + mode,
    )(x, w_in, aux0, aux1, gain, w_out, g, b)


def _attn_kernel(q_ref, k_ref, v_ref, o_ref, *, heads, nbatch, tq, per_head):
    hd = q_ref.shape[1] // heads
    scale = hd ** -0.5
    units = [(nb, h) for nb in range(nbatch) for h in range(heads)]

    def head_rows(ref, nb, h):
        if not per_head:
            return ref[nb, :, h * hd:(h + 1) * hd].astype(BF16)
        nt = hd // 128
        period = heads * nt
        mlen = ref.shape[1] // period
        parts = [ref[nb, pl.ds(t * heads + h, mlen, stride=period), :] for t in range(nt)]
        return jnp.concatenate(parts, axis=1).astype(BF16)

    scores = [_dot_nt(q_ref[nb * tq:(nb + 1) * tq, h * hd:(h + 1) * hd].astype(BF16),
                      head_rows(k_ref, nb, h)) * scale for nb, h in units]
    probs = []
    for s in scores:
        p = jnp.exp(s - jnp.max(s, axis=-1, keepdims=True))
        probs.append((p / jnp.sum(p, axis=-1, keepdims=True)).astype(BF16))
    for (nb, h), p in zip(units, probs):
        o_ref[nb * tq:(nb + 1) * tq, h * hd:(h + 1) * hd] = _dot(p, head_rows(v_ref, nb, h)).astype(o_ref.dtype)


def _xattn_kernel(x_ref, wq_ref, k_ref, v_ref, wo_ref, g_ref, b_ref, rw_ref, rb_ref, y_ref, cls_ref, *,
                  heads, alpha):
    x = x_ref[...]
    hd = x.shape[1] // heads
    scale = hd ** -0.5
    q = _dot(x.astype(BF16), wq_ref[...]).astype(BF16)
    nt = hd // 128
    period = heads * nt
    mlen = k_ref.shape[0] // period

    def head_rows(ref, h):
        parts = [ref[pl.ds(t * heads + h, mlen, stride=period), :] for t in range(nt)]
        return jnp.concatenate(parts, axis=1).astype(BF16)

    scores = [_dot_nt(q[:, h * hd:(h + 1) * hd], head_rows(k_ref, h)) * scale for h in range(heads)]
    probs = []
    for s in scores:
        p = jnp.exp(s - jnp.max(s, axis=-1, keepdims=True))
        probs.append((p / jnp.sum(p, axis=-1, keepdims=True)).astype(BF16))
    o = jnp.concatenate([_dot(p, head_rows(v_ref, h)).astype(BF16) for h, p in enumerate(probs)], axis=1)
    y = _layer_norm(alpha * x + _dot(o, wo_ref[...]), g_ref[...], b_ref[...])
    y_ref[...] = y
    cls_ref[...] = _routing_class(y, rw_ref[...], rb_ref[...])


def _xattn(x, wq, mem_k, mem_v, wo, layer, g, b, router_wt, bias_col, alpha, *, batch, seq, heads, tq):
    n, d = x.shape
    nblk = seq // tq
    const = lambda a: pl.BlockSpec(a.shape, lambda bb, i: (0, 0))
    rows = pl.BlockSpec((tq, d), lambda bb, i: (bb * nblk + i, 0))
    mem = pl.BlockSpec((None, None) + mem_k.shape[2:], lambda bb, i: (layer, bb, 0, 0))
    return pl.pallas_call(
        functools.partial(_xattn_kernel, heads=heads, alpha=alpha),
        out_shape=(jax.ShapeDtypeStruct((n, d), F32), jax.ShapeDtypeStruct((1, n), I32)),
        grid=(batch, nblk),
        in_specs=[rows, const(wq), mem, mem, const(wo), const(g), const(b), const(router_wt), const(bias_col)],
        out_specs=(rows, pl.BlockSpec((1, tq), lambda bb, i: (0, bb * nblk + i))),
        compiler_params=_params(("arbitrary", "arbitrary")),
        name="xattn_block",
    )(x, wq, mem_k, mem_v, wo, g, b, router_wt, bias_col)


def _interleaved_rows(mem):
    nl, b, m, heads, hd = mem.shape
    nt = hd // 128
    return mem.reshape(nl, b, m, heads, nt, 128).transpose(0, 1, 2, 4, 3, 5).reshape(nl, b, m * nt * heads, 128)


def _attention(q, mem_k, mem_v, layer, *, batch, seq, heads):
    n, d = q.shape
    per_head = mem_k.shape[3] != d
    tq, nbatch = seq, 4
    nblk = 1
    mem_spec = pl.BlockSpec((None, nbatch) + mem_k.shape[2:], lambda b, i: (layer, b, 0, 0))
    qspec = pl.BlockSpec((nbatch * tq, d), lambda b, i: (b * nblk + i, 0))
    return pl.pallas_call(
        functools.partial(_attn_kernel, heads=heads, nbatch=nbatch, tq=tq, per_head=per_head),
        out_shape=jax.ShapeDtypeStruct((n, d), q.dtype),
        grid=(batch // nbatch, nblk),
        in_specs=[qspec, mem_spec, mem_spec],
        out_specs=qspec,
        compiler_params=_params(("arbitrary", "arbitrary")),
        name="mem_attention",
    )(q, mem_k, mem_v)


def _routing_class(x, wt, bias_col):
    logits = _dot_precise(wt, x, _dot_nt)
    sel = jax.nn.sigmoid(logits) + bias_col
    rows = [sel[e:e + 1, :] for e in range(N_GROUPS * GROUP_SIZE)]

    def first_argmax(vals):
        best_v, best_i = vals[0], jnp.zeros(vals[0].shape, I32)
        for i in range(1, len(vals)):
            better = vals[i] > best_v
            best_i = jnp.where(better, i, best_i)
            best_v = jnp.where(better, vals[i], best_v)
        return best_i

    group_scores = []
    for gi in range(N_GROUPS):
        a = rows[gi * GROUP_SIZE:(gi + 1) * GROUP_SIZE]
        top2 = None
        for i in range(GROUP_SIZE):
            for j in range(i + 1, GROUP_SIZE):
                s = a[i] + a[j]
                top2 = s if top2 is None else jnp.maximum(top2, s)
        group_scores.append(top2)
    best = first_argmax(group_scores)
    cand = []
    for j in range(GROUP_SIZE):
        cj = rows[j]
        for gi in range(1, N_GROUPS):
            cj = jnp.where(best == gi, rows[gi * GROUP_SIZE + j], cj)
        cand.append(cj)
    i1 = first_argmax(cand)
    i2 = first_argmax([jnp.where(i1 == j, -jnp.inf, cand[j]) for j in range(GROUP_SIZE)])
    lo = jnp.minimum(i1, i2)
    hi = jnp.maximum(i1, i2)
    pair = jnp.where(lo == 0, hi - 1, jnp.where(lo == 1, hi + 1, 5))
    return best * 6 + pair


def _router_kernel(x_ref, wt_ref, bias_ref, cls_ref):
    cls_ref[...] = _routing_class(x_ref[...], wt_ref[...], bias_ref[...])


def _router(x, router_wt, bias_col, tm):
    n, d = x.shape
    ne = router_wt.shape[0]
    return pl.pallas_call(
        _router_kernel,
        out_shape=jax.ShapeDtypeStruct((1, n), I32),
        grid=(n // tm,),
        in_specs=[pl.BlockSpec((tm, d), lambda i: (i, 0)),
                  pl.BlockSpec((ne, d), lambda i: (0, 0)),
                  pl.BlockSpec((ne, 1), lambda i: (0, 0))],
        out_specs=pl.BlockSpec((1, tm), lambda i: (0, i)),
        compiler_params=_params(("arbitrary",)),
        name="router",
    )(x, router_wt, bias_col)


_PAIR_LO = (0, 0, 0, 1, 1, 2)
_PAIR_HI = (1, 2, 3, 2, 3, 3)


def _moe_plan(cls, tile):
    n = cls.shape[0]
    ncls = N_GROUPS * 6
    max_tiles = n // tile + ncls
    shift = max(n - 1, 1).bit_length()
    keys = jnp.sort(cls * (1 << shift) + jnp.arange(n, dtype=I32))
    src = keys & ((1 << shift) - 1)
    cid = jnp.arange(ncls, dtype=I32)
    count = jnp.sum((cls[None, :] == cid[:, None]).astype(I32), axis=1)
    cstart = jnp.cumsum(count) - count
    ntile = (count + tile - 1) // tile
    tend = jnp.cumsum(ntile)
    tid = jnp.arange(max_tiles, dtype=I32)
    n_used = tend[-1]
    tcls = jnp.sum((tid[:, None] >= tend[None, :]).astype(I32), axis=1)
    last_cls = jnp.sum((n_used - 1 >= tend).astype(I32))
    tcls = jnp.where(tid < n_used, tcls, last_cls)
    within = tid - (tend - ntile)[tcls]
    tstart = cstart[tcls] + within * tile
    tcnt = jnp.clip(count[tcls] - within * tile, 0, tile)
    tcnt = jnp.where(tid < n_used, tcnt, 0)
    grp = tcls // 6
    e_lo = grp * GROUP_SIZE + jnp.asarray(_PAIR_LO, I32)[tcls % 6]
    e_hi = grp * GROUP_SIZE + jnp.asarray(_PAIR_HI, I32)[tcls % 6]
    return src, tstart, tcnt, e_lo, e_hi, n_used.reshape(1)


def _moe_kernel(src_ref, tstart_ref, tcnt_ref, elo_ref, ehi_ref, nused_ref,
                x_hbm, rw_lo_ref, rw_hi_ref, wg_lo_ref, wu_lo_ref, wd_lo_ref,
                wg_hi_ref, wu_hi_ref, wd_hi_ref, g_ref, b_ref,
                out_hbm, xbuf, ybuf, gsem, ssem, *, alpha):
    j = pl.program_id(0)
    n_used = nused_ref[0]
    parity = lax.rem(j, 2)
    ngroups, _, d = xbuf.shape[1:]
    block = 16

    def for_rows(cnt, fn):
        nblk = lax.shift_right_logical(cnt, 4)
        for blk in range(ngroups * 8 // block):
            @pl.when(blk < nblk)
            def _():
                for u in range(block):
                    r = blk * block + u
                    fn(r // 8, r % 8, r, u % 2)

        def single(r, c):
            fn(lax.shift_right_logical(r, 3), jnp.bitwise_and(r, 7), r, 0)
            return c

        lax.fori_loop(nblk * block, cnt, single, 0)

    def wait_rows(cnt, group_copy, row_copy):
        ngrp = lax.shift_right_logical(cnt, 3)

        @pl.when(ngrp > 0)
        def _():
            group_copy(ngrp).wait()

        def single(r, c):
            row_copy(lax.shift_right_logical(r, 3), jnp.bitwise_and(r, 7)).wait()
            return c
        lax.fori_loop(ngrp * 8, cnt, single, 0)

    def start_gather(t, s):
        start = tstart_ref[t]

        def fn(gi, u, r, prio):
            tok = src_ref[start + r]
            pltpu.make_async_copy(x_hbm.at[pl.ds(tok, 1)], xbuf.at[s, gi, pl.ds(u, 1)],
                                  gsem.at[s]).start(priority=prio)
        for_rows(tcnt_ref[t], fn)

    def wait_gather(t, s):
        wait_rows(tcnt_ref[t],
                  lambda n: pltpu.make_async_copy(xbuf.at[s, pl.ds(0, n)], xbuf.at[s, pl.ds(0, n)], gsem.at[s]),
                  lambda gi, u: pltpu.make_async_copy(xbuf.at[s, gi, pl.ds(u, 1)], xbuf.at[s, gi, pl.ds(u, 1)],
                                                      gsem.at[s]))

    def start_scatter(t, s):
        start = tstart_ref[t]

        def fn(gi, u, r, prio):
            tok = src_ref[start + r]
            pltpu.make_async_copy(ybuf.at[s, gi, pl.ds(u, 1)], out_hbm.at[pl.ds(tok, 1)],
                                  ssem.at[s]).start(priority=prio)
        for_rows(tcnt_ref[t], fn)

    def wait_scatter(t, s):
        wait_rows(tcnt_ref[t],
                  lambda n: pltpu.make_async_copy(ybuf.at[s, pl.ds(0, n)], ybuf.at[s, pl.ds(0, n)], ssem.at[s]),
                  lambda gi, u: pltpu.make_async_copy(ybuf.at[s, gi, pl.ds(u, 1)], ybuf.at[s, gi, pl.ds(u, 1)],
                                                      ssem.at[s]))

    @pl.when(j == 0)
    def _():
        xbuf[...] = jnp.zeros(xbuf.shape, F32)
        start_gather(0, 0)

    def step(slot):
        @pl.when(j + 1 < n_used)
        def _():
            start_gather(j + 1, 1 - slot)

        wait_gather(j, slot)

        @pl.when(j >= 2)
        def _():
            wait_scatter(j - 2, slot)

        x = xbuf[slot].reshape(ngroups * 8, d)
        xb = x.astype(BF16)

        def expert(rw_ref, wg_ref, wu_ref, wd_ref):
            h = _dot(xb, wg_ref[...].astype(BF16))
            h = h * _sigmoid(h) * _dot(xb, wu_ref[...].astype(BF16))
            y = _dot(h.astype(BF16), wd_ref[...].astype(BF16))
            score = _sigmoid(jnp.sum(x * rw_ref[...], axis=-1, keepdims=True))
            return y, score

        y_lo, s_lo = expert(rw_lo_ref, wg_lo_ref, wu_lo_ref, wd_lo_ref)
        y_hi, s_hi = expert(rw_hi_ref, wg_hi_ref, wu_hi_ref, wd_hi_ref)
        tot = s_lo + s_hi
        m = (s_lo / tot) * y_lo + (s_hi / tot) * y_hi
        ybuf[slot] = _layer_norm(alpha * x + m, g_ref[...], b_ref[...]).reshape(ngroups, 8, d)
        start_scatter(j, slot)

        @pl.when(j == n_used - 1)
        def _():
            @pl.when(j >= 1)
            def _():
                wait_scatter(j - 1, 1 - slot)
            wait_scatter(j, slot)

    for s in range(2):
        pl.when(jnp.logical_and(j < n_used, parity == s))(functools.partial(step, s))


def _moe(x, cls, router_wt3, wg, wu, wd, layer, g, b, alpha):
    n, d = x.shape
    de = wg.shape[3]
    tile = MOE_TILE if n >= 32 * MOE_TILE else MOE_TILE // 4
    src, tstart, tcnt, e_lo, e_hi, n_used = _moe_plan(cls, tile)
    max_tiles = tstart.shape[0]

    def by_lo(shape):
        return pl.BlockSpec(shape, lambda j, s, ts, tc, el, eh, nu: (el[j], 0, 0))

    def by_hi(shape):
        return pl.BlockSpec(shape, lambda j, s, ts, tc, el, eh, nu: (eh[j], 0, 0))

    def w_lo(r, c):
        return pl.BlockSpec((None, None, r, c), lambda j, s, ts, tc, el, eh, nu: (layer, el[j], 0, 0))

    def w_hi(r, c):
        return pl.BlockSpec((None, None, r, c), lambda j, s, ts, tc, el, eh, nu: (layer, eh[j], 0, 0))

    row = pl.BlockSpec((1, d), lambda j, s, ts, tc, el, eh, nu: (0, 0))
    grid_spec = pltpu.PrefetchScalarGridSpec(
        num_scalar_prefetch=6,
        grid=(max_tiles,),
        in_specs=[pl.BlockSpec(memory_space=pl.ANY),
                  by_lo((None, 1, d)), by_hi((None, 1, d)),
                  w_lo(d, de), w_lo(d, de), w_lo(de, d),
                  w_hi(d, de), w_hi(d, de), w_hi(de, d),
                  row, row],
        out_specs=pl.BlockSpec(memory_space=pl.ANY),
        scratch_shapes=[pltpu.VMEM((2, tile // 8, 8, d), F32), pltpu.VMEM((2, tile // 8, 8, d), F32),
                        pltpu.SemaphoreType.DMA((2,)), pltpu.SemaphoreType.DMA((2,))],
    )
    return pl.pallas_call(
        functools.partial(_moe_kernel, alpha=alpha),
        out_shape=jax.ShapeDtypeStruct((n, d), F32),
        grid_spec=grid_spec,
        compiler_params=_params(("arbitrary",)),
        name="grouped_moe",
    )(src, tstart, tcnt, e_lo, e_hi, n_used,
      x, router_wt3, router_wt3, wg, wu, wd, wg, wu, wd, g, b)


def kernel(x_prompt, x_sample, state_hgrn, state_gla, cache_mem_k, cache_mem_v, mem_prompt, hgrn_w_in, hgrn_lb_logits, hgrn_norm_g, hgrn_w_out, gla_w_in, gla_w_gk2, gla_b_gk2, gla_norm_g, gla_w_out, xattn_w_q, xattn_w_kv, xattn_w_o, router_w, router_bias, moe_w_gate, moe_w_up, moe_w_down, ln_g, ln_b):
    batch, seq, d = x_prompt.shape
    dec_batch, dec_seq, _ = x_sample.shape
    depth = ln_g.shape[0]
    alpha = (2 * depth) ** 0.25
    a_heads, a_key, a_val = state_hgrn.shape[2:]
    b_heads, b_key, b_val = state_gla.shape[2:]
    mem_len, x_heads = cache_mem_k.shape[2], cache_mem_k.shape[3]
    rank = gla_w_gk2.shape[1]
    b_main = 2 * b_heads * b_key + b_heads * b_val + d

    hgrn_w_in_b = hgrn_w_in.astype(BF16)
    hgrn_w_out_b = hgrn_w_out.astype(BF16)
    gla_w_in_b = jnp.concatenate(
        [gla_w_in, jnp.zeros(gla_w_in.shape[:2] + (GLA_RANK_PAD - rank,), F32)], axis=-1).astype(BF16)
    gla_w_gk2_p = jnp.concatenate(
        [gla_w_gk2, jnp.zeros((gla_w_gk2.shape[0], GLA_RANK_PAD - rank, gla_w_gk2.shape[2]), F32)], axis=1)
    gla_w_out_b = gla_w_out.astype(BF16)
    w_q_b = xattn_w_q.astype(BF16)
    w_k_b = xattn_w_kv[:, :, :d].astype(BF16)
    w_v_b = xattn_w_kv[:, :, d:].astype(BF16)
    w_o_b = xattn_w_o.astype(BF16)
    wg_b, wu_b, wd_b = moe_w_gate, moe_w_up, moe_w_down
    router_wt = router_w.T
    router_wt3 = router_wt[:, None, :]
    bias_col = router_bias[:, None]
    assert b_main + rank == gla_w_in.shape[2]

    mem_k_p, mem_v_p = _kv_proj(mem_prompt.reshape(batch * mem_len, d), w_k_b, w_v_b, 512, x_heads)
    rows_per_seq = mem_len * d // 128
    mem_k_p = mem_k_p.reshape(depth, batch, rows_per_seq, 128)
    mem_v_p = mem_v_p.reshape(depth, batch, rows_per_seq, 128)

    def head_major(mem):
        nt = d // x_heads // 128
        return mem.reshape(depth, batch, mem_len, nt, x_heads, 128).transpose(0, 1, 2, 4, 3, 5).reshape(
            depth, batch, mem_len, x_heads, d // x_heads)
    mem_k_prompt = head_major(mem_k_p)
    mem_v_prompt = head_major(mem_v_p)

    def run_trunk(x3, states_a, states_b, mem_ks, mem_vs):
        nb, ns, _ = x3.shape
        x = x3.reshape(nb * ns, d)
        new_a, new_b = [], []
        for l in range(depth):
            j = l // 2
            row = lambda a, i: a[l, i][None, :]
            if l % 2 == 0:
                w_in, w_out, states, new = hgrn_w_in_b[j], hgrn_w_out_b[j], states_a, new_a
                aux = (hgrn_lb_logits, hgrn_lb_logits[:1], hgrn_norm_g[j][None, :])
                cfg = dict(mode="hgrn", batch=nb, seq=ns, heads=a_heads, kd=a_key, vd=a_val, lb_row=l)
            else:
                w_in, w_out, states, new = gla_w_in_b[j], gla_w_out_b[j], states_b, new_b
                aux = (gla_w_gk2_p[j], gla_b_gk2[j][None, :], gla_norm_g[j][None, :])
                cfg = dict(mode="gla", batch=nb, seq=ns, heads=b_heads, kd=b_key, vd=b_val)
            if states is None:
                x, s = _mixer(x, w_in, *aux, w_out, row(ln_g, 0), row(ln_b, 0), alpha, **cfg)
            else:
                proj = _linear(x, w_in, F32, 512)
                o, s = _recurrence(proj, states[j], *aux, **cfg)
                x = _linear_res_ln(o, w_out, x, row(ln_g, 0), row(ln_b, 0), alpha, 512)
            new.append(s)
            if ns >= 512:
                x, cls = _xattn(x, w_q_b[l], mem_ks, mem_vs, w_o_b[l], l, row(ln_g, 1), row(ln_b, 1),
                                router_wt, bias_col, alpha, batch=nb, seq=ns, heads=x_heads, tq=512)
            else:
                q = _linear(x, w_q_b[l], F32, 512)
                c = _attention(q, mem_ks, mem_vs, l, batch=nb, seq=ns, heads=x_heads)
                x = _linear_res_ln(c, w_o_b[l], x, row(ln_g, 1), row(ln_b, 1), alpha, 512)
                cls = _router(x, router_wt, bias_col, 512)
            x = _moe(x, cls[0], router_wt3, wg_b, wu_b, wd_b, l, row(ln_g, 2), row(ln_b, 2), alpha)
        return x.reshape(nb, ns, d), jnp.stack(new_a), jnp.stack(new_b)

    y_prompt, state_hgrn_prompt, state_gla_prompt = run_trunk(x_prompt, None, None, mem_k_p, mem_v_p)
    y_sample, state_hgrn_sample, state_gla_sample = run_trunk(
        x_sample, state_hgrn, state_gla, _interleaved_rows(cache_mem_k), _interleaved_rows(cache_mem_v))
    return (y_prompt, y_sample, state_hgrn_prompt, state_gla_prompt, mem_k_prompt, mem_v_prompt,
            state_hgrn_sample, state_gla_sample)
```

```python
import functools

import jax
import jax.numpy as jnp
from jax import lax
from jax.experimental import pallas as pl
from jax.experimental.pallas import tpu as pltpu

F32 = jnp.float32
BF16 = jnp.bfloat16
I32 = jnp.int32

N_GROUPS = 4
GROUP_SIZE = 4
GLA_GATE_NORMALIZER = 16.0
LN_EPS = 1e-5
RMS_EPS = 1e-6
GLA_RANK_PAD = 128

VMEM_LIMIT_BYTES = 56 * 1024 * 1024
REC_CHUNK = 64
MOE_TILE = 256

def _params(sem):
    return pltpu.CompilerParams(dimension_semantics=sem, vmem_limit_bytes=VMEM_LIMIT_BYTES)


def _dot(a, b):
    return jnp.dot(a, b, preferred_element_type=F32)


def _dot_nt(a, b):
    return lax.dot_general(a, b, (((1,), (1,)), ((), ())), preferred_element_type=F32)


def _dot_precise(a, b, dot=_dot):
    a_hi = a.astype(BF16)
    b_hi = b.astype(BF16)
    a_lo = (a - a_hi.astype(F32)).astype(BF16)
    b_lo = (b - b_hi.astype(F32)).astype(BF16)
    return dot(a_hi, b_hi) + dot(a_hi, b_lo) + dot(a_lo, b_hi)


def _dot_tn(a, b):
    return lax.dot_general(a, b, (((0,), (0,)), ((), ())), preferred_element_type=F32)


def _layer_norm(z, g, b):
    mu = jnp.mean(z, axis=-1, keepdims=True)
    zc = z - mu
    var = jnp.mean(zc * zc, axis=-1, keepdims=True)
    return zc * lax.rsqrt(var + LN_EPS) * g + b


def _sigmoid(x):
    return 0.5 * jnp.tanh(0.5 * x) + 0.5


def _log_sigmoid(x):
    return jnp.minimum(x, 0.0) - jnp.log(1.0 + jnp.exp(-jnp.abs(x)))


def _linear_kernel(x_ref, w_ref, o_ref):
    o_ref[...] = _dot(x_ref[...].astype(BF16), w_ref[...]).astype(o_ref.dtype)


def _linear(x, w, out_dtype, tm):
    m, k = x.shape
    n = w.shape[1]
    return pl.pallas_call(
        _linear_kernel,
        out_shape=jax.ShapeDtypeStruct((m, n), out_dtype),
        grid=(m // tm,),
        in_specs=[pl.BlockSpec((tm, k), lambda i: (i, 0)),
                  pl.BlockSpec((k, n), lambda i: (0, 0))],
        out_specs=pl.BlockSpec((tm, n), lambda i: (i, 0)),
        compiler_params=_params(("arbitrary",)),
        name="linear",
    )(x, w)


def _kv_proj_kernel(x_ref, wk_ref, wv_ref, k_ref, v_ref, *, heads):
    x = x_ref[...].astype(BF16)
    tm, d = x.shape
    hd = d // heads
    nt = hd // 128
    for w_ref, o_ref in ((wk_ref, k_ref), (wv_ref, v_ref)):
        y = _dot(x, w_ref[...])
        for h in range(heads):
            for t in range(nt):
                o_ref[pl.ds(t * heads + h, tm, stride=heads * nt), :] = y[:, h * hd + t * 128:h * hd + (t + 1) * 128]


def _kv_proj(mem, wk, wv, tm, heads):
    r, d = mem.shape
    nl = wk.shape[0]
    out = jax.ShapeDtypeStruct((nl, r * d // 128, 128), F32)
    wspec = pl.BlockSpec((None, d, d), lambda l, i: (l, 0, 0))
    ospec = pl.BlockSpec((None, tm * d // 128, 128), lambda l, i: (l, i, 0))
    return pl.pallas_call(
        functools.partial(_kv_proj_kernel, heads=heads),
        out_shape=(out, out),
        grid=(nl, r // tm),
        in_specs=[pl.BlockSpec((tm, d), lambda l, i: (i, 0)), wspec, wspec],
        out_specs=(ospec, ospec),
        compiler_params=_params(("arbitrary", "arbitrary")),
        name="kv_proj",
    )(mem, wk, wv)


def _linear_res_ln_kernel(h_ref, w_ref, x_ref, g_ref, b_ref, o_ref, *, alpha):
    c = _dot(h_ref[...].astype(BF16), w_ref[...])
    o_ref[...] = _layer_norm(alpha * x_ref[...] + c, g_ref[...], b_ref[...])


def _linear_res_ln(h, w, x, g, b, alpha, tm):
    m, k = h.shape
    d = w.shape[1]
    row = pl.BlockSpec((1, d), lambda i: (0, 0))
    return pl.pallas_call(
        functools.partial(_linear_res_ln_kernel, alpha=alpha),
        out_shape=jax.ShapeDtypeStruct((m, d), F32),
        grid=(m // tm,),
        in_specs=[pl.BlockSpec((tm, k), lambda i: (i, 0)),
                  pl.BlockSpec((k, d), lambda i: (0, 0)),
                  pl.BlockSpec((tm, d), lambda i: (i, 0)), row, row],
        out_specs=pl.BlockSpec((tm, d), lambda i: (i, 0)),
        compiler_params=_params(("arbitrary",)),
        name="linear_res_ln",
    )(h, w, x, g, b)


def _cumsum_rows(x):
    n = x.shape[0]
    row = lax.broadcasted_iota(I32, x.shape, 0)
    s = 1
    while s < n:
        x = x + jnp.where(row >= s, pltpu.roll(x, s, 0), 0.0)
        s *= 2
    return x


def _chunk_prepare(q, k, v, g):
    c, kd = q.shape
    vd = v.shape[1]
    b = _cumsum_rows(g)
    b_last = b[c - 1:c, :]
    b_mid = b[c // 2:c // 2 + 1, :]
    qa = q * jnp.exp(b - b_mid)
    ka = k * jnp.exp(b_mid - b)
    scores = _dot_nt(qa.astype(BF16), ka.astype(BF16))
    ri = lax.broadcasted_iota(I32, (c, c), 0)
    ci = lax.broadcasted_iota(I32, (c, c), 1)
    scores = jnp.where(ri >= ci, scores, 0.0).astype(BF16)
    decay = jnp.broadcast_to(jnp.exp(b_last), (kd, kd)).T
    if vd != kd:
        decay = jnp.concatenate([decay] * (vd // kd), axis=1)
    return ((qa * jnp.exp(b_mid)).astype(BF16), scores,
            (ka * jnp.exp(b_last - b_mid)).astype(BF16), v.astype(BF16), decay)


def _chunk_local(prep):
    _, scores, ks, vb, _ = prep
    return _dot(scores, vb), _dot_tn(ks, vb)


def _chunk_finish(prep, local, state):
    q_in, _, _, _, decay = prep
    o_local, increment = local
    return _dot(q_in, state.astype(BF16)) + o_local, state * decay + increment


def _gated_rmsnorm(o, gate, gain):
    o = o * lax.rsqrt(jnp.mean(o * o, axis=-1, keepdims=True) + RMS_EPS)
    return o * (0.5 * gain) * (gate * (jnp.tanh(0.5 * gate) + 1.0))


def _lower_bound(logits, lb_row):
    e = jnp.exp(logits - jnp.max(logits, axis=0, keepdims=True))
    return jnp.sum(e[:lb_row + 1], axis=0, keepdims=True) / jnp.sum(e, axis=0, keepdims=True)


def _gla_gates(lr, w_gk2, b_gk2):
    z = _dot_precise(lr, w_gk2) + b_gk2
    return _log_sigmoid(z) * (1.0 / GLA_GATE_NORMALIZER)


def _head_inputs(mode, proj_ref, rows, h, heads, kd, vd, lb, gk_all):
    hk = heads * kd
    hv = heads * vd
    v = proj_ref[rows, 2 * hk + h * vd:2 * hk + (h + 1) * vd]
    gate = proj_ref[rows, 2 * hk + hv + h * vd:2 * hk + hv + (h + 1) * vd]
    if mode == "hgrn":
        q = proj_ref[rows, h * kd:(h + 1) * kd]
        f = proj_ref[rows, hk + h * kd:hk + (h + 1) * kd]
        lbh = lb[:, h * kd:(h + 1) * kd]
        q = (0.5 * kd ** -0.5) * q * (jnp.tanh(0.5 * q) + 1.0)
        fg = 0.5 * (1.0 + lbh) + (0.5 * (1.0 - lbh)) * jnp.tanh(0.5 * f)
        return q, 1.0 - fg, v, jnp.log(fg), gate
    q = proj_ref[rows, h * kd:(h + 1) * kd] * (kd ** -0.5)
    k = proj_ref[rows, hk + h * kd:hk + (h + 1) * kd]
    return q, k, v, gk_all[:, h * kd:(h + 1) * kd], gate


def _rec_kernel(proj_ref, s0_ref, aux0_ref, aux1_ref, gain_ref, o_ref, s_ref, *,
                mode, heads, kd, vd, chunk, tb, nbatch, lb_row):
    @pl.when(pl.program_id(1) == 0)
    def _():
        s_ref[...] = s0_ref[...]

    hk = heads * kd
    hv = heads * vd
    gain = gain_ref[...]
    lb = _lower_bound(aux0_ref[...], lb_row) if mode == "hgrn" else None

    def one_batch(nb):
        for sc in range(tb // chunk):
            r0 = nb * tb + sc * chunk
            rows = pl.ds(r0, chunk) if isinstance(r0, int) else pl.ds(pl.multiple_of(r0, 8), chunk)
            gk_all = None
            if mode == "gla":
                lr = proj_ref[rows, 2 * hk + 2 * hv:2 * hk + 2 * hv + GLA_RANK_PAD]
                gk_all = _gla_gates(lr, aux0_ref[...], aux1_ref[...])
            inputs = [_head_inputs(mode, proj_ref, rows, h, heads, kd, vd, lb, gk_all) for h in range(heads)]
            preps = [_chunk_prepare(q, k, v, g) for q, k, v, g, _ in inputs]
            local = [_chunk_local(p) for p in preps]
            for h in range(heads):
                o, s_new = _chunk_finish(preps[h], local[h], s_ref[nb, h])
                s_ref[nb, h] = s_new
                o_ref[rows, h * vd:(h + 1) * vd] = _gated_rmsnorm(o, inputs[h][4], gain)

    if nbatch == 1:
        one_batch(0)
    else:
        def body(nb, carry):
            one_batch(nb)
            return carry
        lax.fori_loop(0, nbatch, body, 0, unroll=4)


def _recurrence(proj, s0, aux0, aux1, gain, *, mode, batch, seq, heads, kd, vd, lb_row=0):
    n, width = proj.shape
    chunk = min(REC_CHUNK, seq)
    tb = min(2 * chunk, seq)
    nbatch = 1 if seq > tb else min(8, batch)
    nblk = seq // tb
    grid = (batch // nbatch, nblk)
    state_spec = pl.BlockSpec((nbatch, heads, kd, vd), lambda b, c: (b, 0, 0, 0))
    full2 = lambda a: pl.BlockSpec(a.shape, lambda b, c: (0, 0))
    kern = functools.partial(_rec_kernel, mode=mode, heads=heads, kd=kd, vd=vd, chunk=chunk,
                             tb=tb, nbatch=nbatch, lb_row=lb_row)
    return pl.pallas_call(
        kern,
        out_shape=(jax.ShapeDtypeStruct((n, heads * vd), F32),
                   jax.ShapeDtypeStruct((batch, heads, kd, vd), F32)),
        grid=grid,
        in_specs=[pl.BlockSpec((nbatch * tb, width), lambda b, c: (b * nblk + c, 0)), state_spec,
                  full2(aux0), full2(aux1), full2(gain)],
        out_specs=(pl.BlockSpec((nbatch * tb, heads * vd), lambda b, c: (b * nblk + c, 0)),
                   state_spec),
        compiler_params=_params(("arbitrary", "arbitrary")),
        name="recurrence_" + mode,
    )(proj, s0, aux0, aux1, gain)


def _mixer_kernel(x_ref, w_in_ref, aux0_ref, aux1_ref, gain_ref, w_out_ref, g_ref, b_ref,
                  y_ref, s_ref, proj_scr, o_scr, *, mode, heads, kd, vd, chunk, lb_row, alpha):
    @pl.when(pl.program_id(1) == 0)
    def _():
        s_ref[...] = jnp.zeros(s_ref.shape, F32)

    tb = x_ref.shape[0]
    hk = heads * kd
    hv = heads * vd
    gain = gain_ref[...]
    x = x_ref[...]
    xb = x.astype(BF16)
    group = 256 // kd
    seg_starts = (0, hk, 2 * hk, 2 * hk + hv)
    seg_widths = (kd, kd, vd, vd)

    def project(gi):
        for start, w in zip(seg_starts, seg_widths):
            cols = slice(start + gi * group * w, start + (gi + 1) * group * w)
            proj_scr[:, cols] = _dot(xb, w_in_ref[:, cols])

    lb = None
    gk_blocks = None
    if mode == "hgrn":
        lb = _lower_bound(aux0_ref[...], lb_row)
    else:
        lr = _dot(xb, w_in_ref[:, 2 * hk + 2 * hv:2 * hk + 2 * hv + GLA_RANK_PAD])
        gk_full = _gla_gates(lr, aux0_ref[...], aux1_ref[...])
        gk_blocks = [gk_full[sc * chunk:(sc + 1) * chunk, :] for sc in range(tb // chunk)]

    n_groups = heads // group
    project(0)
    for gi in range(n_groups):
        if gi + 1 < n_groups:
            project(gi + 1)
        units = [(h, sc) for h in range(gi * group, (gi + 1) * group) for sc in range(tb // chunk)]
        inputs = [_head_inputs(mode, proj_scr, pl.ds(sc * chunk, chunk), h, heads, kd, vd, lb,
                               None if gk_blocks is None else gk_blocks[sc]) for h, sc in units]
        preps = [_chunk_prepare(q, k, v, g) for q, k, v, g, _ in inputs]
        local = [_chunk_local(p) for p in preps]
        for u, (h, sc) in enumerate(units):
            o, s_new = _chunk_finish(preps[u], local[u], s_ref[0, h])
            s_ref[0, h] = s_new
            o_scr[pl.ds(sc * chunk, chunk), h * vd:(h + 1) * vd] = (
                _gated_rmsnorm(o, inputs[u][4], gain).astype(BF16))
    c = _dot(o_scr[...], w_out_ref[...])
    y_ref[...] = _layer_norm(alpha * x + c, g_ref[...], b_ref[...])


def _mixer(x, w_in, aux0, aux1, gain, w_out, g, b, alpha, *, mode, batch, seq, heads, kd, vd, lb_row=0):
    n, d = x.shape
    width = w_in.shape[1]
    tb = 8 * REC_CHUNK
    nblk = seq // tb
    const = lambda a: pl.BlockSpec(a.shape, lambda bb, c: (0, 0))
    rows = pl.BlockSpec((tb, d), lambda bb, c: (bb * nblk + c, 0))
    kern = functools.partial(_mixer_kernel, mode=mode, heads=heads, kd=kd, vd=vd, chunk=REC_CHUNK,
                             lb_row=lb_row, alpha=alpha)
    return pl.pallas_call(
        kern,
        out_shape=(jax.ShapeDtypeStruct((n, d), F32),
                   jax.ShapeDtypeStruct((batch, heads, kd, vd), F32)),
        grid=(batch, nblk),
        in_specs=[rows, const(w_in), const(aux0), const(aux1), const(gain), const(w_out), const(g), const(b)],
        out_specs=(rows, pl.BlockSpec((1, heads, kd, vd), lambda bb, c: (bb, 0, 0, 0))),
        scratch_shapes=[pltpu.VMEM((tb, width), F32), pltpu.VMEM((tb, heads * vd), BF16)],
        compiler_params=_params(("arbitrary", "arbitrary")),
        name="mixer_" + mode,
    )(x, w_in, aux0, aux1, gain, w_out, g, b)


def _attn_kernel(q_ref, k_ref, v_ref, o_ref, *, heads, nbatch, tq, per_head):
    hd = q_ref.shape[1] // heads
    scale = hd ** -0.5
    units = [(nb, h) for nb in range(nbatch) for h in range(heads)]

    def head_rows(ref, nb, h):
        if not per_head:
            return ref[nb, :, h * hd:(h + 1) * hd].astype(BF16)
        nt = hd // 128
        period = heads * nt
        mlen = ref.shape[1] // period
        parts = [ref[nb, pl.ds(t * heads + h, mlen, stride=period), :] for t in range(nt)]
        return jnp.concatenate(parts, axis=1).astype(BF16)

    scores = [_dot_nt(q_ref[nb * tq:(nb + 1) * tq, h * hd:(h + 1) * hd].astype(BF16),
                      head_rows(k_ref, nb, h)) * scale for nb, h in units]
    probs = []
    for s in scores:
        p = jnp.exp(s - jnp.max(s, axis=-1, keepdims=True))
        probs.append((p / jnp.sum(p, axis=-1, keepdims=True)).astype(BF16))
    for (nb, h), p in zip(units, probs):
        o_ref[nb * tq:(nb + 1) * tq, h * hd:(h + 1) * hd] = _dot(p, head_rows(v_ref, nb, h)).astype(o_ref.dtype)


def _xattn_kernel(x_ref, wq_ref, k_ref, v_ref, wo_ref, g_ref, b_ref, rw_ref, rb_ref, y_ref, cls_ref, *,
                  heads, alpha):
    x = x_ref[...]
    hd = x.shape[1] // heads
    scale = hd ** -0.5
    q = _dot(x.astype(BF16), wq_ref[...]).astype(BF16)
    nt = hd // 128
    period = heads * nt
    mlen = k_ref.shape[0] // period

    def head_rows(ref, h):
        parts = [ref[pl.ds(t * heads + h, mlen, stride=period), :] for t in range(nt)]
        return jnp.concatenate(parts, axis=1).astype(BF16)

    scores = [_dot_nt(q[:, h * hd:(h + 1) * hd], head_rows(k_ref, h)) * scale for h in range(heads)]
    probs = []
    for s in scores:
        p = jnp.exp(s - jnp.max(s, axis=-1, keepdims=True))
        probs.append((p / jnp.sum(p, axis=-1, keepdims=True)).astype(BF16))
    o = jnp.concatenate([_dot(p, head_rows(v_ref, h)).astype(BF16) for h, p in enumerate(probs)], axis=1)
    y = _layer_norm(alpha * x + _dot(o, wo_ref[...]), g_ref[...], b_ref[...])
    y_ref[...] = y
    cls_ref[...] = _routing_class(y, rw_ref[...], rb_ref[...])


def _xattn(x, wq, mem_k, mem_v, wo, layer, g, b, router_wt, bias_col, alpha, *, batch, seq, heads, tq):
    n, d = x.shape
    nblk = seq // tq
    const = lambda a: pl.BlockSpec(a.shape, lambda bb, i: (0, 0))
    rows = pl.BlockSpec((tq, d), lambda bb, i: (bb * nblk + i, 0))
    mem = pl.BlockSpec((None, None) + mem_k.shape[2:], lambda bb, i: (layer, bb, 0, 0))
    return pl.pallas_call(
        functools.partial(_xattn_kernel, heads=heads, alpha=alpha),
        out_shape=(jax.ShapeDtypeStruct((n, d), F32), jax.ShapeDtypeStruct((1, n), I32)),
        grid=(batch, nblk),
        in_specs=[rows, const(wq), mem, mem, const(wo), const(g), const(b), const(router_wt), const(bias_col)],
        out_specs=(rows, pl.BlockSpec((1, tq), lambda bb, i: (0, bb * nblk + i))),
        compiler_params=_params(("arbitrary", "arbitrary")),
        name="xattn_block",
    )(x, wq, mem_k, mem_v, wo, g, b, router_wt, bias_col)


def _interleaved_rows(mem):
    nl, b, m, heads, hd = mem.shape
    nt = hd // 128
    return mem.reshape(nl, b, m, heads, nt, 128).transpose(0, 1, 2, 4, 3, 5).reshape(nl, b, m * nt * heads, 128)


def _attention(q, mem_k, mem_v, layer, *, batch, seq, heads):
    n, d = q.shape
    per_head = mem_k.shape[3] != d
    tq, nbatch = seq, 4
    nblk = 1
    mem_spec = pl.BlockSpec((None, nbatch) + mem_k.shape[2:], lambda b, i: (layer, b, 0, 0))
    qspec = pl.BlockSpec((nbatch * tq, d), lambda b, i: (b * nblk + i, 0))
    return pl.pallas_call(
        functools.partial(_attn_kernel, heads=heads, nbatch=nbatch, tq=tq, per_head=per_head),
        out_shape=jax.ShapeDtypeStruct((n, d), q.dtype),
        grid=(batch // nbatch, nblk),
        in_specs=[qspec, mem_spec, mem_spec],
        out_specs=qspec,
        compiler_params=_params(("arbitrary", "arbitrary")),
        name="mem_attention",
    )(q, mem_k, mem_v)


def _routing_class(x, wt, bias_col):
    logits = _dot_precise(wt, x, _dot_nt)
    sel = jax.nn.sigmoid(logits) + bias_col
    rows = [sel[e:e + 1, :] for e in range(N_GROUPS * GROUP_SIZE)]

    def first_argmax(vals):
        best_v, best_i = vals[0], jnp.zeros(vals[0].shape, I32)
        for i in range(1, len(vals)):
            better = vals[i] > best_v
            best_i = jnp.where(better, i, best_i)
            best_v = jnp.where(better, vals[i], best_v)
        return best_i

    group_scores = []
    for gi in range(N_GROUPS):
        a = rows[gi * GROUP_SIZE:(gi + 1) * GROUP_SIZE]
        top2 = None
        for i in range(GROUP_SIZE):
            for j in range(i + 1, GROUP_SIZE):
                s = a[i] + a[j]
                top2 = s if top2 is None else jnp.maximum(top2, s)
        group_scores.append(top2)
    best = first_argmax(group_scores)
    cand = []
    for j in range(GROUP_SIZE):
        cj = rows[j]
        for gi in range(1, N_GROUPS):
            cj = jnp.where(best == gi, rows[gi * GROUP_SIZE + j], cj)
        cand.append(cj)
    i1 = first_argmax(cand)
    i2 = first_argmax([jnp.where(i1 == j, -jnp.inf, cand[j]) for j in range(GROUP_SIZE)])
    lo = jnp.minimum(i1, i2)
    hi = jnp.maximum(i1, i2)
    pair = jnp.where(lo == 0, hi - 1, jnp.where(lo == 1, hi + 1, 5))
    return best * 6 + pair


def _router_kernel(x_ref, wt_ref, bias_ref, cls_ref):
    cls_ref[...] = _routing_class(x_ref[...], wt_ref[...], bias_ref[...])


def _router(x, router_wt, bias_col, tm):
    n, d = x.shape
    ne = router_wt.shape[0]
    return pl.pallas_call(
        _router_kernel,
        out_shape=jax.ShapeDtypeStruct((1, n), I32),
        grid=(n // tm,),
        in_specs=[pl.BlockSpec((tm, d), lambda i: (i, 0)),
                  pl.BlockSpec((ne, d), lambda i: (0, 0)),
                  pl.BlockSpec((ne, 1), lambda i: (0, 0))],
        out_specs=pl.BlockSpec((1, tm), lambda i: (0, i)),
        compiler_params=_params(("arbitrary",)),
        name="router",
    )(x, router_wt, bias_col)


_PAIR_LO = (0, 0, 0, 1, 1, 2)
_PAIR_HI = (1, 2, 3, 2, 3, 3)


def _moe_plan(cls, tile):
    n = cls.shape[0]
    ncls = N_GROUPS * 6
    max_tiles = n // tile + ncls
    shift = max(n - 1, 1).bit_length()
    keys = jnp.sort(cls * (1 << shift) + jnp.arange(n, dtype=I32))
    src = keys & ((1 << shift) - 1)
    cid = jnp.arange(ncls, dtype=I32)
    count = jnp.sum((cls[None, :] == cid[:, None]).astype(I32), axis=1)
    cstart = jnp.cumsum(count) - count
    ntile = (count + tile - 1) // tile
    tend = jnp.cumsum(ntile)
    tid = jnp.arange(max_tiles, dtype=I32)
    n_used = tend[-1]
    tcls = jnp.sum((tid[:, None] >= tend[None, :]).astype(I32), axis=1)
    last_cls = jnp.sum((n_used - 1 >= tend).astype(I32))
    tcls = jnp.where(tid < n_used, tcls, last_cls)
    within = tid - (tend - ntile)[tcls]
    tstart = cstart[tcls] + within * tile
    tcnt = jnp.clip(count[tcls] - within * tile, 0, tile)
    tcnt = jnp.where(tid < n_used, tcnt, 0)
    grp = tcls // 6
    e_lo = grp * GROUP_SIZE + jnp.asarray(_PAIR_LO, I32)[tcls % 6]
    e_hi = grp * GROUP_SIZE + jnp.asarray(_PAIR_HI, I32)[tcls % 6]
    return src, tstart, tcnt, e_lo, e_hi, n_used.reshape(1)


def _moe_kernel(src_ref, tstart_ref, tcnt_ref, elo_ref, ehi_ref, nused_ref,
                x_hbm, rw_lo_ref, rw_hi_ref, wg_lo_ref, wu_lo_ref, wd_lo_ref,
                wg_hi_ref, wu_hi_ref, wd_hi_ref, g_ref, b_ref,
                out_hbm, xbuf, ybuf, gsem, ssem, *, alpha):
    j = pl.program_id(0)
    n_used = nused_ref[0]
    parity = lax.rem(j, 2)
    ngroups, _, d = xbuf.shape[1:]
    block = 32

    def for_rows(cnt, fn):
        nblk = lax.shift_right_logical(cnt, block.bit_length() - 1)
        for blk in range(ngroups * 8 // block):
            @pl.when(blk < nblk)
            def _():
                for u in range(block):
                    r = blk * block + u
                    fn(r // 8, r % 8, r, u % 2)

        def single(r, c):
            fn(lax.shift_right_logical(r, 3), jnp.bitwise_and(r, 7), r, 0)
            return c

        lax.fori_loop(nblk * block, cnt, single, 0)

    def wait_rows(cnt, group_copy, row_copy):
        ngrp = lax.shift_right_logical(cnt, 3)

        @pl.when(ngrp > 0)
        def _():
            group_copy(ngrp).wait()

        def single(r, c):
            row_copy(lax.shift_right_logical(r, 3), jnp.bitwise_and(r, 7)).wait()
            return c
        lax.fori_loop(ngrp * 8, cnt, single, 0)

    def start_gather(t, s):
        start = tstart_ref[t]

        def fn(gi, u, r, prio):
            tok = src_ref[start + r]
            pltpu.make_async_copy(x_hbm.at[pl.ds(tok, 1)], xbuf.at[s, gi, pl.ds(u, 1)],
                                  gsem.at[s]).start(priority=prio)
        for_rows(tcnt_ref[t], fn)

    def wait_gather(t, s):
        wait_rows(tcnt_ref[t],
                  lambda n: pltpu.make_async_copy(xbuf.at[s, pl.ds(0, n)], xbuf.at[s, pl.ds(0, n)], gsem.at[s]),
                  lambda gi, u: pltpu.make_async_copy(xbuf.at[s, gi, pl.ds(u, 1)], xbuf.at[s, gi, pl.ds(u, 1)],
                                                      gsem.at[s]))

    def start_scatter(t, s):
        start = tstart_ref[t]

        def fn(gi, u, r, prio):
            tok = src_ref[start + r]
            pltpu.make_async_copy(ybuf.at[s, gi, pl.ds(u, 1)], out_hbm.at[pl.ds(tok, 1)],
                                  ssem.at[s]).start(priority=prio)
        for_rows(tcnt_ref[t], fn)

    def wait_scatter(t, s):
        wait_rows(tcnt_ref[t],
                  lambda n: pltpu.make_async_copy(ybuf.at[s, pl.ds(0, n)], ybuf.at[s, pl.ds(0, n)], ssem.at[s]),
                  lambda gi, u: pltpu.make_async_copy(ybuf.at[s, gi, pl.ds(u, 1)], ybuf.at[s, gi, pl.ds(u, 1)],
                                                      ssem.at[s]))

    @pl.when(j == 0)
    def _():
        xbuf[...] = jnp.zeros(xbuf.shape, F32)
        start_gather(0, 0)

    def step(slot):
        @pl.when(j + 1 < n_used)
        def _():
            start_gather(j + 1, 1 - slot)

        wait_gather(j, slot)

        @pl.when(j >= 2)
        def _():
            wait_scatter(j - 2, slot)

        x = xbuf[slot].reshape(ngroups * 8, d)
        xb = x.astype(BF16)

        def expert(rw_ref, wg_ref, wu_ref, wd_ref):
            h = _dot(xb, wg_ref[...].astype(BF16))
            h = h * _sigmoid(h) * _dot(xb, wu_ref[...].astype(BF16))
            y = _dot(h.astype(BF16), wd_ref[...].astype(BF16))
            score = _sigmoid(jnp.sum(x * rw_ref[...], axis=-1, keepdims=True))
            return y, score

        y_lo, s_lo = expert(rw_lo_ref, wg_lo_ref, wu_lo_ref, wd_lo_ref)
        y_hi, s_hi = expert(rw_hi_ref, wg_hi_ref, wu_hi_ref, wd_hi_ref)
        tot = s_lo + s_hi
        m = (s_lo / tot) * y_lo + (s_hi / tot) * y_hi
        ybuf[slot] = _layer_norm(alpha * x + m, g_ref[...], b_ref[...]).reshape(ngroups, 8, d)
        start_scatter(j, slot)

        @pl.when(j == n_used - 1)
        def _():
            @pl.when(j >= 1)
            def _():
                wait_scatter(j - 1, 1 - slot)
            wait_scatter(j, slot)

    for s in range(2):
        pl.when(jnp.logical_and(j < n_used, parity == s))(functools.partial(step, s))


def _moe(x, cls, router_wt3, wg, wu, wd, layer, g, b, alpha):
    n, d = x.shape
    de = wg.shape[3]
    tile = MOE_TILE if n >= 32 * MOE_TILE else MOE_TILE // 4
    src, tstart, tcnt, e_lo, e_hi, n_used = _moe_plan(cls, tile)
    max_tiles = tstart.shape[0]

    def by_lo(shape):
        return pl.BlockSpec(shape, lambda j, s, ts, tc, el, eh, nu: (el[j], 0, 0))

    def by_hi(shape):
        return pl.BlockSpec(shape, lambda j, s, ts, tc, el, eh, nu: (eh[j], 0, 0))

    def w_lo(r, c):
        return pl.BlockSpec((None, None, r, c), lambda j, s, ts, tc, el, eh, nu: (layer, el[j], 0, 0))

    def w_hi(r, c):
        return pl.BlockSpec((None, None, r, c), lambda j, s, ts, tc, el, eh, nu: (layer, eh[j], 0, 0))

    row = pl.BlockSpec((1, d), lambda j, s, ts, tc, el, eh, nu: (0, 0))
    grid_spec = pltpu.PrefetchScalarGridSpec(
        num_scalar_prefetch=6,
        grid=(max_tiles,),
        in_specs=[pl.BlockSpec(memory_space=pl.ANY),
                  by_lo((None, 1, d)), by_hi((None, 1, d)),
                  w_lo(d, de), w_lo(d, de), w_lo(de, d),
                  w_hi(d, de), w_hi(d, de), w_hi(de, d),
                  row, row],
        out_specs=pl.BlockSpec(memory_space=pl.ANY),
        scratch_shapes=[pltpu.VMEM((2, tile // 8, 8, d), F32), pltpu.VMEM((2, tile // 8, 8, d), F32),
                        pltpu.SemaphoreType.DMA((2,)), pltpu.SemaphoreType.DMA((2,))],
    )
    return pl.pallas_call(
        functools.partial(_moe_kernel, alpha=alpha),
        out_shape=jax.ShapeDtypeStruct((n, d), F32),
        grid_spec=grid_spec,
        compiler_params=_params(("arbitrary",)),
        name="grouped_moe",
    )(src, tstart, tcnt, e_lo, e_hi, n_used,
      x, router_wt3, router_wt3, wg, wu, wd, wg, wu, wd, g, b)


def kernel(x_prompt, x_sample, state_hgrn, state_gla, cache_mem_k, cache_mem_v, mem_prompt, hgrn_w_in, hgrn_lb_logits, hgrn_norm_g, hgrn_w_out, gla_w_in, gla_w_gk2, gla_b_gk2, gla_norm_g, gla_w_out, xattn_w_q, xattn_w_kv, xattn_w_o, router_w, router_bias, moe_w_gate, moe_w_up, moe_w_down, ln_g, ln_b):
    batch, seq, d = x_prompt.shape
    dec_batch, dec_seq, _ = x_sample.shape
    depth = ln_g.shape[0]
    alpha = (2 * depth) ** 0.25
    a_heads, a_key, a_val = state_hgrn.shape[2:]
    b_heads, b_key, b_val = state_gla.shape[2:]
    mem_len, x_heads = cache_mem_k.shape[2], cache_mem_k.shape[3]
    rank = gla_w_gk2.shape[1]
    b_main = 2 * b_heads * b_key + b_heads * b_val + d

    hgrn_w_in_b = hgrn_w_in.astype(BF16)
    hgrn_w_out_b = hgrn_w_out.astype(BF16)
    gla_w_in_b = jnp.concatenate(
        [gla_w_in, jnp.zeros(gla_w_in.shape[:2] + (GLA_RANK_PAD - rank,), F32)], axis=-1).astype(BF16)
    gla_w_gk2_p = jnp.concatenate(
        [gla_w_gk2, jnp.zeros((gla_w_gk2.shape[0], GLA_RANK_PAD - rank, gla_w_gk2.shape[2]), F32)], axis=1)
    gla_w_out_b = gla_w_out.astype(BF16)
    w_q_b = xattn_w_q.astype(BF16)
    w_k_b = xattn_w_kv[:, :, :d].astype(BF16)
    w_v_b = xattn_w_kv[:, :, d:].astype(BF16)
    w_o_b = xattn_w_o.astype(BF16)
    wg_b, wu_b, wd_b = moe_w_gate, moe_w_up, moe_w_down
    router_wt = router_w.T
    router_wt3 = router_wt[:, None, :]
    bias_col = router_bias[:, None]
    assert b_main + rank == gla_w_in.shape[2]

    mem_k_p, mem_v_p = _kv_proj(mem_prompt.reshape(batch * mem_len, d), w_k_b, w_v_b, 512, x_heads)
    rows_per_seq = mem_len * d // 128
    mem_k_p = mem_k_p.reshape(depth, batch, rows_per_seq, 128)
    mem_v_p = mem_v_p.reshape(depth, batch, rows_per_seq, 128)

    def head_major(mem):
        nt = d // x_heads // 128
        return mem.reshape(depth, batch, mem_len, nt, x_heads, 128).transpose(0, 1, 2, 4, 3, 5).reshape(
            depth, batch, mem_len, x_heads, d // x_heads)
    mem_k_prompt = head_major(mem_k_p)
    mem_v_prompt = head_major(mem_v_p)

    def run_trunk(x3, states_a, states_b, mem_ks, mem_vs):
        nb, ns, _ = x3.shape
        x = x3.reshape(nb * ns, d)
        new_a, new_b = [], []
        for l in range(depth):
            j = l // 2
            row = lambda a, i: a[l, i][None, :]
            if l % 2 == 0:
                w_in, w_out, states, new = hgrn_w_in_b[j], hgrn_w_out_b[j], states_a, new_a
                aux = (hgrn_lb_logits, hgrn_lb_logits[:1], hgrn_norm_g[j][None, :])
                cfg = dict(mode="hgrn", batch=nb, seq=ns, heads=a_heads, kd=a_key, vd=a_val, lb_row=l)
            else:
                w_in, w_out, states, new = gla_w_in_b[j], gla_w_out_b[j], states_b, new_b
                aux = (gla_w_gk2_p[j], gla_b_gk2[j][None, :], gla_norm_g[j][None, :])
                cfg = dict(mode="gla", batch=nb, seq=ns, heads=b_heads, kd=b_key, vd=b_val)
            if states is None:
                x, s = _mixer(x, w_in, *aux, w_out, row(ln_g, 0), row(ln_b, 0), alpha, **cfg)
            else:
                proj = _linear(x, w_in, F32, 512)
                o, s = _recurrence(proj, states[j], *aux, **cfg)
                x = _linear_res_ln(o, w_out, x, row(ln_g, 0), row(ln_b, 0), alpha, 512)
            new.append(s)
            if ns >= 512:
                x, cls = _xattn(x, w_q_b[l], mem_ks, mem_vs, w_o_b[l], l, row(ln_g, 1), row(ln_b, 1),
                                router_wt, bias_col, alpha, batch=nb, seq=ns, heads=x_heads, tq=1024)
            else:
                q = _linear(x, w_q_b[l], F32, 512)
                c = _attention(q, mem_ks, mem_vs, l, batch=nb, seq=ns, heads=x_heads)
                x = _linear_res_ln(c, w_o_b[l], x, row(ln_g, 1), row(ln_b, 1), alpha, 512)
                cls = _router(x, router_wt, bias_col, 512)
            x = _moe(x, cls[0], router_wt3, wg_b, wu_b, wd_b, l, row(ln_g, 2), row(ln_b, 2), alpha)
        return x.reshape(nb, ns, d), jnp.stack(new_a), jnp.stack(new_b)

    y_prompt, state_hgrn_prompt, state_gla_prompt = run_trunk(x_prompt, None, None, mem_k_p, mem_v_p)
    y_sample, state_hgrn_sample, state_gla_sample = run_trunk(
        x_sample, state_hgrn, state_gla, _interleaved_rows(cache_mem_k), _interleaved_rows(cache_mem_v))
    return (y_prompt, y_sample, state_hgrn_prompt, state_gla_prompt, mem_k_prompt, mem_v_prompt,
            state_hgrn_sample, state_gla_sample)
```

```python
import functools

import jax
import jax.numpy as jnp
from jax import lax
from jax.experimental import pallas as pl
from jax.experimental.pallas import tpu as pltpu

F32 = jnp.float32
BF16 = jnp.bfloat16
I32 = jnp.int32

N_GROUPS = 4
GROUP_SIZE = 4
GLA_GATE_NORMALIZER = 16.0
LN_EPS = 1e-5
RMS_EPS = 1e-6
GLA_RANK_PAD = 128

VMEM_LIMIT_BYTES = 56 * 1024 * 1024

ROW_TILE = 512
XATTN_TILE = 1024
REC_CHUNK = 64
MIXER_CHUNKS = 8
REC_SEQS = 8
REC_UNROLL = 4
ATTN_SEQS = 4
MOE_TILE = 256
MOE_COPY_BLOCK = 32


def _params(sem):
    return pltpu.CompilerParams(dimension_semantics=sem, vmem_limit_bytes=VMEM_LIMIT_BYTES)


def _dot(a, b):
    return jnp.dot(a, b, preferred_element_type=F32)


def _dot_nt(a, b):
    return lax.dot_general(a, b, (((1,), (1,)), ((), ())), preferred_element_type=F32)


def _dot_precise(a, b, dot=_dot):
    a_hi = a.astype(BF16)
    b_hi = b.astype(BF16)
    a_lo = (a - a_hi.astype(F32)).astype(BF16)
    b_lo = (b - b_hi.astype(F32)).astype(BF16)
    return dot(a_hi, b_hi) + dot(a_hi, b_lo) + dot(a_lo, b_hi)


def _dot_tn(a, b):
    return lax.dot_general(a, b, (((0,), (0,)), ((), ())), preferred_element_type=F32)


def _layer_norm(z, g, b):
    mu = jnp.mean(z, axis=-1, keepdims=True)
    zc = z - mu
    var = jnp.mean(zc * zc, axis=-1, keepdims=True)
    return zc * lax.rsqrt(var + LN_EPS) * g + b


def _sigmoid(x):
    return 0.5 * jnp.tanh(0.5 * x) + 0.5


def _log_sigmoid(x):
    return jnp.minimum(x, 0.0) - jnp.log(1.0 + jnp.exp(-jnp.abs(x)))


def _linear_kernel(x_ref, w_ref, o_ref):
    o_ref[...] = _dot(x_ref[...].astype(BF16), w_ref[...]).astype(o_ref.dtype)


def _linear(x, w, out_dtype, tm):
    m, k = x.shape
    n = w.shape[1]
    return pl.pallas_call(
        _linear_kernel,
        out_shape=jax.ShapeDtypeStruct((m, n), out_dtype),
        grid=(m // tm,),
        in_specs=[pl.BlockSpec((tm, k), lambda i: (i, 0)),
                  pl.BlockSpec((k, n), lambda i: (0, 0))],
        out_specs=pl.BlockSpec((tm, n), lambda i: (i, 0)),
        compiler_params=_params(("arbitrary",)),
        name="linear",
    )(x, w)


def _kv_proj_kernel(x_ref, wk_ref, wv_ref, k_ref, v_ref, *, heads):
    x = x_ref[...].astype(BF16)
    tm, d = x.shape
    hd = d // heads
    nt = hd // 128
    for w_ref, o_ref in ((wk_ref, k_ref), (wv_ref, v_ref)):
        y = _dot(x, w_ref[...])
        for h in range(heads):
            for t in range(nt):
                o_ref[pl.ds(t * heads + h, tm, stride=heads * nt), :] = y[:, h * hd + t * 128:h * hd + (t + 1) * 128]


def _kv_proj(mem, wk, wv, tm, heads):
    r, d = mem.shape
    nl = wk.shape[0]
    out = jax.ShapeDtypeStruct((nl, r * d // 128, 128), F32)
    wspec = pl.BlockSpec((None, d, d), lambda l, i: (l, 0, 0))
    ospec = pl.BlockSpec((None, tm * d // 128, 128), lambda l, i: (l, i, 0))
    return pl.pallas_call(
        functools.partial(_kv_proj_kernel, heads=heads),
        out_shape=(out, out),
        grid=(nl, r // tm),
        in_specs=[pl.BlockSpec((tm, d), lambda l, i: (i, 0)), wspec, wspec],
        out_specs=(ospec, ospec),
        compiler_params=_params(("arbitrary", "arbitrary")),
        name="kv_proj",
    )(mem, wk, wv)


def _linear_res_ln_kernel(h_ref, w_ref, x_ref, g_ref, b_ref, o_ref, *, alpha):
    c = _dot(h_ref[...].astype(BF16), w_ref[...])
    o_ref[...] = _layer_norm(alpha * x_ref[...] + c, g_ref[...], b_ref[...])


def _linear_res_ln(h, w, x, g, b, alpha, tm):
    m, k = h.shape
    d = w.shape[1]
    row = pl.BlockSpec((1, d), lambda i: (0, 0))
    return pl.pallas_call(
        functools.partial(_linear_res_ln_kernel, alpha=alpha),
        out_shape=jax.ShapeDtypeStruct((m, d), F32),
        grid=(m // tm,),
        in_specs=[pl.BlockSpec((tm, k), lambda i: (i, 0)),
                  pl.BlockSpec((k, d), lambda i: (0, 0)),
                  pl.BlockSpec((tm, d), lambda i: (i, 0)), row, row],
        out_specs=pl.BlockSpec((tm, d), lambda i: (i, 0)),
        compiler_params=_params(("arbitrary",)),
        name="linear_res_ln",
    )(h, w, x, g, b)


def _cumsum_rows(x):
    n = x.shape[0]
    row = lax.broadcasted_iota(I32, x.shape, 0)
    s = 1
    while s < n:
        x = x + jnp.where(row >= s, pltpu.roll(x, s, 0), 0.0)
        s *= 2
    return x


def _chunk_prepare(q, k, v, g):
    c, kd = q.shape
    vd = v.shape[1]
    b = _cumsum_rows(g)
    b_last = b[c - 1:c, :]
    b_mid = b[c // 2:c // 2 + 1, :]
    qa = q * jnp.exp(b - b_mid)
    ka = k * jnp.exp(b_mid - b)
    scores = _dot_nt(qa.astype(BF16), ka.astype(BF16))
    ri = lax.broadcasted_iota(I32, (c, c), 0)
    ci = lax.broadcasted_iota(I32, (c, c), 1)
    scores = jnp.where(ri >= ci, scores, 0.0).astype(BF16)
    decay = jnp.broadcast_to(jnp.exp(b_last), (kd, kd)).T
    if vd != kd:
        decay = jnp.concatenate([decay] * (vd // kd), axis=1)
    return ((qa * jnp.exp(b_mid)).astype(BF16), scores,
            (ka * jnp.exp(b_last - b_mid)).astype(BF16), v.astype(BF16), decay)


def _chunk_local(prep):
    _, scores, ks, vb, _ = prep
    return _dot(scores, vb), _dot_tn(ks, vb)


def _chunk_finish(prep, local, state):
    q_in, _, _, _, decay = prep
    o_local, increment = local
    return _dot(q_in, state.astype(BF16)) + o_local, state * decay + increment


def _gated_rmsnorm(o, gate, gain):
    o = o * lax.rsqrt(jnp.mean(o * o, axis=-1, keepdims=True) + RMS_EPS)
    return o * (0.5 * gain) * (gate * (jnp.tanh(0.5 * gate) + 1.0))


def _lower_bound(logits, lb_row):
    e = jnp.exp(logits - jnp.max(logits, axis=0, keepdims=True))
    return jnp.sum(e[:lb_row + 1], axis=0, keepdims=True) / jnp.sum(e, axis=0, keepdims=True)


def _gla_gates(lr, w_gk2, b_gk2):
    z = _dot_precise(lr, w_gk2) + b_gk2
    return _log_sigmoid(z) * (1.0 / GLA_GATE_NORMALIZER)


def _head_inputs(mode, proj_ref, rows, h, heads, kd, vd, lb, gk_all):
    hk = heads * kd
    hv = heads * vd
    v = proj_ref[rows, 2 * hk + h * vd:2 * hk + (h + 1) * vd]
    gate = proj_ref[rows, 2 * hk + hv + h * vd:2 * hk + hv + (h + 1) * vd]
    if mode == "hgrn":
        q = proj_ref[rows, h * kd:(h + 1) * kd]
        f = proj_ref[rows, hk + h * kd:hk + (h + 1) * kd]
        lbh = lb[:, h * kd:(h + 1) * kd]
        q = (0.5 * kd ** -0.5) * q * (jnp.tanh(0.5 * q) + 1.0)
        fg = 0.5 * (1.0 + lbh) + (0.5 * (1.0 - lbh)) * jnp.tanh(0.5 * f)
        return q, 1.0 - fg, v, jnp.log(fg), gate
    q = proj_ref[rows, h * kd:(h + 1) * kd] * (kd ** -0.5)
    k = proj_ref[rows, hk + h * kd:hk + (h + 1) * kd]
    return q, k, v, gk_all[:, h * kd:(h + 1) * kd], gate


def _rec_kernel(proj_ref, s0_ref, aux0_ref, aux1_ref, gain_ref, o_ref, s_ref, *,
                mode, heads, kd, vd, chunk, tb, nbatch, lb_row):
    @pl.when(pl.program_id(1) == 0)
    def _():
        s_ref[...] = s0_ref[...]

    hk = heads * kd
    hv = heads * vd
    gain = gain_ref[...]
    lb = _lower_bound(aux0_ref[...], lb_row) if mode == "hgrn" else None

    def one_batch(nb):
        for sc in range(tb // chunk):
            r0 = nb * tb + sc * chunk
            rows = pl.ds(r0, chunk) if isinstance(r0, int) else pl.ds(pl.multiple_of(r0, 8), chunk)
            gk_all = None
            if mode == "gla":
                lr = proj_ref[rows, 2 * hk + 2 * hv:2 * hk + 2 * hv + GLA_RANK_PAD]
                gk_all = _gla_gates(lr, aux0_ref[...], aux1_ref[...])
            inputs = [_head_inputs(mode, proj_ref, rows, h, heads, kd, vd, lb, gk_all) for h in range(heads)]
            preps = [_chunk_prepare(q, k, v, g) for q, k, v, g, _ in inputs]
            local = [_chunk_local(p) for p in preps]
            for h in range(heads):
                o, s_new = _chunk_finish(preps[h], local[h], s_ref[nb, h])
                s_ref[nb, h] = s_new
                o_ref[rows, h * vd:(h + 1) * vd] = _gated_rmsnorm(o, inputs[h][4], gain)

    if nbatch == 1:
        one_batch(0)
    else:
        def body(nb, carry):
            one_batch(nb)
            return carry
        lax.fori_loop(0, nbatch, body, 0, unroll=REC_UNROLL)


def _recurrence(proj, s0, aux0, aux1, gain, *, mode, batch, seq, heads, kd, vd, lb_row=0):
    n, width = proj.shape
    chunk = min(REC_CHUNK, seq)
    tb = min(2 * chunk, seq)
    nbatch = 1 if seq > tb else min(REC_SEQS, batch)
    nblk = seq // tb
    grid = (batch // nbatch, nblk)
    state_spec = pl.BlockSpec((nbatch, heads, kd, vd), lambda b, c: (b, 0, 0, 0))
    full2 = lambda a: pl.BlockSpec(a.shape, lambda b, c: (0, 0))
    kern = functools.partial(_rec_kernel, mode=mode, heads=heads, kd=kd, vd=vd, chunk=chunk,
                             tb=tb, nbatch=nbatch, lb_row=lb_row)
    return pl.pallas_call(
        kern,
        out_shape=(jax.ShapeDtypeStruct((n, heads * vd), F32),
                   jax.ShapeDtypeStruct((batch, heads, kd, vd), F32)),
        grid=grid,
        in_specs=[pl.BlockSpec((nbatch * tb, width), lambda b, c: (b * nblk + c, 0)), state_spec,
                  full2(aux0), full2(aux1), full2(gain)],
        out_specs=(pl.BlockSpec((nbatch * tb, heads * vd), lambda b, c: (b * nblk + c, 0)),
                   state_spec),
        compiler_params=_params(("arbitrary", "arbitrary")),
        name="recurrence_" + mode,
    )(proj, s0, aux0, aux1, gain)


def _mixer_kernel(x_ref, w_in_ref, aux0_ref, aux1_ref, gain_ref, w_out_ref, g_ref, b_ref,
                  y_ref, s_ref, proj_scr, o_scr, *, mode, heads, kd, vd, chunk, lb_row, alpha):
    @pl.when(pl.program_id(1) == 0)
    def _():
        s_ref[...] = jnp.zeros(s_ref.shape, F32)

    tb = x_ref.shape[0]
    hk = heads * kd
    hv = heads * vd
    gain = gain_ref[...]
    x = x_ref[...]
    xb = x.astype(BF16)
    group = 256 // kd
    seg_starts = (0, hk, 2 * hk, 2 * hk + hv)
    seg_widths = (kd, kd, vd, vd)

    def project(gi):
        for start, w in zip(seg_starts, seg_widths):
            cols = slice(start + gi * group * w, start + (gi + 1) * group * w)
            proj_scr[:, cols] = _dot(xb, w_in_ref[:, cols])

    lb = None
    gk_blocks = None
    if mode == "hgrn":
        lb = _lower_bound(aux0_ref[...], lb_row)
    else:
        lr = _dot(xb, w_in_ref[:, 2 * hk + 2 * hv:2 * hk + 2 * hv + GLA_RANK_PAD])
        gk_full = _gla_gates(lr, aux0_ref[...], aux1_ref[...])
        gk_blocks = [gk_full[sc * chunk:(sc + 1) * chunk, :] for sc in range(tb // chunk)]

    n_groups = heads // group
    project(0)
    for gi in range(n_groups):
        if gi + 1 < n_groups:
            project(gi + 1)
        units = [(h, sc) for h in range(gi * group, (gi + 1) * group) for sc in range(tb // chunk)]
        inputs = [_head_inputs(mode, proj_scr, pl.ds(sc * chunk, chunk), h, heads, kd, vd, lb,
                               None if gk_blocks is None else gk_blocks[sc]) for h, sc in units]
        preps = [_chunk_prepare(q, k, v, g) for q, k, v, g, _ in inputs]
        local = [_chunk_local(p) for p in preps]
        for u, (h, sc) in enumerate(units):
            o, s_new = _chunk_finish(preps[u], local[u], s_ref[0, h])
            s_ref[0, h] = s_new
            o_scr[pl.ds(sc * chunk, chunk), h * vd:(h + 1) * vd] = (
                _gated_rmsnorm(o, inputs[u][4], gain).astype(BF16))
    c = _dot(o_scr[...], w_out_ref[...])
    y_ref[...] = _layer_norm(alpha * x + c, g_ref[...], b_ref[...])


def _mixer(x, w_in, aux0, aux1, gain, w_out, g, b, alpha, *, mode, batch, seq, heads, kd, vd, lb_row=0):
    n, d = x.shape
    width = w_in.shape[1]
    tb = MIXER_CHUNKS * REC_CHUNK
    nblk = seq // tb
    const = lambda a: pl.BlockSpec(a.shape, lambda bb, c: (0, 0))
    rows = pl.BlockSpec((tb, d), lambda bb, c: (bb * nblk + c, 0))
    kern = functools.partial(_mixer_kernel, mode=mode, heads=heads, kd=kd, vd=vd, chunk=REC_CHUNK,
                             lb_row=lb_row, alpha=alpha)
    return pl.pallas_call(
        kern,
        out_shape=(jax.ShapeDtypeStruct((n, d), F32),
                   jax.ShapeDtypeStruct((batch, heads, kd, vd), F32)),
        grid=(batch, nblk),
        in_specs=[rows, const(w_in), const(aux0), const(aux1), const(gain), const(w_out), const(g), const(b)],
        out_specs=(rows, pl.BlockSpec((1, heads, kd, vd), lambda bb, c: (bb, 0, 0, 0))),
        scratch_shapes=[pltpu.VMEM((tb, width), F32), pltpu.VMEM((tb, heads * vd), BF16)],
        compiler_params=_params(("arbitrary", "arbitrary")),
        name="mixer_" + mode,
    )(x, w_in, aux0, aux1, gain, w_out, g, b)


def _attn_kernel(q_ref, k_ref, v_ref, o_ref, *, heads, nbatch, tq):
    hd = q_ref.shape[1] // heads
    scale = hd ** -0.5
    units = [(nb, h) for nb in range(nbatch) for h in range(heads)]

    def head_rows(ref, nb, h):
        nt = hd // 128
        period = heads * nt
        mlen = ref.shape[1] // period
        parts = [ref[nb, pl.ds(t * heads + h, mlen, stride=period), :] for t in range(nt)]
        return jnp.concatenate(parts, axis=1).astype(BF16)

    scores = [_dot_nt(q_ref[nb * tq:(nb + 1) * tq, h * hd:(h + 1) * hd].astype(BF16),
                      head_rows(k_ref, nb, h)) * scale for nb, h in units]
    probs = []
    for s in scores:
        p = jnp.exp(s - jnp.max(s, axis=-1, keepdims=True))
        probs.append((p / jnp.sum(p, axis=-1, keepdims=True)).astype(BF16))
    for (nb, h), p in zip(units, probs):
        o_ref[nb * tq:(nb + 1) * tq, h * hd:(h + 1) * hd] = _dot(p, head_rows(v_ref, nb, h)).astype(o_ref.dtype)


def _xattn_kernel(x_ref, wq_ref, k_ref, v_ref, wo_ref, g_ref, b_ref, rw_ref, rb_ref, y_ref, cls_ref, *,
                  heads, alpha):
    x = x_ref[...]
    hd = x.shape[1] // heads
    scale = hd ** -0.5
    q = _dot(x.astype(BF16), wq_ref[...]).astype(BF16)
    nt = hd // 128
    period = heads * nt
    mlen = k_ref.shape[0] // period

    def head_rows(ref, h):
        parts = [ref[pl.ds(t * heads + h, mlen, stride=period), :] for t in range(nt)]
        return jnp.concatenate(parts, axis=1).astype(BF16)

    scores = [_dot_nt(q[:, h * hd:(h + 1) * hd], head_rows(k_ref, h)) * scale for h in range(heads)]
    probs = []
    for s in scores:
        p = jnp.exp(s - jnp.max(s, axis=-1, keepdims=True))
        probs.append((p / jnp.sum(p, axis=-1, keepdims=True)).astype(BF16))
    o = jnp.concatenate([_dot(p, head_rows(v_ref, h)).astype(BF16) for h, p in enumerate(probs)], axis=1)
    y = _layer_norm(alpha * x + _dot(o, wo_ref[...]), g_ref[...], b_ref[...])
    y_ref[...] = y
    cls_ref[...] = _routing_class(y, rw_ref[...], rb_ref[...])


def _xattn(x, wq, mem_k, mem_v, wo, layer, g, b, router_wt, bias_col, alpha, *, batch, seq, heads, tq):
    n, d = x.shape
    nblk = seq // tq
    const = lambda a: pl.BlockSpec(a.shape, lambda bb, i: (0, 0))
    rows = pl.BlockSpec((tq, d), lambda bb, i: (bb * nblk + i, 0))
    mem = pl.BlockSpec((None, None) + mem_k.shape[2:], lambda bb, i: (layer, bb, 0, 0))
    return pl.pallas_call(
        functools.partial(_xattn_kernel, heads=heads, alpha=alpha),
        out_shape=(jax.ShapeDtypeStruct((n, d), F32), jax.ShapeDtypeStruct((1, n), I32)),
        grid=(batch, nblk),
        in_specs=[rows, const(wq), mem, mem, const(wo), const(g), const(b), const(router_wt), const(bias_col)],
        out_specs=(rows, pl.BlockSpec((1, tq), lambda bb, i: (0, bb * nblk + i))),
        compiler_params=_params(("arbitrary", "arbitrary")),
        name="xattn_block",
    )(x, wq, mem_k, mem_v, wo, g, b, router_wt, bias_col)


def _interleaved_rows(mem):
    nl, b, m, heads, hd = mem.shape
    nt = hd // 128
    return mem.reshape(nl, b, m, heads, nt, 128).transpose(0, 1, 2, 4, 3, 5).reshape(nl, b, m * nt * heads, 128)


def _attention(q, mem_k, mem_v, layer, *, batch, seq, heads):
    n, d = q.shape
    nbatch = ATTN_SEQS
    mem_spec = pl.BlockSpec((None, nbatch) + mem_k.shape[2:], lambda b: (layer, b, 0, 0))
    qspec = pl.BlockSpec((nbatch * seq, d), lambda b: (b, 0))
    return pl.pallas_call(
        functools.partial(_attn_kernel, heads=heads, nbatch=nbatch, tq=seq),
        out_shape=jax.ShapeDtypeStruct((n, d), q.dtype),
        grid=(batch // nbatch,),
        in_specs=[qspec, mem_spec, mem_spec],
        out_specs=qspec,
        compiler_params=_params(("arbitrary",)),
        name="mem_attention",
    )(q, mem_k, mem_v)


def _routing_class(x, wt, bias_col):
    logits = _dot_precise(wt, x, _dot_nt)
    sel = jax.nn.sigmoid(logits) + bias_col
    rows = [sel[e:e + 1, :] for e in range(N_GROUPS * GROUP_SIZE)]

    def first_argmax(vals):
        best_v, best_i = vals[0], jnp.zeros(vals[0].shape, I32)
        for i in range(1, len(vals)):
            better = vals[i] > best_v
            best_i = jnp.where(better, i, best_i)
            best_v = jnp.where(better, vals[i], best_v)
        return best_i

    group_scores = []
    for gi in range(N_GROUPS):
        a = rows[gi * GROUP_SIZE:(gi + 1) * GROUP_SIZE]
        top2 = None
        for i in range(GROUP_SIZE):
            for j in range(i + 1, GROUP_SIZE):
                s = a[i] + a[j]
                top2 = s if top2 is None else jnp.maximum(top2, s)
        group_scores.append(top2)
    best = first_argmax(group_scores)
    cand = []
    for j in range(GROUP_SIZE):
        cj = rows[j]
        for gi in range(1, N_GROUPS):
            cj = jnp.where(best == gi, rows[gi * GROUP_SIZE + j], cj)
        cand.append(cj)
    i1 = first_argmax(cand)
    i2 = first_argmax([jnp.where(i1 == j, -jnp.inf, cand[j]) for j in range(GROUP_SIZE)])
    lo = jnp.minimum(i1, i2)
    hi = jnp.maximum(i1, i2)
    pair = jnp.where(lo == 0, hi - 1, jnp.where(lo == 1, 6 - hi, 5))
    return best * 6 + pair


def _router_kernel(x_ref, wt_ref, bias_ref, cls_ref):
    cls_ref[...] = _routing_class(x_ref[...], wt_ref[...], bias_ref[...])


def _router(x, router_wt, bias_col, tm):
    n, d = x.shape
    ne = router_wt.shape[0]
    return pl.pallas_call(
        _router_kernel,
        out_shape=jax.ShapeDtypeStruct((1, n), I32),
        grid=(n // tm,),
        in_specs=[pl.BlockSpec((tm, d), lambda i: (i, 0)),
                  pl.BlockSpec((ne, d), lambda i: (0, 0)),
                  pl.BlockSpec((ne, 1), lambda i: (0, 0))],
        out_specs=pl.BlockSpec((1, tm), lambda i: (0, i)),
        compiler_params=_params(("arbitrary",)),
        name="router",
    )(x, router_wt, bias_col)


_PAIR_EXPERTS = ((0, 1), (0, 2), (0, 3), (1, 3), (1, 2), (3, 2))


def _moe_plan(cls, tile):
    n = cls.shape[0]
    ncls = N_GROUPS * 6
    max_tiles = n // tile + ncls
    shift = max(n - 1, 1).bit_length()
    keys = jnp.sort(cls * (1 << shift) + jnp.arange(n, dtype=I32))
    src = keys & ((1 << shift) - 1)
    cid = jnp.arange(ncls, dtype=I32)
    count = jnp.sum((cls[None, :] == cid[:, None]).astype(I32), axis=1)
    cstart = jnp.cumsum(count) - count
    ntile = (count + tile - 1) // tile
    tend = jnp.cumsum(ntile)
    tid = jnp.arange(max_tiles, dtype=I32)
    n_used = tend[-1]
    tcls = jnp.sum((tid[:, None] >= tend[None, :]).astype(I32), axis=1)
    last_cls = jnp.sum((n_used - 1 >= tend).astype(I32))
    tcls = jnp.where(tid < n_used, tcls, last_cls)
    within = tid - (tend - ntile)[tcls]
    tstart = cstart[tcls] + within * tile
    tcnt = jnp.clip(count[tcls] - within * tile, 0, tile)
    tcnt = jnp.where(tid < n_used, tcnt, 0)
    grp = tcls // 6
    e_lo = grp * GROUP_SIZE + jnp.asarray([p[0] for p in _PAIR_EXPERTS], I32)[tcls % 6]
    e_hi = grp * GROUP_SIZE + jnp.asarray([p[1] for p in _PAIR_EXPERTS], I32)[tcls % 6]
    return src, tstart, tcnt, e_lo, e_hi, n_used.reshape(1)


def _moe_kernel(src_ref, tstart_ref, tcnt_ref, elo_ref, ehi_ref, nused_ref,
                x_hbm, rw_lo_ref, rw_hi_ref, wg_lo_ref, wu_lo_ref, wd_lo_ref,
                wg_hi_ref, wu_hi_ref, wd_hi_ref, g_ref, b_ref,
                out_hbm, xbuf, ybuf, gsem, ssem, *, alpha):
    j = pl.program_id(0)
    n_used = nused_ref[0]
    parity = lax.rem(j, 2)
    ngroups, _, d = xbuf.shape[1:]
    block = MOE_COPY_BLOCK

    def for_rows(cnt, fn):
        nblk = lax.shift_right_logical(cnt, block.bit_length() - 1)
        for blk in range(ngroups * 8 // block):
            @pl.when(blk < nblk)
            def _():
                for u in range(block):
                    r = blk * block + u
                    fn(r // 8, r % 8, r, u % 2)

        def single(r, c):
            fn(lax.shift_right_logical(r, 3), jnp.bitwise_and(r, 7), r, 0)
            return c

        lax.fori_loop(nblk * block, cnt, single, 0)

    def wait_rows(cnt, group_copy, row_copy):
        ngrp = lax.shift_right_logical(cnt, 3)

        @pl.when(ngrp > 0)
        def _():
            group_copy(ngrp).wait()

        def single(r, c):
            row_copy(lax.shift_right_logical(r, 3), jnp.bitwise_and(r, 7)).wait()
            return c
        lax.fori_loop(ngrp * 8, cnt, single, 0)

    def start_gather(t, s):
        start = tstart_ref[t]

        def fn(gi, u, r, prio):
            tok = src_ref[start + r]
            pltpu.make_async_copy(x_hbm.at[pl.ds(tok, 1)], xbuf.at[s, gi, pl.ds(u, 1)],
                                  gsem.at[s]).start(priority=prio)
        for_rows(tcnt_ref[t], fn)

    def wait_gather(t, s):
        wait_rows(tcnt_ref[t],
                  lambda n: pltpu.make_async_copy(xbuf.at[s, pl.ds(0, n)], xbuf.at[s, pl.ds(0, n)], gsem.at[s]),
                  lambda gi, u: pltpu.make_async_copy(xbuf.at[s, gi, pl.ds(u, 1)], xbuf.at[s, gi, pl.ds(u, 1)],
                                                      gsem.at[s]))

    def start_scatter(t, s):
        start = tstart_ref[t]

        def fn(gi, u, r, prio):
            tok = src_ref[start + r]
            pltpu.make_async_copy(ybuf.at[s, gi, pl.ds(u, 1)], out_hbm.at[pl.ds(tok, 1)],
                                  ssem.at[s]).start(priority=prio)
        for_rows(tcnt_ref[t], fn)

    def wait_scatter(t, s):
        wait_rows(tcnt_ref[t],
                  lambda n: pltpu.make_async_copy(ybuf.at[s, pl.ds(0, n)], ybuf.at[s, pl.ds(0, n)], ssem.at[s]),
                  lambda gi, u: pltpu.make_async_copy(ybuf.at[s, gi, pl.ds(u, 1)], ybuf.at[s, gi, pl.ds(u, 1)],
                                                      ssem.at[s]))

    @pl.when(j == 0)
    def _():
        xbuf[...] = jnp.zeros(xbuf.shape, F32)
        start_gather(0, 0)

    def step(slot):
        @pl.when(j + 1 < n_used)
        def _():
            start_gather(j + 1, 1 - slot)

        wait_gather(j, slot)

        @pl.when(j >= 2)
        def _():
            wait_scatter(j - 2, slot)

        x = xbuf[slot].reshape(ngroups * 8, d)
        xb = x.astype(BF16)

        def expert(rw_ref, wg_ref, wu_ref, wd_ref):
            h = _dot(xb, wg_ref[...].astype(BF16))
            h = h * _sigmoid(h) * _dot(xb, wu_ref[...].astype(BF16))
            y = _dot(h.astype(BF16), wd_ref[...].astype(BF16))
            score = _sigmoid(jnp.sum(x * rw_ref[...], axis=-1, keepdims=True))
            return y, score

        y_lo, s_lo = expert(rw_lo_ref, wg_lo_ref, wu_lo_ref, wd_lo_ref)
        y_hi, s_hi = expert(rw_hi_ref, wg_hi_ref, wu_hi_ref, wd_hi_ref)
        tot = s_lo + s_hi
        m = (s_lo / tot) * y_lo + (s_hi / tot) * y_hi
        ybuf[slot] = _layer_norm(alpha * x + m, g_ref[...], b_ref[...]).reshape(ngroups, 8, d)
        start_scatter(j, slot)

        @pl.when(j == n_used - 1)
        def _():
            @pl.when(j >= 1)
            def _():
                wait_scatter(j - 1, 1 - slot)
            wait_scatter(j, slot)

    for s in range(2):
        pl.when(jnp.logical_and(j < n_used, parity == s))(functools.partial(step, s))


def _moe(x, cls, router_wt3, wg, wu, wd, layer, g, b, alpha):
    n, d = x.shape
    de = wg.shape[3]
    tile = MOE_TILE if n >= 32 * MOE_TILE else MOE_TILE // 4
    src, tstart, tcnt, e_lo, e_hi, n_used = _moe_plan(cls, tile)
    max_tiles = tstart.shape[0]

    def by_lo(shape):
        return pl.BlockSpec(shape, lambda j, s, ts, tc, el, eh, nu: (el[j], 0, 0))

    def by_hi(shape):
        return pl.BlockSpec(shape, lambda j, s, ts, tc, el, eh, nu: (eh[j], 0, 0))

    def w_lo(r, c):
        return pl.BlockSpec((None, None, r, c), lambda j, s, ts, tc, el, eh, nu: (layer, el[j], 0, 0))

    def w_hi(r, c):
        return pl.BlockSpec((None, None, r, c), lambda j, s, ts, tc, el, eh, nu: (layer, eh[j], 0, 0))

    row = pl.BlockSpec((1, d), lambda j, s, ts, tc, el, eh, nu: (0, 0))
    grid_spec = pltpu.PrefetchScalarGridSpec(
        num_scalar_prefetch=6,
        grid=(max_tiles,),
        in_specs=[pl.BlockSpec(memory_space=pl.ANY),
                  by_lo((None, 1, d)), by_hi((None, 1, d)),
                  w_lo(d, de), w_lo(d, de), w_lo(de, d),
                  w_hi(d, de), w_hi(d, de), w_hi(de, d),
                  row, row],
        out_specs=pl.BlockSpec(memory_space=pl.ANY),
        scratch_shapes=[pltpu.VMEM((2, tile // 8, 8, d), F32), pltpu.VMEM((2, tile // 8, 8, d), F32),
                        pltpu.SemaphoreType.DMA((2,)), pltpu.SemaphoreType.DMA((2,))],
    )
    return pl.pallas_call(
        functools.partial(_moe_kernel, alpha=alpha),
        out_shape=jax.ShapeDtypeStruct((n, d), F32),
        grid_spec=grid_spec,
        compiler_params=_params(("arbitrary",)),
        name="grouped_moe",
    )(src, tstart, tcnt, e_lo, e_hi, n_used,
      x, router_wt3, router_wt3, wg, wu, wd, wg, wu, wd, g, b)


def kernel(x_prompt, x_sample, state_hgrn, state_gla, cache_mem_k, cache_mem_v, mem_prompt, hgrn_w_in, hgrn_lb_logits, hgrn_norm_g, hgrn_w_out, gla_w_in, gla_w_gk2, gla_b_gk2, gla_norm_g, gla_w_out, xattn_w_q, xattn_w_kv, xattn_w_o, router_w, router_bias, moe_w_gate, moe_w_up, moe_w_down, ln_g, ln_b):
    batch, seq, d = x_prompt.shape
    dec_batch, dec_seq, _ = x_sample.shape
    depth = ln_g.shape[0]
    alpha = (2 * depth) ** 0.25
    a_heads, a_key, a_val = state_hgrn.shape[2:]
    b_heads, b_key, b_val = state_gla.shape[2:]
    mem_len, x_heads = cache_mem_k.shape[2], cache_mem_k.shape[3]
    rank = gla_w_gk2.shape[1]
    b_main = 2 * b_heads * b_key + b_heads * b_val + d

    hgrn_w_in_b = hgrn_w_in.astype(BF16)
    hgrn_w_out_b = hgrn_w_out.astype(BF16)
    gla_w_in_b = jnp.concatenate(
        [gla_w_in, jnp.zeros(gla_w_in.shape[:2] + (GLA_RANK_PAD - rank,), F32)], axis=-1).astype(BF16)
    gla_w_gk2_p = jnp.concatenate(
        [gla_w_gk2, jnp.zeros((gla_w_gk2.shape[0], GLA_RANK_PAD - rank, gla_w_gk2.shape[2]), F32)], axis=1)
    gla_w_out_b = gla_w_out.astype(BF16)
    w_q_b = xattn_w_q.astype(BF16)
    w_k_b = xattn_w_kv[:, :, :d].astype(BF16)
    w_v_b = xattn_w_kv[:, :, d:].astype(BF16)
    w_o_b = xattn_w_o.astype(BF16)
    wg_b, wu_b, wd_b = moe_w_gate, moe_w_up, moe_w_down
    router_wt = router_w.T
    router_wt3 = router_wt[:, None, :]
    bias_col = router_bias[:, None]
    assert b_main + rank == gla_w_in.shape[2]

    mem_k_p, mem_v_p = _kv_proj(mem_prompt.reshape(batch * mem_len, d), w_k_b, w_v_b, ROW_TILE, x_heads)
    rows_per_seq = mem_len * d // 128
    mem_k_p = mem_k_p.reshape(depth, batch, rows_per_seq, 128)
    mem_v_p = mem_v_p.reshape(depth, batch, rows_per_seq, 128)

    def head_major(mem):
        nt = d // x_heads // 128
        return mem.reshape(depth, batch, mem_len, nt, x_heads, 128).transpose(0, 1, 2, 4, 3, 5).reshape(
            depth, batch, mem_len, x_heads, d // x_heads)
    mem_k_prompt = head_major(mem_k_p)
    mem_v_prompt = head_major(mem_v_p)

    def run_trunk(x3, states_a, states_b, mem_ks, mem_vs):
        nb, ns, _ = x3.shape
        x = x3.reshape(nb * ns, d)
        new_a, new_b = [], []
        for l in range(depth):
            j = l // 2
            row = lambda a, i: a[l, i][None, :]
            if l % 2 == 0:
                w_in, w_out, states, new = hgrn_w_in_b[j], hgrn_w_out_b[j], states_a, new_a
                aux = (hgrn_lb_logits, hgrn_lb_logits[:1], hgrn_norm_g[j][None, :])
                cfg = dict(mode="hgrn", batch=nb, seq=ns, heads=a_heads, kd=a_key, vd=a_val, lb_row=l)
            else:
                w_in, w_out, states, new = gla_w_in_b[j], gla_w_out_b[j], states_b, new_b
                aux = (gla_w_gk2_p[j], gla_b_gk2[j][None, :], gla_norm_g[j][None, :])
                cfg = dict(mode="gla", batch=nb, seq=ns, heads=b_heads, kd=b_key, vd=b_val)
            if states is None:
                x, s = _mixer(x, w_in, *aux, w_out, row(ln_g, 0), row(ln_b, 0), alpha, **cfg)
            else:
                proj = _linear(x, w_in, F32, ROW_TILE)
                o, s = _recurrence(proj, states[j], *aux, **cfg)
                x = _linear_res_ln(o, w_out, x, row(ln_g, 0), row(ln_b, 0), alpha, ROW_TILE)
            new.append(s)
            if ns % XATTN_TILE == 0:
                x, cls = _xattn(x, w_q_b[l], mem_ks, mem_vs, w_o_b[l], l, row(ln_g, 1), row(ln_b, 1),
                                router_wt, bias_col, alpha, batch=nb, seq=ns, heads=x_heads, tq=XATTN_TILE)
            else:
                q = _linear(x, w_q_b[l], F32, ROW_TILE)
                c = _attention(q, mem_ks, mem_vs, l, batch=nb, seq=ns, heads=x_heads)
                x = _linear_res_ln(c, w_o_b[l], x, row(ln_g, 1), row(ln_b, 1), alpha, ROW_TILE)
                cls = _router(x, router_wt, bias_col, ROW_TILE)
            x = _moe(x, cls[0], router_wt3, wg_b, wu_b, wd_b, l, row(ln_g, 2), row(ln_b, 2), alpha)
        return x.reshape(nb, ns, d), jnp.stack(new_a), jnp.stack(new_b)

    y_prompt, state_hgrn_prompt, state_gla_prompt = run_trunk(x_prompt, None, None, mem_k_p, mem_v_p)
    y_sample, state_hgrn_sample, state_gla_sample = run_trunk(
        x_sample, state_hgrn, state_gla, _interleaved_rows(cache_mem_k), _interleaved_rows(cache_mem_v))
    return (y_prompt, y_sample, state_hgrn_prompt, state_gla_prompt, mem_k_prompt, mem_v_prompt,
            state_hgrn_sample, state_gla_sample)
```

```python
import functools

import jax
import jax.numpy as jnp
from jax import lax
from jax.experimental import pallas as pl
from jax.experimental.pallas import tpu as pltpu

F32 = jnp.float32
BF16 = jnp.bfloat16
I32 = jnp.int32

N_GROUPS = 4
GROUP_SIZE = 4
GLA_GATE_NORMALIZER = 16.0
LN_EPS = 1e-5
RMS_EPS = 1e-6
GLA_RANK_PAD = 128

VMEM_LIMIT_BYTES = 56 * 1024 * 1024

ROW_TILE = 512
XATTN_TILE = 1024
REC_CHUNK = 64
MIXER_CHUNKS = 8
REC_SEQS = 8
REC_UNROLL = 4
ATTN_SEQS = 4
MOE_TILE = 256
MOE_COPY_BLOCK = 32


def _params(sem):
    return pltpu.CompilerParams(dimension_semantics=sem, vmem_limit_bytes=VMEM_LIMIT_BYTES)


def _dot(a, b):
    return jnp.dot(a, b, preferred_element_type=F32)


def _dot_nt(a, b):
    return lax.dot_general(a, b, (((1,), (1,)), ((), ())), preferred_element_type=F32)


def _dot_precise(a, b, dot=_dot):
    a_hi = a.astype(BF16)
    b_hi = b.astype(BF16)
    a_lo = (a - a_hi.astype(F32)).astype(BF16)
    b_lo = (b - b_hi.astype(F32)).astype(BF16)
    return dot(a_hi, b_hi) + dot(a_hi, b_lo) + dot(a_lo, b_hi)


def _dot_tn(a, b):
    return lax.dot_general(a, b, (((0,), (0,)), ((), ())), preferred_element_type=F32)


def _layer_norm(z, g, b):
    mu = jnp.mean(z, axis=-1, keepdims=True)
    zc = z - mu
    var = jnp.mean(zc * zc, axis=-1, keepdims=True)
    return zc * lax.rsqrt(var + LN_EPS) * g + b


def _sigmoid(x):
    return 0.5 * jnp.tanh(0.5 * x) + 0.5


def _log_sigmoid(x):
    return jnp.minimum(x, 0.0) - jnp.log(1.0 + jnp.exp(-jnp.abs(x)))


def _linear_kernel(x_ref, w_ref, o_ref):
    o_ref[...] = _dot(x_ref[...].astype(BF16), w_ref[...].astype(BF16)).astype(o_ref.dtype)


def _linear(x, w, out_dtype, tm):
    m, k = x.shape
    n = w.shape[1]
    return pl.pallas_call(
        _linear_kernel,
        out_shape=jax.ShapeDtypeStruct((m, n), out_dtype),
        grid=(m // tm,),
        in_specs=[pl.BlockSpec((tm, k), lambda i: (i, 0)),
                  pl.BlockSpec((k, n), lambda i: (0, 0))],
        out_specs=pl.BlockSpec((tm, n), lambda i: (i, 0)),
        compiler_params=_params(("arbitrary",)),
        name="linear",
    )(x, w)


def _kv_proj_kernel(x_ref, wk_ref, wv_ref, k_ref, v_ref, *, heads):
    x = x_ref[...].astype(BF16)
    tm, d = x.shape
    hd = d // heads
    nt = hd // 128
    for w_ref, o_ref in ((wk_ref, k_ref), (wv_ref, v_ref)):
        y = _dot(x, w_ref[...].astype(BF16))
        for h in range(heads):
            for t in range(nt):
                o_ref[pl.ds(t * heads + h, tm, stride=heads * nt), :] = y[:, h * hd + t * 128:h * hd + (t + 1) * 128]


def _kv_proj(mem, w_kv, tm, heads):
    r, d = mem.shape
    nl = w_kv.shape[0]
    out = jax.ShapeDtypeStruct((nl, r * d // 128, 128), F32)
    ospec = pl.BlockSpec((None, tm * d // 128, 128), lambda l, i: (l, i, 0))
    return pl.pallas_call(
        functools.partial(_kv_proj_kernel, heads=heads),
        out_shape=(out, out),
        grid=(nl, r // tm),
        in_specs=[pl.BlockSpec((tm, d), lambda l, i: (i, 0)),
                  pl.BlockSpec((None, d, d), lambda l, i: (l, 0, 0)),
                  pl.BlockSpec((None, d, d), lambda l, i: (l, 0, 1))],
        out_specs=(ospec, ospec),
        compiler_params=_params(("arbitrary", "arbitrary")),
        name="kv_proj",
    )(mem, w_kv, w_kv)


def _linear_res_ln_kernel(h_ref, w_ref, x_ref, g_ref, b_ref, o_ref, *, alpha):
    c = _dot(h_ref[...].astype(BF16), w_ref[...].astype(BF16))
    o_ref[...] = _layer_norm(alpha * x_ref[...] + c, g_ref[...], b_ref[...])


def _linear_res_ln(h, w, x, g, b, alpha, tm):
    m, k = h.shape
    d = w.shape[1]
    row = pl.BlockSpec((1, d), lambda i: (0, 0))
    return pl.pallas_call(
        functools.partial(_linear_res_ln_kernel, alpha=alpha),
        out_shape=jax.ShapeDtypeStruct((m, d), F32),
        grid=(m // tm,),
        in_specs=[pl.BlockSpec((tm, k), lambda i: (i, 0)),
                  pl.BlockSpec((k, d), lambda i: (0, 0)),
                  pl.BlockSpec((tm, d), lambda i: (i, 0)), row, row],
        out_specs=pl.BlockSpec((tm, d), lambda i: (i, 0)),
        compiler_params=_params(("arbitrary",)),
        name="linear_res_ln",
    )(h, w, x, g, b)


def _cumsum_rows(x):
    n = x.shape[0]
    row = lax.broadcasted_iota(I32, x.shape, 0)
    s = 1
    while s < n:
        x = x + jnp.where(row >= s, pltpu.roll(x, s, 0), 0.0)
        s *= 2
    return x


def _chunk_prepare(q, k, v, g):
    c, kd = q.shape
    vd = v.shape[1]
    b = _cumsum_rows(g)
    b_last = b[c - 1:c, :]
    b_mid = b[c // 2:c // 2 + 1, :]
    qa = q * jnp.exp(b - b_mid)
    ka = k * jnp.exp(b_mid - b)
    scores = _dot_nt(qa.astype(BF16), ka.astype(BF16))
    ri = lax.broadcasted_iota(I32, (c, c), 0)
    ci = lax.broadcasted_iota(I32, (c, c), 1)
    scores = jnp.where(ri >= ci, scores, 0.0).astype(BF16)
    decay = jnp.broadcast_to(jnp.exp(b_last), (kd, kd)).T
    if vd != kd:
        decay = jnp.concatenate([decay] * (vd // kd), axis=1)
    return ((qa * jnp.exp(b_mid)).astype(BF16), scores,
            (ka * jnp.exp(b_last - b_mid)).astype(BF16), v.astype(BF16), decay)


def _chunk_local(prep):
    _, scores, ks, vb, _ = prep
    return _dot(scores, vb), _dot_tn(ks, vb)


def _chunk_finish(prep, local, state):
    q_in, _, _, _, decay = prep
    o_local, increment = local
    return _dot(q_in, state.astype(BF16)) + o_local, state * decay + increment


def _gated_rmsnorm(o, gate, gain):
    o = o * lax.rsqrt(jnp.mean(o * o, axis=-1, keepdims=True) + RMS_EPS)
    return o * (0.5 * gain) * (gate * (jnp.tanh(0.5 * gate) + 1.0))


def _lower_bound(logits, lb_row):
    e = jnp.exp(logits - jnp.max(logits, axis=0, keepdims=True))
    return jnp.sum(e[:lb_row + 1], axis=0, keepdims=True) / jnp.sum(e, axis=0, keepdims=True)


def _gla_gates(lr, w_gk2, b_gk2):
    z = _dot_precise(lr, w_gk2) + b_gk2
    return _log_sigmoid(z) * (1.0 / GLA_GATE_NORMALIZER)


def _head_inputs(mode, proj_ref, rows, h, heads, kd, vd, lb, gk_all):
    hk = heads * kd
    hv = heads * vd
    v = proj_ref[rows, 2 * hk + h * vd:2 * hk + (h + 1) * vd]
    gate = proj_ref[rows, 2 * hk + hv + h * vd:2 * hk + hv + (h + 1) * vd]
    if mode == "hgrn":
        q = proj_ref[rows, h * kd:(h + 1) * kd]
        f = proj_ref[rows, hk + h * kd:hk + (h + 1) * kd]
        lbh = lb[:, h * kd:(h + 1) * kd]
        q = (0.5 * kd ** -0.5) * q * (jnp.tanh(0.5 * q) + 1.0)
        fg = 0.5 * (1.0 + lbh) + (0.5 * (1.0 - lbh)) * jnp.tanh(0.5 * f)
        return q, 1.0 - fg, v, jnp.log(fg), gate
    q = proj_ref[rows, h * kd:(h + 1) * kd] * (kd ** -0.5)
    k = proj_ref[rows, hk + h * kd:hk + (h + 1) * kd]
    return q, k, v, gk_all[:, h * kd:(h + 1) * kd], gate


def _rec_kernel(proj_ref, s0_ref, aux0_ref, aux1_ref, gain_ref, o_ref, s_ref, *,
                mode, heads, kd, vd, chunk, tb, nbatch, lb_row):
    @pl.when(pl.program_id(1) == 0)
    def _():
        s_ref[...] = s0_ref[...]

    hk = heads * kd
    hv = heads * vd
    gain = gain_ref[...]
    lb = _lower_bound(aux0_ref[...], lb_row) if mode == "hgrn" else None

    def one_batch(nb):
        for sc in range(tb // chunk):
            r0 = nb * tb + sc * chunk
            rows = pl.ds(r0, chunk) if isinstance(r0, int) else pl.ds(pl.multiple_of(r0, 8), chunk)
            gk_all = None
            if mode == "gla":
                lr = proj_ref[rows, 2 * hk + 2 * hv:2 * hk + 2 * hv + GLA_RANK_PAD]
                gk_all = _gla_gates(lr, aux0_ref[...], aux1_ref[...])
            inputs = [_head_inputs(mode, proj_ref, rows, h, heads, kd, vd, lb, gk_all) for h in range(heads)]
            preps = [_chunk_prepare(q, k, v, g) for q, k, v, g, _ in inputs]
            local = [_chunk_local(p) for p in preps]
            for h in range(heads):
                o, s_new = _chunk_finish(preps[h], local[h], s_ref[nb, h])
                s_ref[nb, h] = s_new
                o_ref[rows, h * vd:(h + 1) * vd] = _gated_rmsnorm(o, inputs[h][4], gain)

    if nbatch == 1:
        one_batch(0)
    else:
        def body(nb, carry):
            one_batch(nb)
            return carry
        lax.fori_loop(0, nbatch, body, 0, unroll=REC_UNROLL)


def _recurrence(proj, s0, aux0, aux1, gain, *, mode, batch, seq, heads, kd, vd, lb_row=0):
    n, width = proj.shape
    chunk = min(REC_CHUNK, seq)
    tb = min(2 * chunk, seq)
    nbatch = 1 if seq > tb else min(REC_SEQS, batch)
    nblk = seq // tb
    grid = (batch // nbatch, nblk)
    state_spec = pl.BlockSpec((nbatch, heads, kd, vd), lambda b, c: (b, 0, 0, 0))
    full2 = lambda a: pl.BlockSpec(a.shape, lambda b, c: (0, 0))
    kern = functools.partial(_rec_kernel, mode=mode, heads=heads, kd=kd, vd=vd, chunk=chunk,
                             tb=tb, nbatch=nbatch, lb_row=lb_row)
    return pl.pallas_call(
        kern,
        out_shape=(jax.ShapeDtypeStruct((n, heads * vd), F32),
                   jax.ShapeDtypeStruct((batch, heads, kd, vd), F32)),
        grid=grid,
        in_specs=[pl.BlockSpec((nbatch * tb, width), lambda b, c: (b * nblk + c, 0)), state_spec,
                  full2(aux0), full2(aux1), full2(gain)],
        out_specs=(pl.BlockSpec((nbatch * tb, heads * vd), lambda b, c: (b * nblk + c, 0)),
                   state_spec),
        compiler_params=_params(("arbitrary", "arbitrary")),
        name="recurrence_" + mode,
    )(proj, s0, aux0, aux1, gain)


def _mixer_kernel(x_ref, w_in_ref, aux0_ref, aux1_ref, gain_ref, w_out_ref, g_ref, b_ref,
                  y_ref, s_ref, proj_scr, o_scr, *, mode, heads, kd, vd, chunk, lb_row, alpha):
    @pl.when(pl.program_id(1) == 0)
    def _():
        s_ref[...] = jnp.zeros(s_ref.shape, F32)

    tb = x_ref.shape[0]
    hk = heads * kd
    hv = heads * vd
    gain = gain_ref[...]
    x = x_ref[...]
    xb = x.astype(BF16)
    group = 256 // kd
    seg_starts = (0, hk, 2 * hk, 2 * hk + hv)
    seg_widths = (kd, kd, vd, vd)

    def project(gi):
        for start, w in zip(seg_starts, seg_widths):
            cols = slice(start + gi * group * w, start + (gi + 1) * group * w)
            proj_scr[:, cols] = _dot(xb, w_in_ref[:, cols])

    lb = None
    gk_blocks = None
    if mode == "hgrn":
        lb = _lower_bound(aux0_ref[...], lb_row)
    else:
        lr = _dot(xb, w_in_ref[:, 2 * hk + 2 * hv:2 * hk + 2 * hv + GLA_RANK_PAD])
        gk_full = _gla_gates(lr, aux0_ref[...], aux1_ref[...])
        gk_blocks = [gk_full[sc * chunk:(sc + 1) * chunk, :] for sc in range(tb // chunk)]

    n_groups = heads // group
    project(0)
    for gi in range(n_groups):
        if gi + 1 < n_groups:
            project(gi + 1)
        units = [(h, sc) for h in range(gi * group, (gi + 1) * group) for sc in range(tb // chunk)]
        inputs = [_head_inputs(mode, proj_scr, pl.ds(sc * chunk, chunk), h, heads, kd, vd, lb,
                               None if gk_blocks is None else gk_blocks[sc]) for h, sc in units]
        preps = [_chunk_prepare(q, k, v, g) for q, k, v, g, _ in inputs]
        local = [_chunk_local(p) for p in preps]
        for u, (h, sc) in enumerate(units):
            o, s_new = _chunk_finish(preps[u], local[u], s_ref[0, h])
            s_ref[0, h] = s_new
            o_scr[pl.ds(sc * chunk, chunk), h * vd:(h + 1) * vd] = (
                _gated_rmsnorm(o, inputs[u][4], gain).astype(BF16))
    c = _dot(o_scr[...], w_out_ref[...])
    y_ref[...] = _layer_norm(alpha * x + c, g_ref[...], b_ref[...])


def _mixer(x, w_in, aux0, aux1, gain, w_out, g, b, alpha, *, mode, batch, seq, heads, kd, vd, lb_row=0):
    n, d = x.shape
    width = w_in.shape[1]
    tb = MIXER_CHUNKS * REC_CHUNK
    nblk = seq // tb
    const = lambda a: pl.BlockSpec(a.shape, lambda bb, c: (0, 0))
    rows = pl.BlockSpec((tb, d), lambda bb, c: (bb * nblk + c, 0))
    kern = functools.partial(_mixer_kernel, mode=mode, heads=heads, kd=kd, vd=vd, chunk=REC_CHUNK,
                             lb_row=lb_row, alpha=alpha)
    return pl.pallas_call(
        kern,
        out_shape=(jax.ShapeDtypeStruct((n, d), F32),
                   jax.ShapeDtypeStruct((batch, heads, kd, vd), F32)),
        grid=(batch, nblk),
        in_specs=[rows, const(w_in), const(aux0), const(aux1), const(gain), const(w_out), const(g), const(b)],
        out_specs=(rows, pl.BlockSpec((1, heads, kd, vd), lambda bb, c: (bb, 0, 0, 0))),
        scratch_shapes=[pltpu.VMEM((tb, width), F32), pltpu.VMEM((tb, heads * vd), BF16)],
        compiler_params=_params(("arbitrary", "arbitrary")),
        name="mixer_" + mode,
    )(x, w_in, aux0, aux1, gain, w_out, g, b)


def _attn_kernel(q_ref, k_ref, v_ref, o_ref, *, heads, nbatch, tq):
    hd = q_ref.shape[1] // heads
    scale = hd ** -0.5
    units = [(nb, h) for nb in range(nbatch) for h in range(heads)]

    def head_rows(ref, nb, h):
        nt = hd // 128
        period = heads * nt
        mlen = ref.shape[1] // period
        parts = [ref[nb, pl.ds(t * heads + h, mlen, stride=period), :] for t in range(nt)]
        return jnp.concatenate(parts, axis=1).astype(BF16)

    scores = [_dot_nt(q_ref[nb * tq:(nb + 1) * tq, h * hd:(h + 1) * hd].astype(BF16),
                      head_rows(k_ref, nb, h)) * scale for nb, h in units]
    probs = []
    for s in scores:
        p = jnp.exp(s - jnp.max(s, axis=-1, keepdims=True))
        probs.append((p / jnp.sum(p, axis=-1, keepdims=True)).astype(BF16))
    for (nb, h), p in zip(units, probs):
        o_ref[nb * tq:(nb + 1) * tq, h * hd:(h + 1) * hd] = _dot(p, head_rows(v_ref, nb, h)).astype(o_ref.dtype)


def _xattn_kernel(x_ref, wq_ref, k_ref, v_ref, wo_ref, g_ref, b_ref, rw_ref, rb_ref, y_ref, cls_ref, *,
                  heads, alpha):
    x = x_ref[...]
    hd = x.shape[1] // heads
    scale = hd ** -0.5
    q = _dot(x.astype(BF16), wq_ref[...].astype(BF16)).astype(BF16)
    nt = hd // 128
    period = heads * nt
    mlen = k_ref.shape[0] // period

    def head_rows(ref, h):
        parts = [ref[pl.ds(t * heads + h, mlen, stride=period), :] for t in range(nt)]
        return jnp.concatenate(parts, axis=1).astype(BF16)

    scores = [_dot_nt(q[:, h * hd:(h + 1) * hd], head_rows(k_ref, h)) * scale for h in range(heads)]
    probs = []
    for s in scores:
        p = jnp.exp(s - jnp.max(s, axis=-1, keepdims=True))
        probs.append((p / jnp.sum(p, axis=-1, keepdims=True)).astype(BF16))
    o = jnp.concatenate([_dot(p, head_rows(v_ref, h)).astype(BF16) for h, p in enumerate(probs)], axis=1)
    y = _layer_norm(alpha * x + _dot(o, wo_ref[...].astype(BF16)), g_ref[...], b_ref[...])
    y_ref[...] = y
    cls_ref[...] = _routing_class(y, rw_ref[...], rb_ref[...])


def _xattn(x, wq, mem_k, mem_v, wo, layer, g, b, router_wt, bias_col, alpha, *, batch, seq, heads, tq):
    n, d = x.shape
    nblk = seq // tq
    const = lambda a: pl.BlockSpec(a.shape, lambda bb, i: (0, 0))
    rows = pl.BlockSpec((tq, d), lambda bb, i: (bb * nblk + i, 0))
    mem = pl.BlockSpec((None, None) + mem_k.shape[2:], lambda bb, i: (layer, bb, 0, 0))
    return pl.pallas_call(
        functools.partial(_xattn_kernel, heads=heads, alpha=alpha),
        out_shape=(jax.ShapeDtypeStruct((n, d), F32), jax.ShapeDtypeStruct((1, n), I32)),
        grid=(batch, nblk),
        in_specs=[rows, const(wq), mem, mem, const(wo), const(g), const(b), const(router_wt), const(bias_col)],
        out_specs=(rows, pl.BlockSpec((1, tq), lambda bb, i: (0, bb * nblk + i))),
        compiler_params=_params(("arbitrary", "arbitrary")),
        name="xattn_block",
    )(x, wq, mem_k, mem_v, wo, g, b, router_wt, bias_col)


def _interleaved_rows(mem):
    nl, b, m, heads, hd = mem.shape
    nt = hd // 128
    return mem.reshape(nl, b, m, heads, nt, 128).transpose(0, 1, 2, 4, 3, 5).reshape(nl, b, m * nt * heads, 128)


def _attention(q, mem_k, mem_v, layer, *, batch, seq, heads):
    n, d = q.shape
    nbatch = ATTN_SEQS
    mem_spec = pl.BlockSpec((None, nbatch) + mem_k.shape[2:], lambda b: (layer, b, 0, 0))
    qspec = pl.BlockSpec((nbatch * seq, d), lambda b: (b, 0))
    return pl.pallas_call(
        functools.partial(_attn_kernel, heads=heads, nbatch=nbatch, tq=seq),
        out_shape=jax.ShapeDtypeStruct((n, d), q.dtype),
        grid=(batch // nbatch,),
        in_specs=[qspec, mem_spec, mem_spec],
        out_specs=qspec,
        compiler_params=_params(("arbitrary",)),
        name="mem_attention",
    )(q, mem_k, mem_v)


def _routing_class(x, wt, bias_col):
    logits = _dot_precise(wt, x, _dot_nt)
    sel = jax.nn.sigmoid(logits) + bias_col
    rows = [sel[e:e + 1, :] for e in range(N_GROUPS * GROUP_SIZE)]

    def first_argmax(vals):
        best_v, best_i = vals[0], jnp.zeros(vals[0].shape, I32)
        for i in range(1, len(vals)):
            better = vals[i] > best_v
            best_i = jnp.where(better, i, best_i)
            best_v = jnp.where(better, vals[i], best_v)
        return best_i

    group_scores = []
    for gi in range(N_GROUPS):
        a = rows[gi * GROUP_SIZE:(gi + 1) * GROUP_SIZE]
        top2 = None
        for i in range(GROUP_SIZE):
            for j in range(i + 1, GROUP_SIZE):
                s = a[i] + a[j]
                top2 = s if top2 is None else jnp.maximum(top2, s)
        group_scores.append(top2)
    best = first_argmax(group_scores)
    cand = []
    for j in range(GROUP_SIZE):
        cj = rows[j]
        for gi in range(1, N_GROUPS):
            cj = jnp.where(best == gi, rows[gi * GROUP_SIZE + j], cj)
        cand.append(cj)
    i1 = first_argmax(cand)
    i2 = first_argmax([jnp.where(i1 == j, -jnp.inf, cand[j]) for j in range(GROUP_SIZE)])
    lo = jnp.minimum(i1, i2)
    hi = jnp.maximum(i1, i2)
    pair = jnp.where(lo == 0, hi - 1, jnp.where(lo == 1, 6 - hi, 5))
    return best * 6 + pair


def _router_kernel(x_ref, wt_ref, bias_ref, cls_ref):
    cls_ref[...] = _routing_class(x_ref[...], wt_ref[...], bias_ref[...])


def _router(x, router_wt, bias_col, tm):
    n, d = x.shape
    ne = router_wt.shape[0]
    return pl.pallas_call(
        _router_kernel,
        out_shape=jax.ShapeDtypeStruct((1, n), I32),
        grid=(n // tm,),
        in_specs=[pl.BlockSpec((tm, d), lambda i: (i, 0)),
                  pl.BlockSpec((ne, d), lambda i: (0, 0)),
                  pl.BlockSpec((ne, 1), lambda i: (0, 0))],
        out_specs=pl.BlockSpec((1, tm), lambda i: (0, i)),
        compiler_params=_params(("arbitrary",)),
        name="router",
    )(x, router_wt, bias_col)


_PAIR_EXPERTS = ((0, 1), (0, 2), (0, 3), (1, 3), (1, 2), (3, 2))


def _moe_plan(cls, tile):
    n = cls.shape[0]
    ncls = N_GROUPS * 6
    max_tiles = n // tile + ncls
    shift = max(n - 1, 1).bit_length()
    keys = jnp.sort(cls * (1 << shift) + jnp.arange(n, dtype=I32))
    src = keys & ((1 << shift) - 1)
    cid = jnp.arange(ncls, dtype=I32)
    count = jnp.sum((cls[None, :] == cid[:, None]).astype(I32), axis=1)
    cstart = jnp.cumsum(count) - count
    ntile = (count + tile - 1) // tile
    tend = jnp.cumsum(ntile)
    tid = jnp.arange(max_tiles, dtype=I32)
    n_used = tend[-1]
    tcls = jnp.sum((tid[:, None] >= tend[None, :]).astype(I32), axis=1)
    last_cls = jnp.sum((n_used - 1 >= tend).astype(I32))
    tcls = jnp.where(tid < n_used, tcls, last_cls)
    within = tid - (tend - ntile)[tcls]
    tstart = cstart[tcls] + within * tile
    tcnt = jnp.clip(count[tcls] - within * tile, 0, tile)
    tcnt = jnp.where(tid < n_used, tcnt, 0)
    grp = tcls // 6
    e_lo = grp * GROUP_SIZE + jnp.asarray([p[0] for p in _PAIR_EXPERTS], I32)[tcls % 6]
    e_hi = grp * GROUP_SIZE + jnp.asarray([p[1] for p in _PAIR_EXPERTS], I32)[tcls % 6]
    return src, tstart, tcnt, e_lo, e_hi, n_used.reshape(1)


def _moe_kernel(src_ref, tstart_ref, tcnt_ref, elo_ref, ehi_ref, nused_ref,
                x_hbm, rw_lo_ref, rw_hi_ref, wg_lo_ref, wu_lo_ref, wd_lo_ref,
                wg_hi_ref, wu_hi_ref, wd_hi_ref, g_ref, b_ref,
                out_hbm, xbuf, ybuf, gsem, ssem, *, alpha):
    j = pl.program_id(0)
    n_used = nused_ref[0]
    parity = lax.rem(j, 2)
    ngroups, _, d = xbuf.shape[1:]
    block = MOE_COPY_BLOCK

    def for_rows(cnt, fn):
        nblk = lax.shift_right_logical(cnt, block.bit_length() - 1)
        for blk in range(ngroups * 8 // block):
            @pl.when(blk < nblk)
            def _():
                for u in range(block):
                    r = blk * block + u
                    fn(r // 8, r % 8, r, u % 2)

        def single(r, c):
            fn(lax.shift_right_logical(r, 3), jnp.bitwise_and(r, 7), r, 0)
            return c

        lax.fori_loop(nblk * block, cnt, single, 0)

    def wait_rows(cnt, group_copy, row_copy):
        ngrp = lax.shift_right_logical(cnt, 3)

        @pl.when(ngrp > 0)
        def _():
            group_copy(ngrp).wait()

        def single(r, c):
            row_copy(lax.shift_right_logical(r, 3), jnp.bitwise_and(r, 7)).wait()
            return c
        lax.fori_loop(ngrp * 8, cnt, single, 0)

    def start_gather(t, s):
        start = tstart_ref[t]

        def fn(gi, u, r, prio):
            tok = src_ref[start + r]
            pltpu.make_async_copy(x_hbm.at[pl.ds(tok, 1)], xbuf.at[s, gi, pl.ds(u, 1)],
                                  gsem.at[s]).start(priority=prio)
        for_rows(tcnt_ref[t], fn)

    def wait_gather(t, s):
        wait_rows(tcnt_ref[t],
                  lambda n: pltpu.make_async_copy(xbuf.at[s, pl.ds(0, n)], xbuf.at[s, pl.ds(0, n)], gsem.at[s]),
                  lambda gi, u: pltpu.make_async_copy(xbuf.at[s, gi, pl.ds(u, 1)], xbuf.at[s, gi, pl.ds(u, 1)],
                                                      gsem.at[s]))

    def start_scatter(t, s):
        start = tstart_ref[t]

        def fn(gi, u, r, prio):
            tok = src_ref[start + r]
            pltpu.make_async_copy(ybuf.at[s, gi, pl.ds(u, 1)], out_hbm.at[pl.ds(tok, 1)],
                                  ssem.at[s]).start(priority=prio)
        for_rows(tcnt_ref[t], fn)

    def wait_scatter(t, s):
        wait_rows(tcnt_ref[t],
                  lambda n: pltpu.make_async_copy(ybuf.at[s, pl.ds(0, n)], ybuf.at[s, pl.ds(0, n)], ssem.at[s]),
                  lambda gi, u: pltpu.make_async_copy(ybuf.at[s, gi, pl.ds(u, 1)], ybuf.at[s, gi, pl.ds(u, 1)],
                                                      ssem.at[s]))

    @pl.when(j == 0)
    def _():
        xbuf[...] = jnp.zeros(xbuf.shape, F32)
        start_gather(0, 0)

    def step(slot):
        @pl.when(j + 1 < n_used)
        def _():
            start_gather(j + 1, 1 - slot)

        wait_gather(j, slot)

        @pl.when(j >= 2)
        def _():
            wait_scatter(j - 2, slot)

        x = xbuf[slot].reshape(ngroups * 8, d)
        xb = x.astype(BF16)

        def expert(rw_ref, wg_ref, wu_ref, wd_ref):
            h = _dot(xb, wg_ref[...].astype(BF16))
            h = h * _sigmoid(h) * _dot(xb, wu_ref[...].astype(BF16))
            y = _dot(h.astype(BF16), wd_ref[...].astype(BF16))
            score = _sigmoid(jnp.sum(x * rw_ref[...], axis=-1, keepdims=True))
            return y, score

        y_lo, s_lo = expert(rw_lo_ref, wg_lo_ref, wu_lo_ref, wd_lo_ref)
        y_hi, s_hi = expert(rw_hi_ref, wg_hi_ref, wu_hi_ref, wd_hi_ref)
        tot = s_lo + s_hi
        m = (s_lo / tot) * y_lo + (s_hi / tot) * y_hi
        ybuf[slot] = _layer_norm(alpha * x + m, g_ref[...], b_ref[...]).reshape(ngroups, 8, d)
        start_scatter(j, slot)

        @pl.when(j == n_used - 1)
        def _():
            @pl.when(j >= 1)
            def _():
                wait_scatter(j - 1, 1 - slot)
            wait_scatter(j, slot)

    for s in range(2):
        pl.when(jnp.logical_and(j < n_used, parity == s))(functools.partial(step, s))


def _moe(x, cls, router_wt3, wg, wu, wd, layer, g, b, alpha):
    n, d = x.shape
    de = wg.shape[3]
    tile = MOE_TILE if n >= 32 * MOE_TILE else MOE_TILE // 4
    src, tstart, tcnt, e_lo, e_hi, n_used = _moe_plan(cls, tile)
    max_tiles = tstart.shape[0]

    def by_lo(shape):
        return pl.BlockSpec(shape, lambda j, s, ts, tc, el, eh, nu: (el[j], 0, 0))

    def by_hi(shape):
        return pl.BlockSpec(shape, lambda j, s, ts, tc, el, eh, nu: (eh[j], 0, 0))

    def w_lo(r, c):
        return pl.BlockSpec((None, None, r, c), lambda j, s, ts, tc, el, eh, nu: (layer, el[j], 0, 0))

    def w_hi(r, c):
        return pl.BlockSpec((None, None, r, c), lambda j, s, ts, tc, el, eh, nu: (layer, eh[j], 0, 0))

    row = pl.BlockSpec((1, d), lambda j, s, ts, tc, el, eh, nu: (0, 0))
    grid_spec = pltpu.PrefetchScalarGridSpec(
        num_scalar_prefetch=6,
        grid=(max_tiles,),
        in_specs=[pl.BlockSpec(memory_space=pl.ANY),
                  by_lo((None, 1, d)), by_hi((None, 1, d)),
                  w_lo(d, de), w_lo(d, de), w_lo(de, d),
                  w_hi(d, de), w_hi(d, de), w_hi(de, d),
                  row, row],
        out_specs=pl.BlockSpec(memory_space=pl.ANY),
        scratch_shapes=[pltpu.VMEM((2, tile // 8, 8, d), F32), pltpu.VMEM((2, tile // 8, 8, d), F32),
                        pltpu.SemaphoreType.DMA((2,)), pltpu.SemaphoreType.DMA((2,))],
    )
    return pl.pallas_call(
        functools.partial(_moe_kernel, alpha=alpha),
        out_shape=jax.ShapeDtypeStruct((n, d), F32),
        grid_spec=grid_spec,
        compiler_params=_params(("arbitrary",)),
        name="grouped_moe",
    )(src, tstart, tcnt, e_lo, e_hi, n_used,
      x, router_wt3, router_wt3, wg, wu, wd, wg, wu, wd, g, b)


def kernel(x_prompt, x_sample, state_hgrn, state_gla, cache_mem_k, cache_mem_v, mem_prompt, hgrn_w_in, hgrn_lb_logits, hgrn_norm_g, hgrn_w_out, gla_w_in, gla_w_gk2, gla_b_gk2, gla_norm_g, gla_w_out, xattn_w_q, xattn_w_kv, xattn_w_o, router_w, router_bias, moe_w_gate, moe_w_up, moe_w_down, ln_g, ln_b):
    batch, seq, d = x_prompt.shape
    dec_batch, dec_seq, _ = x_sample.shape
    depth = ln_g.shape[0]
    alpha = (2 * depth) ** 0.25
    a_heads, a_key, a_val = state_hgrn.shape[2:]
    b_heads, b_key, b_val = state_gla.shape[2:]
    mem_len, x_heads = cache_mem_k.shape[2], cache_mem_k.shape[3]
    rank = gla_w_gk2.shape[1]
    b_main = 2 * b_heads * b_key + b_heads * b_val + d

    hgrn_w_in_b = hgrn_w_in.astype(BF16)
    hgrn_w_out_b = hgrn_w_out.astype(BF16)
    gla_w_in_b = jnp.concatenate(
        [gla_w_in, jnp.zeros(gla_w_in.shape[:2] + (GLA_RANK_PAD - rank,), F32)], axis=-1).astype(BF16)
    gla_w_gk2_p = jnp.concatenate(
        [gla_w_gk2, jnp.zeros((gla_w_gk2.shape[0], GLA_RANK_PAD - rank, gla_w_gk2.shape[2]), F32)], axis=1)
    gla_w_out_b = gla_w_out.astype(BF16)
    w_q_b, w_o_b = xattn_w_q, xattn_w_o
    wg_b, wu_b, wd_b = moe_w_gate, moe_w_up, moe_w_down
    router_wt = router_w.T
    router_wt3 = router_wt[:, None, :]
    bias_col = router_bias[:, None]
    assert b_main + rank == gla_w_in.shape[2]

    mem_k_p, mem_v_p = _kv_proj(mem_prompt.reshape(batch * mem_len, d), xattn_w_kv, ROW_TILE, x_heads)
    rows_per_seq = mem_len * d // 128
    mem_k_p = mem_k_p.reshape(depth, batch, rows_per_seq, 128)
    mem_v_p = mem_v_p.reshape(depth, batch, rows_per_seq, 128)

    def head_major(mem):
        nt = d // x_heads // 128
        return mem.reshape(depth, batch, mem_len, nt, x_heads, 128).transpose(0, 1, 2, 4, 3, 5).reshape(
            depth, batch, mem_len, x_heads, d // x_heads)
    mem_k_prompt = head_major(mem_k_p)
    mem_v_prompt = head_major(mem_v_p)

    def run_trunk(x3, states_a, states_b, mem_ks, mem_vs):
        nb, ns, _ = x3.shape
        x = x3.reshape(nb * ns, d)
        new_a, new_b = [], []
        for l in range(depth):
            j = l // 2
            row = lambda a, i: a[l, i][None, :]
            if l % 2 == 0:
                w_in, w_out, states, new = hgrn_w_in_b[j], hgrn_w_out_b[j], states_a, new_a
                aux = (hgrn_lb_logits, hgrn_lb_logits[:1], hgrn_norm_g[j][None, :])
                cfg = dict(mode="hgrn", batch=nb, seq=ns, heads=a_heads, kd=a_key, vd=a_val, lb_row=l)
            else:
                w_in, w_out, states, new = gla_w_in_b[j], gla_w_out_b[j], states_b, new_b
                aux = (gla_w_gk2_p[j], gla_b_gk2[j][None, :], gla_norm_g[j][None, :])
                cfg = dict(mode="gla", batch=nb, seq=ns, heads=b_heads, kd=b_key, vd=b_val)
            if states is None:
                x, s = _mixer(x, w_in, *aux, w_out, row(ln_g, 0), row(ln_b, 0), alpha, **cfg)
            else:
                proj = _linear(x, w_in, F32, ROW_TILE)
                o, s = _recurrence(proj, states[j], *aux, **cfg)
                x = _linear_res_ln(o, w_out, x, row(ln_g, 0), row(ln_b, 0), alpha, ROW_TILE)
            new.append(s)
            if ns % XATTN_TILE == 0:
                x, cls = _xattn(x, w_q_b[l], mem_ks, mem_vs, w_o_b[l], l, row(ln_g, 1), row(ln_b, 1),
                                router_wt, bias_col, alpha, batch=nb, seq=ns, heads=x_heads, tq=XATTN_TILE)
            else:
                q = _linear(x, w_q_b[l], F32, ROW_TILE)
                c = _attention(q, mem_ks, mem_vs, l, batch=nb, seq=ns, heads=x_heads)
                x = _linear_res_ln(c, w_o_b[l], x, row(ln_g, 1), row(ln_b, 1), alpha, ROW_TILE)
                cls = _router(x, router_wt, bias_col, ROW_TILE)
            x = _moe(x, cls[0], router_wt3, wg_b, wu_b, wd_b, l, row(ln_g, 2), row(ln_b, 2), alpha)
        return x.reshape(nb, ns, d), jnp.stack(new_a), jnp.stack(new_b)

    y_prompt, state_hgrn_prompt, state_gla_prompt = run_trunk(x_prompt, None, None, mem_k_p, mem_v_p)
    y_sample, state_hgrn_sample, state_gla_sample = run_trunk(
        x_sample, state_hgrn, state_gla, _interleaved_rows(cache_mem_k), _interleaved_rows(cache_mem_v))
    return (y_prompt, y_sample, state_hgrn_prompt, state_gla_prompt, mem_k_prompt, mem_v_prompt,
            state_hgrn_sample, state_gla_sample)
```

```python
import functools

import jax
import jax.numpy as jnp
from jax import lax
from jax.experimental import pallas as pl
from jax.experimental.pallas import tpu as pltpu

F32 = jnp.float32
BF16 = jnp.bfloat16
I32 = jnp.int32

N_GROUPS = 4
GROUP_SIZE = 4
GLA_GATE_NORMALIZER = 16.0
LN_EPS = 1e-5
RMS_EPS = 1e-6
GLA_RANK_PAD = 128

VMEM_LIMIT_BYTES = 56 * 1024 * 1024

ROW_TILE = 512
XATTN_TILE = 1024
REC_CHUNK = 64
MIXER_CHUNKS = 8
REC_SEQS = 8
REC_UNROLL = 8
ATTN_SEQS = 8
MOE_TILE = 256
MOE_COPY_BLOCK = 32


def _params(sem):
    return pltpu.CompilerParams(dimension_semantics=sem, vmem_limit_bytes=VMEM_LIMIT_BYTES)


def _dot(a, b):
    return jnp.dot(a, b, preferred_element_type=F32)


def _dot_nt(a, b):
    return lax.dot_general(a, b, (((1,), (1,)), ((), ())), preferred_element_type=F32)


def _dot_precise(a, b, dot=_dot):
    a_hi = a.astype(BF16)
    b_hi = b.astype(BF16)
    a_lo = (a - a_hi.astype(F32)).astype(BF16)
    b_lo = (b - b_hi.astype(F32)).astype(BF16)
    return dot(a_hi, b_hi) + dot(a_hi, b_lo) + dot(a_lo, b_hi)


def _dot_tn(a, b):
    return lax.dot_general(a, b, (((0,), (0,)), ((), ())), preferred_element_type=F32)


def _layer_norm(z, g, b):
    mu = jnp.mean(z, axis=-1, keepdims=True)
    zc = z - mu
    var = jnp.mean(zc * zc, axis=-1, keepdims=True)
    return zc * lax.rsqrt(var + LN_EPS) * g + b


def _sigmoid(x):
    return 0.5 * jnp.tanh(0.5 * x) + 0.5


def _log_sigmoid(x):
    return jnp.minimum(x, 0.0) - jnp.log(1.0 + jnp.exp(-jnp.abs(x)))


def _linear_kernel(x_ref, w_ref, o_ref):
    o_ref[...] = _dot(x_ref[...].astype(BF16), w_ref[...].astype(BF16)).astype(o_ref.dtype)


def _linear(x, w, out_dtype, tm):
    m, k = x.shape
    n = w.shape[1]
    return pl.pallas_call(
        _linear_kernel,
        out_shape=jax.ShapeDtypeStruct((m, n), out_dtype),
        grid=(m // tm,),
        in_specs=[pl.BlockSpec((tm, k), lambda i: (i, 0)),
                  pl.BlockSpec((k, n), lambda i: (0, 0))],
        out_specs=pl.BlockSpec((tm, n), lambda i: (i, 0)),
        compiler_params=_params(("arbitrary",)),
        name="linear",
    )(x, w)


def _kv_proj_kernel(x_ref, wk_ref, wv_ref, k_ref, v_ref, *, heads):
    x = x_ref[...].astype(BF16)
    tm, d = x.shape
    hd = d // heads
    nt = hd // 128
    for w_ref, o_ref in ((wk_ref, k_ref), (wv_ref, v_ref)):
        y = _dot(x, w_ref[...].astype(BF16))
        for h in range(heads):
            for t in range(nt):
                o_ref[pl.ds(t * heads + h, tm, stride=heads * nt), :] = y[:, h * hd + t * 128:h * hd + (t + 1) * 128]


def _kv_proj(mem, w_kv, tm, heads):
    r, d = mem.shape
    nl = w_kv.shape[0]
    out = jax.ShapeDtypeStruct((nl, r * d // 128, 128), F32)
    ospec = pl.BlockSpec((None, tm * d // 128, 128), lambda l, i: (l, i, 0))
    return pl.pallas_call(
        functools.partial(_kv_proj_kernel, heads=heads),
        out_shape=(out, out),
        grid=(nl, r // tm),
        in_specs=[pl.BlockSpec((tm, d), lambda l, i: (i, 0)),
                  pl.BlockSpec((None, d, d), lambda l, i: (l, 0, 0)),
                  pl.BlockSpec((None, d, d), lambda l, i: (l, 0, 1))],
        out_specs=(ospec, ospec),
        compiler_params=_params(("arbitrary", "arbitrary")),
        name="kv_proj",
    )(mem, w_kv, w_kv)


def _linear_res_ln_kernel(h_ref, w_ref, x_ref, g_ref, b_ref, o_ref, *, alpha):
    c = _dot(h_ref[...].astype(BF16), w_ref[...].astype(BF16))
    o_ref[...] = _layer_norm(alpha * x_ref[...] + c, g_ref[...], b_ref[...])


def _linear_res_ln(h, w, x, g, b, alpha, tm):
    m, k = h.shape
    d = w.shape[1]
    row = pl.BlockSpec((1, d), lambda i: (0, 0))
    return pl.pallas_call(
        functools.partial(_linear_res_ln_kernel, alpha=alpha),
        out_shape=jax.ShapeDtypeStruct((m, d), F32),
        grid=(m // tm,),
        in_specs=[pl.BlockSpec((tm, k), lambda i: (i, 0)),
                  pl.BlockSpec((k, d), lambda i: (0, 0)),
                  pl.BlockSpec((tm, d), lambda i: (i, 0)), row, row],
        out_specs=pl.BlockSpec((tm, d), lambda i: (i, 0)),
        compiler_params=_params(("arbitrary",)),
        name="linear_res_ln",
    )(h, w, x, g, b)


def _cumsum_rows(x):
    n = x.shape[0]
    row = lax.broadcasted_iota(I32, x.shape, 0)
    s = 1
    while s < n:
        x = x + jnp.where(row >= s, pltpu.roll(x, s, 0), 0.0)
        s *= 2
    return x


def _chunk_prepare(q, k, v, g):
    c, kd = q.shape
    vd = v.shape[1]
    b = _cumsum_rows(g)
    b_last = b[c - 1:c, :]
    b_mid = b[c // 2:c // 2 + 1, :]
    qa = q * jnp.exp(b - b_mid)
    ka = k * jnp.exp(b_mid - b)
    scores = _dot_nt(qa.astype(BF16), ka.astype(BF16))
    ri = lax.broadcasted_iota(I32, (c, c), 0)
    ci = lax.broadcasted_iota(I32, (c, c), 1)
    scores = jnp.where(ri >= ci, scores, 0.0).astype(BF16)
    decay = jnp.broadcast_to(jnp.exp(b_last), (kd, kd)).T
    if vd != kd:
        decay = jnp.concatenate([decay] * (vd // kd), axis=1)
    return ((qa * jnp.exp(b_mid)).astype(BF16), scores,
            (ka * jnp.exp(b_last - b_mid)).astype(BF16), v.astype(BF16), decay)


def _chunk_local(prep):
    _, scores, ks, vb, _ = prep
    return _dot(scores, vb), _dot_tn(ks, vb)


def _chunk_finish(prep, local, state):
    q_in, _, _, _, decay = prep
    o_local, increment = local
    return _dot(q_in, state.astype(BF16)) + o_local, state * decay + increment


def _gated_rmsnorm(o, gate, gain):
    o = o * lax.rsqrt(jnp.mean(o * o, axis=-1, keepdims=True) + RMS_EPS)
    return o * (0.5 * gain) * (gate * (jnp.tanh(0.5 * gate) + 1.0))


def _lower_bound(logits, lb_row):
    e = jnp.exp(logits - jnp.max(logits, axis=0, keepdims=True))
    return jnp.sum(e[:lb_row + 1], axis=0, keepdims=True) / jnp.sum(e, axis=0, keepdims=True)


def _gla_gates(lr, w_gk2, b_gk2):
    z = _dot_precise(lr, w_gk2) + b_gk2
    return _log_sigmoid(z) * (1.0 / GLA_GATE_NORMALIZER)


def _head_inputs(mode, proj_ref, rows, h, heads, kd, vd, lb, gk_all):
    hk = heads * kd
    hv = heads * vd
    v = proj_ref[rows, 2 * hk + h * vd:2 * hk + (h + 1) * vd]
    gate = proj_ref[rows, 2 * hk + hv + h * vd:2 * hk + hv + (h + 1) * vd]
    if mode == "hgrn":
        q = proj_ref[rows, h * kd:(h + 1) * kd]
        f = proj_ref[rows, hk + h * kd:hk + (h + 1) * kd]
        lbh = lb[:, h * kd:(h + 1) * kd]
        q = (0.5 * kd ** -0.5) * q * (jnp.tanh(0.5 * q) + 1.0)
        fg = 0.5 * (1.0 + lbh) + (0.5 * (1.0 - lbh)) * jnp.tanh(0.5 * f)
        return q, 1.0 - fg, v, jnp.log(fg), gate
    q = proj_ref[rows, h * kd:(h + 1) * kd] * (kd ** -0.5)
    k = proj_ref[rows, hk + h * kd:hk + (h + 1) * kd]
    return q, k, v, gk_all[:, h * kd:(h + 1) * kd], gate


def _rec_kernel(proj_ref, s0_ref, aux0_ref, aux1_ref, gain_ref, o_ref, s_ref, *,
                mode, heads, kd, vd, chunk, tb, nbatch, lb_row):
    @pl.when(pl.program_id(1) == 0)
    def _():
        s_ref[...] = s0_ref[...]

    hk = heads * kd
    hv = heads * vd
    gain = gain_ref[...]
    lb = _lower_bound(aux0_ref[...], lb_row) if mode == "hgrn" else None

    def one_batch(nb):
        for sc in range(tb // chunk):
            r0 = nb * tb + sc * chunk
            rows = pl.ds(r0, chunk) if isinstance(r0, int) else pl.ds(pl.multiple_of(r0, 8), chunk)
            gk_all = None
            if mode == "gla":
                lr = proj_ref[rows, 2 * hk + 2 * hv:2 * hk + 2 * hv + GLA_RANK_PAD]
                gk_all = _gla_gates(lr, aux0_ref[...], aux1_ref[...])
            inputs = [_head_inputs(mode, proj_ref, rows, h, heads, kd, vd, lb, gk_all) for h in range(heads)]
            preps = [_chunk_prepare(q, k, v, g) for q, k, v, g, _ in inputs]
            local = [_chunk_local(p) for p in preps]
            for h in range(heads):
                o, s_new = _chunk_finish(preps[h], local[h], s_ref[nb, h])
                s_ref[nb, h] = s_new
                o_ref[rows, h * vd:(h + 1) * vd] = _gated_rmsnorm(o, inputs[h][4], gain)

    if nbatch == 1:
        one_batch(0)
    else:
        def body(nb, carry):
            one_batch(nb)
            return carry
        lax.fori_loop(0, nbatch, body, 0, unroll=REC_UNROLL)


def _recurrence(proj, s0, aux0, aux1, gain, *, mode, batch, seq, heads, kd, vd, lb_row=0):
    n, width = proj.shape
    chunk = min(REC_CHUNK, seq)
    tb = min(2 * chunk, seq)
    nbatch = 1 if seq > tb else min(REC_SEQS, batch)
    nblk = seq // tb
    grid = (batch // nbatch, nblk)
    state_spec = pl.BlockSpec((nbatch, heads, kd, vd), lambda b, c: (b, 0, 0, 0))
    full2 = lambda a: pl.BlockSpec(a.shape, lambda b, c: (0, 0))
    kern = functools.partial(_rec_kernel, mode=mode, heads=heads, kd=kd, vd=vd, chunk=chunk,
                             tb=tb, nbatch=nbatch, lb_row=lb_row)
    return pl.pallas_call(
        kern,
        out_shape=(jax.ShapeDtypeStruct((n, heads * vd), F32),
                   jax.ShapeDtypeStruct((batch, heads, kd, vd), F32)),
        grid=grid,
        in_specs=[pl.BlockSpec((nbatch * tb, width), lambda b, c: (b * nblk + c, 0)), state_spec,
                  full2(aux0), full2(aux1), full2(gain)],
        out_specs=(pl.BlockSpec((nbatch * tb, heads * vd), lambda b, c: (b * nblk + c, 0)),
                   state_spec),
        compiler_params=_params(("arbitrary", "arbitrary")),
        name="recurrence_" + mode,
    )(proj, s0, aux0, aux1, gain)


def _mixer_kernel(x_ref, w_in_ref, aux0_ref, aux1_ref, gain_ref, w_out_ref, g_ref, b_ref,
                  y_ref, s_ref, proj_scr, o_scr, *, mode, heads, kd, vd, chunk, lb_row, alpha):
    @pl.when(pl.program_id(1) == 0)
    def _():
        s_ref[...] = jnp.zeros(s_ref.shape, F32)

    tb = x_ref.shape[0]
    hk = heads * kd
    hv = heads * vd
    gain = gain_ref[...]
    x = x_ref[...]
    xb = x.astype(BF16)
    group = 256 // kd
    seg_starts = (0, hk, 2 * hk, 2 * hk + hv)
    seg_widths = (kd, kd, vd, vd)

    def project(gi):
        for start, w in zip(seg_starts, seg_widths):
            cols = slice(start + gi * group * w, start + (gi + 1) * group * w)
            proj_scr[:, cols] = _dot(xb, w_in_ref[:, cols])

    lb = None
    gk_blocks = None
    if mode == "hgrn":
        lb = _lower_bound(aux0_ref[...], lb_row)
    else:
        lr = _dot(xb, w_in_ref[:, 2 * hk + 2 * hv:2 * hk + 2 * hv + GLA_RANK_PAD])
        gk_full = _gla_gates(lr, aux0_ref[...], aux1_ref[...])
        gk_blocks = [gk_full[sc * chunk:(sc + 1) * chunk, :] for sc in range(tb // chunk)]

    n_groups = heads // group
    project(0)
    for gi in range(n_groups):
        if gi + 1 < n_groups:
            project(gi + 1)
        units = [(h, sc) for h in range(gi * group, (gi + 1) * group) for sc in range(tb // chunk)]
        inputs = [_head_inputs(mode, proj_scr, pl.ds(sc * chunk, chunk), h, heads, kd, vd, lb,
                               None if gk_blocks is None else gk_blocks[sc]) for h, sc in units]
        preps = [_chunk_prepare(q, k, v, g) for q, k, v, g, _ in inputs]
        local = [_chunk_local(p) for p in preps]
        for u, (h, sc) in enumerate(units):
            o, s_new = _chunk_finish(preps[u], local[u], s_ref[0, h])
            s_ref[0, h] = s_new
            o_scr[pl.ds(sc * chunk, chunk), h * vd:(h + 1) * vd] = (
                _gated_rmsnorm(o, inputs[u][4], gain).astype(BF16))
    c = _dot(o_scr[...], w_out_ref[...])
    y_ref[...] = _layer_norm(alpha * x + c, g_ref[...], b_ref[...])


def _mixer(x, w_in, aux0, aux1, gain, w_out, g, b, alpha, *, mode, batch, seq, heads, kd, vd, lb_row=0):
    n, d = x.shape
    width = w_in.shape[1]
    tb = MIXER_CHUNKS * REC_CHUNK
    nblk = seq // tb
    const = lambda a: pl.BlockSpec(a.shape, lambda bb, c: (0, 0))
    rows = pl.BlockSpec((tb, d), lambda bb, c: (bb * nblk + c, 0))
    kern = functools.partial(_mixer_kernel, mode=mode, heads=heads, kd=kd, vd=vd, chunk=REC_CHUNK,
                             lb_row=lb_row, alpha=alpha)
    return pl.pallas_call(
        kern,
        out_shape=(jax.ShapeDtypeStruct((n, d), F32),
                   jax.ShapeDtypeStruct((batch, heads, kd, vd), F32)),
        grid=(batch, nblk),
        in_specs=[rows, const(w_in), const(aux0), const(aux1), const(gain), const(w_out), const(g), const(b)],
        out_specs=(rows, pl.BlockSpec((1, heads, kd, vd), lambda bb, c: (bb, 0, 0, 0))),
        scratch_shapes=[pltpu.VMEM((tb, width), F32), pltpu.VMEM((tb, heads * vd), BF16)],
        compiler_params=_params(("arbitrary", "arbitrary")),
        name="mixer_" + mode,
    )(x, w_in, aux0, aux1, gain, w_out, g, b)


def _attn_kernel(q_ref, k_ref, v_ref, o_ref, *, heads, nbatch, tq):
    hd = q_ref.shape[1] // heads
    scale = hd ** -0.5
    units = [(nb, h) for nb in range(nbatch) for h in range(heads)]

    def head_rows(ref, nb, h):
        nt = hd // 128
        period = heads * nt
        mlen = ref.shape[1] // period
        parts = [ref[nb, pl.ds(t * heads + h, mlen, stride=period), :] for t in range(nt)]
        return jnp.concatenate(parts, axis=1).astype(BF16)

    scores = [_dot_nt(q_ref[nb * tq:(nb + 1) * tq, h * hd:(h + 1) * hd].astype(BF16),
                      head_rows(k_ref, nb, h)) * scale for nb, h in units]
    probs = []
    for s in scores:
        p = jnp.exp(s - jnp.max(s, axis=-1, keepdims=True))
        probs.append((p / jnp.sum(p, axis=-1, keepdims=True)).astype(BF16))
    for (nb, h), p in zip(units, probs):
        o_ref[nb * tq:(nb + 1) * tq, h * hd:(h + 1) * hd] = _dot(p, head_rows(v_ref, nb, h)).astype(o_ref.dtype)


def _xattn_kernel(x_ref, wq_ref, k_ref, v_ref, wo_ref, g_ref, b_ref, rw_ref, rb_ref, y_ref, cls_ref, *,
                  heads, alpha):
    x = x_ref[...]
    hd = x.shape[1] // heads
    scale = hd ** -0.5
    q = _dot(x.astype(BF16), wq_ref[...].astype(BF16)).astype(BF16)
    nt = hd // 128
    period = heads * nt
    mlen = k_ref.shape[0] // period

    def head_rows(ref, h):
        parts = [ref[pl.ds(t * heads + h, mlen, stride=period), :] for t in range(nt)]
        return jnp.concatenate(parts, axis=1).astype(BF16)

    scores = [_dot_nt(q[:, h * hd:(h + 1) * hd], head_rows(k_ref, h)) * scale for h in range(heads)]
    probs = []
    for s in scores:
        p = jnp.exp(s - jnp.max(s, axis=-1, keepdims=True))
        probs.append((p / jnp.sum(p, axis=-1, keepdims=True)).astype(BF16))
    o = jnp.concatenate([_dot(p, head_rows(v_ref, h)).astype(BF16) for h, p in enumerate(probs)], axis=1)
    y = _layer_norm(alpha * x + _dot(o, wo_ref[...].astype(BF16)), g_ref[...], b_ref[...])
    y_ref[...] = y
    cls_ref[...] = _routing_class(y, rw_ref[...], rb_ref[...])


def _xattn(x, wq, mem_k, mem_v, wo, layer, g, b, router_wt, bias_col, alpha, *, batch, seq, heads, tq):
    n, d = x.shape
    nblk = seq // tq
    const = lambda a: pl.BlockSpec(a.shape, lambda bb, i: (0, 0))
    rows = pl.BlockSpec((tq, d), lambda bb, i: (bb * nblk + i, 0))
    mem = pl.BlockSpec((None, None) + mem_k.shape[2:], lambda bb, i: (layer, bb, 0, 0))
    return pl.pallas_call(
        functools.partial(_xattn_kernel, heads=heads, alpha=alpha),
        out_shape=(jax.ShapeDtypeStruct((n, d), F32), jax.ShapeDtypeStruct((1, n), I32)),
        grid=(batch, nblk),
        in_specs=[rows, const(wq), mem, mem, const(wo), const(g), const(b), const(router_wt), const(bias_col)],
        out_specs=(rows, pl.BlockSpec((1, tq), lambda bb, i: (0, bb * nblk + i))),
        compiler_params=_params(("arbitrary", "arbitrary")),
        name="xattn_block",
    )(x, wq, mem_k, mem_v, wo, g, b, router_wt, bias_col)


def _interleaved_rows(mem):
    nl, b, m, heads, hd = mem.shape
    nt = hd // 128
    return mem.reshape(nl, b, m, heads, nt, 128).transpose(0, 1, 2, 4, 3, 5).reshape(nl, b, m * nt * heads, 128)


def _attention(q, mem_k, mem_v, layer, *, batch, seq, heads):
    n, d = q.shape
    nbatch = ATTN_SEQS
    mem_spec = pl.BlockSpec((None, nbatch) + mem_k.shape[2:], lambda b: (layer, b, 0, 0))
    qspec = pl.BlockSpec((nbatch * seq, d), lambda b: (b, 0))
    return pl.pallas_call(
        functools.partial(_attn_kernel, heads=heads, nbatch=nbatch, tq=seq),
        out_shape=jax.ShapeDtypeStruct((n, d), q.dtype),
        grid=(batch // nbatch,),
        in_specs=[qspec, mem_spec, mem_spec],
        out_specs=qspec,
        compiler_params=_params(("arbitrary",)),
        name="mem_attention",
    )(q, mem_k, mem_v)


def _routing_class(x, wt, bias_col):
    logits = _dot_precise(wt, x, _dot_nt)
    sel = jax.nn.sigmoid(logits) + bias_col
    rows = [sel[e:e + 1, :] for e in range(N_GROUPS * GROUP_SIZE)]

    def first_argmax(vals):
        best_v, best_i = vals[0], jnp.zeros(vals[0].shape, I32)
        for i in range(1, len(vals)):
            better = vals[i] > best_v
            best_i = jnp.where(better, i, best_i)
            best_v = jnp.where(better, vals[i], best_v)
        return best_i

    group_scores = []
    for gi in range(N_GROUPS):
        a = rows[gi * GROUP_SIZE:(gi + 1) * GROUP_SIZE]
        top2 = None
        for i in range(GROUP_SIZE):
            for j in range(i + 1, GROUP_SIZE):
                s = a[i] + a[j]
                top2 = s if top2 is None else jnp.maximum(top2, s)
        group_scores.append(top2)
    best = first_argmax(group_scores)
    cand = []
    for j in range(GROUP_SIZE):
        cj = rows[j]
        for gi in range(1, N_GROUPS):
            cj = jnp.where(best == gi, rows[gi * GROUP_SIZE + j], cj)
        cand.append(cj)
    i1 = first_argmax(cand)
    i2 = first_argmax([jnp.where(i1 == j, -jnp.inf, cand[j]) for j in range(GROUP_SIZE)])
    lo = jnp.minimum(i1, i2)
    hi = jnp.maximum(i1, i2)
    pair = jnp.where(lo == 0, hi - 1, jnp.where(lo == 1, 6 - hi, 5))
    return best * 6 + pair


def _router_kernel(x_ref, wt_ref, bias_ref, cls_ref):
    cls_ref[...] = _routing_class(x_ref[...], wt_ref[...], bias_ref[...])


def _router(x, router_wt, bias_col, tm):
    n, d = x.shape
    ne = router_wt.shape[0]
    return pl.pallas_call(
        _router_kernel,
        out_shape=jax.ShapeDtypeStruct((1, n), I32),
        grid=(n // tm,),
        in_specs=[pl.BlockSpec((tm, d), lambda i: (i, 0)),
                  pl.BlockSpec((ne, d), lambda i: (0, 0)),
                  pl.BlockSpec((ne, 1), lambda i: (0, 0))],
        out_specs=pl.BlockSpec((1, tm), lambda i: (0, i)),
        compiler_params=_params(("arbitrary",)),
        name="router",
    )(x, router_wt, bias_col)


_PAIR_EXPERTS = ((0, 1), (0, 2), (0, 3), (1, 3), (1, 2), (3, 2))


def _moe_plan(cls, tile):
    n = cls.shape[0]
    ncls = N_GROUPS * 6
    max_tiles = n // tile + ncls
    shift = max(n - 1, 1).bit_length()
    keys = jnp.sort(cls * (1 << shift) + jnp.arange(n, dtype=I32))
    src = keys & ((1 << shift) - 1)
    cid = jnp.arange(ncls, dtype=I32)
    count = jnp.sum((cls[None, :] == cid[:, None]).astype(I32), axis=1)
    cstart = jnp.cumsum(count) - count
    ntile = (count + tile - 1) // tile
    tend = jnp.cumsum(ntile)
    tid = jnp.arange(max_tiles, dtype=I32)
    n_used = tend[-1]
    tcls = jnp.sum((tid[:, None] >= tend[None, :]).astype(I32), axis=1)
    last_cls = jnp.sum((n_used - 1 >= tend).astype(I32))
    tcls = jnp.where(tid < n_used, tcls, last_cls)
    within = tid - (tend - ntile)[tcls]
    tstart = cstart[tcls] + within * tile
    tcnt = jnp.clip(count[tcls] - within * tile, 0, tile)
    tcnt = jnp.where(tid < n_used, tcnt, 0)
    grp = tcls // 6
    e_lo = grp * GROUP_SIZE + jnp.asarray([p[0] for p in _PAIR_EXPERTS], I32)[tcls % 6]
    e_hi = grp * GROUP_SIZE + jnp.asarray([p[1] for p in _PAIR_EXPERTS], I32)[tcls % 6]
    return src, tstart, tcnt, e_lo, e_hi, n_used.reshape(1)


def _moe_kernel(src_ref, tstart_ref, tcnt_ref, elo_ref, ehi_ref, nused_ref,
                x_hbm, rw_lo_ref, rw_hi_ref, wg_lo_ref, wu_lo_ref, wd_lo_ref,
                wg_hi_ref, wu_hi_ref, wd_hi_ref, g_ref, b_ref,
                out_hbm, xbuf, ybuf, gsem, ssem, *, alpha):
    j = pl.program_id(0)
    n_used = nused_ref[0]
    parity = lax.rem(j, 2)
    ngroups, _, d = xbuf.shape[1:]
    block = MOE_COPY_BLOCK

    def for_rows(cnt, fn):
        nblk = lax.shift_right_logical(cnt, block.bit_length() - 1)
        for blk in range(ngroups * 8 // block):
            @pl.when(blk < nblk)
            def _():
                for u in range(block):
                    r = blk * block + u
                    fn(r // 8, r % 8, r, u % 2)

        def single(r, c):
            fn(lax.shift_right_logical(r, 3), jnp.bitwise_and(r, 7), r, 0)
            return c

        lax.fori_loop(nblk * block, cnt, single, 0)

    def wait_rows(cnt, group_copy, row_copy):
        ngrp = lax.shift_right_logical(cnt, 3)

        @pl.when(ngrp > 0)
        def _():
            group_copy(ngrp).wait()

        def single(r, c):
            row_copy(lax.shift_right_logical(r, 3), jnp.bitwise_and(r, 7)).wait()
            return c
        lax.fori_loop(ngrp * 8, cnt, single, 0)

    def start_gather(t, s):
        start = tstart_ref[t]

        def fn(gi, u, r, prio):
            tok = src_ref[start + r]
            pltpu.make_async_copy(x_hbm.at[pl.ds(tok, 1)], xbuf.at[s, gi, pl.ds(u, 1)],
                                  gsem.at[s]).start(priority=prio)
        for_rows(tcnt_ref[t], fn)

    def wait_gather(t, s):
        wait_rows(tcnt_ref[t],
                  lambda n: pltpu.make_async_copy(xbuf.at[s, pl.ds(0, n)], xbuf.at[s, pl.ds(0, n)], gsem.at[s]),
                  lambda gi, u: pltpu.make_async_copy(xbuf.at[s, gi, pl.ds(u, 1)], xbuf.at[s, gi, pl.ds(u, 1)],
                                                      gsem.at[s]))

    def start_scatter(t, s):
        start = tstart_ref[t]

        def fn(gi, u, r, prio):
            tok = src_ref[start + r]
            pltpu.make_async_copy(ybuf.at[s, gi, pl.ds(u, 1)], out_hbm.at[pl.ds(tok, 1)],
                                  ssem.at[s]).start(priority=prio)
        for_rows(tcnt_ref[t], fn)

    def wait_scatter(t, s):
        wait_rows(tcnt_ref[t],
                  lambda n: pltpu.make_async_copy(ybuf.at[s, pl.ds(0, n)], ybuf.at[s, pl.ds(0, n)], ssem.at[s]),
                  lambda gi, u: pltpu.make_async_copy(ybuf.at[s, gi, pl.ds(u, 1)], ybuf.at[s, gi, pl.ds(u, 1)],
                                                      ssem.at[s]))

    @pl.when(j == 0)
    def _():
        xbuf[...] = jnp.zeros(xbuf.shape, F32)
        start_gather(0, 0)

    def step(slot):
        @pl.when(j + 1 < n_used)
        def _():
            start_gather(j + 1, 1 - slot)

        wait_gather(j, slot)

        @pl.when(j >= 2)
        def _():
            wait_scatter(j - 2, slot)

        x = xbuf[slot].reshape(ngroups * 8, d)
        xb = x.astype(BF16)

        def expert(rw_ref, wg_ref, wu_ref, wd_ref):
            h = _dot(xb, wg_ref[...].astype(BF16))
            h = h * _sigmoid(h) * _dot(xb, wu_ref[...].astype(BF16))
            y = _dot(h.astype(BF16), wd_ref[...].astype(BF16))
            score = _sigmoid(jnp.sum(x * rw_ref[...], axis=-1, keepdims=True))
            return y, score

        y_lo, s_lo = expert(rw_lo_ref, wg_lo_ref, wu_lo_ref, wd_lo_ref)
        y_hi, s_hi = expert(rw_hi_ref, wg_hi_ref, wu_hi_ref, wd_hi_ref)
        tot = s_lo + s_hi
        m = (s_lo / tot) * y_lo + (s_hi / tot) * y_hi
        ybuf[slot] = _layer_norm(alpha * x + m, g_ref[...], b_ref[...]).reshape(ngroups, 8, d)
        start_scatter(j, slot)

        @pl.when(j == n_used - 1)
        def _():
            @pl.when(j >= 1)
            def _():
                wait_scatter(j - 1, 1 - slot)
            wait_scatter(j, slot)

    for s in range(2):
        pl.when(jnp.logical_and(j < n_used, parity == s))(functools.partial(step, s))


def _moe(x, cls, router_wt3, wg, wu, wd, layer, g, b, alpha):
    n, d = x.shape
    de = wg.shape[3]
    tile = MOE_TILE if n >= 32 * MOE_TILE else MOE_TILE // 4
    src, tstart, tcnt, e_lo, e_hi, n_used = _moe_plan(cls, tile)
    max_tiles = tstart.shape[0]

    def by_lo(shape):
        return pl.BlockSpec(shape, lambda j, s, ts, tc, el, eh, nu: (el[j], 0, 0))

    def by_hi(shape):
        return pl.BlockSpec(shape, lambda j, s, ts, tc, el, eh, nu: (eh[j], 0, 0))

    def w_lo(r, c):
        return pl.BlockSpec((None, None, r, c), lambda j, s, ts, tc, el, eh, nu: (layer, el[j], 0, 0))

    def w_hi(r, c):
        return pl.BlockSpec((None, None, r, c), lambda j, s, ts, tc, el, eh, nu: (layer, eh[j], 0, 0))

    row = pl.BlockSpec((1, d), lambda j, s, ts, tc, el, eh, nu: (0, 0))
    grid_spec = pltpu.PrefetchScalarGridSpec(
        num_scalar_prefetch=6,
        grid=(max_tiles,),
        in_specs=[pl.BlockSpec(memory_space=pl.ANY),
                  by_lo((None, 1, d)), by_hi((None, 1, d)),
                  w_lo(d, de), w_lo(d, de), w_lo(de, d),
                  w_hi(d, de), w_hi(d, de), w_hi(de, d),
                  row, row],
        out_specs=pl.BlockSpec(memory_space=pl.ANY),
        scratch_shapes=[pltpu.VMEM((2, tile // 8, 8, d), F32), pltpu.VMEM((2, tile // 8, 8, d), F32),
                        pltpu.SemaphoreType.DMA((2,)), pltpu.SemaphoreType.DMA((2,))],
    )
    return pl.pallas_call(
        functools.partial(_moe_kernel, alpha=alpha),
        out_shape=jax.ShapeDtypeStruct((n, d), F32),
        grid_spec=grid_spec,
        compiler_params=_params(("arbitrary",)),
        name="grouped_moe",
    )(src, tstart, tcnt, e_lo, e_hi, n_used,
      x, router_wt3, router_wt3, wg, wu, wd, wg, wu, wd, g, b)


def kernel(x_prompt, x_sample, state_hgrn, state_gla, cache_mem_k, cache_mem_v, mem_prompt, hgrn_w_in, hgrn_lb_logits, hgrn_norm_g, hgrn_w_out, gla_w_in, gla_w_gk2, gla_b_gk2, gla_norm_g, gla_w_out, xattn_w_q, xattn_w_kv, xattn_w_o, router_w, router_bias, moe_w_gate, moe_w_up, moe_w_down, ln_g, ln_b):
    batch, seq, d = x_prompt.shape
    dec_batch, dec_seq, _ = x_sample.shape
    depth = ln_g.shape[0]
    alpha = (2 * depth) ** 0.25
    a_heads, a_key, a_val = state_hgrn.shape[2:]
    b_heads, b_key, b_val = state_gla.shape[2:]
    mem_len, x_heads = cache_mem_k.shape[2], cache_mem_k.shape[3]
    rank = gla_w_gk2.shape[1]
    b_main = 2 * b_heads * b_key + b_heads * b_val + d

    hgrn_w_in_b = hgrn_w_in.astype(BF16)
    hgrn_w_out_b = hgrn_w_out.astype(BF16)
    gla_w_in_b = jnp.concatenate(
        [gla_w_in, jnp.zeros(gla_w_in.shape[:2] + (GLA_RANK_PAD - rank,), F32)], axis=-1).astype(BF16)
    gla_w_gk2_p = jnp.concatenate(
        [gla_w_gk2, jnp.zeros((gla_w_gk2.shape[0], GLA_RANK_PAD - rank, gla_w_gk2.shape[2]), F32)], axis=1)
    gla_w_out_b = gla_w_out.astype(BF16)
    w_q_b, w_o_b = xattn_w_q, xattn_w_o
    wg_b, wu_b, wd_b = moe_w_gate, moe_w_up, moe_w_down
    router_wt = router_w.T
    router_wt3 = router_wt[:, None, :]
    bias_col = router_bias[:, None]
    assert b_main + rank == gla_w_in.shape[2]

    mem_k_p, mem_v_p = _kv_proj(mem_prompt.reshape(batch * mem_len, d), xattn_w_kv, ROW_TILE, x_heads)
    rows_per_seq = mem_len * d // 128
    mem_k_p = mem_k_p.reshape(depth, batch, rows_per_seq, 128)
    mem_v_p = mem_v_p.reshape(depth, batch, rows_per_seq, 128)

    def head_major(mem):
        nt = d // x_heads // 128
        return mem.reshape(depth, batch, mem_len, nt, x_heads, 128).transpose(0, 1, 2, 4, 3, 5).reshape(
            depth, batch, mem_len, x_heads, d // x_heads)
    mem_k_prompt = head_major(mem_k_p)
    mem_v_prompt = head_major(mem_v_p)

    def run_trunk(x3, states_a, states_b, mem_ks, mem_vs):
        nb, ns, _ = x3.shape
        x = x3.reshape(nb * ns, d)
        new_a, new_b = [], []
        for l in range(depth):
            j = l // 2
            row = lambda a, i: a[l, i][None, :]
            if l % 2 == 0:
                w_in, w_out, states, new = hgrn_w_in_b[j], hgrn_w_out_b[j], states_a, new_a
                aux = (hgrn_lb_logits, hgrn_lb_logits[:1], hgrn_norm_g[j][None, :])
                cfg = dict(mode="hgrn", batch=nb, seq=ns, heads=a_heads, kd=a_key, vd=a_val, lb_row=l)
            else:
                w_in, w_out, states, new = gla_w_in_b[j], gla_w_out_b[j], states_b, new_b
                aux = (gla_w_gk2_p[j], gla_b_gk2[j][None, :], gla_norm_g[j][None, :])
                cfg = dict(mode="gla", batch=nb, seq=ns, heads=b_heads, kd=b_key, vd=b_val)
            if states is None:
                x, s = _mixer(x, w_in, *aux, w_out, row(ln_g, 0), row(ln_b, 0), alpha, **cfg)
            else:
                proj = _linear(x, w_in, F32, ROW_TILE)
                o, s = _recurrence(proj, states[j], *aux, **cfg)
                x = _linear_res_ln(o, w_out, x, row(ln_g, 0), row(ln_b, 0), alpha, ROW_TILE)
            new.append(s)
            if ns % XATTN_TILE == 0:
                x, cls = _xattn(x, w_q_b[l], mem_ks, mem_vs, w_o_b[l], l, row(ln_g, 1), row(ln_b, 1),
                                router_wt, bias_col, alpha, batch=nb, seq=ns, heads=x_heads, tq=XATTN_TILE)
            else:
                q = _linear(x, w_q_b[l], F32, ROW_TILE)
                c = _attention(q, mem_ks, mem_vs, l, batch=nb, seq=ns, heads=x_heads)
                x = _linear_res_ln(c, w_o_b[l], x, row(ln_g, 1), row(ln_b, 1), alpha, ROW_TILE)
                cls = _router(x, router_wt, bias_col, ROW_TILE)
            x = _moe(x, cls[0], router_wt3, wg_b, wu_b, wd_b, l, row(ln_g, 2), row(ln_b, 2), alpha)
        return x.reshape(nb, ns, d), jnp.stack(new_a), jnp.stack(new_b)

    y_prompt, state_hgrn_prompt, state_gla_prompt = run_trunk(x_prompt, None, None, mem_k_p, mem_v_p)
    y_sample, state_hgrn_sample, state_gla_sample = run_trunk(
        x_sample, state_hgrn, state_gla, _interleaved_rows(cache_mem_k), _interleaved_rows(cache_mem_v))
    return (y_prompt, y_sample, state_hgrn_prompt, state_gla_prompt, mem_k_prompt, mem_v_prompt,
            state_hgrn_sample, state_gla_sample)
```
